```python
import math
import jax, jax.numpy as jnp
from jax import lax
import numpy as np

D_MODEL = 1024
BATCH = 8
SEQ = 2048
DEPTH = 1
DEC_BATCH = 128
DEC_SEQ = 4
PAST_LEN = 16384
PAGE_SIZE = 128

H_A = 4
DV_A = D_MODEL // H_A
DK_A = DV_A // 2
QK_A = H_A * DK_A
CONV_W = 4
H_B = 4
DV_B = D_MODEL // H_B
DK_B = DV_B // 2
QK_B = H_B * DK_B
GLA_RANK = 16
GLA_TAU = 16.0
CHUNK = 64
D_FF = ((8 * D_MODEL // 3 + 255) // 256) * 256
EPS = 1e-6
SPLIT_SIZES = (2 * QK_A, H_A * DV_A, H_A * DV_A, H_A, H_A,
               QK_B, QK_B, H_B * DV_B, H_B * DV_B, GLA_RANK,
               D_MODEL, D_MODEL)
N_IN = 2 * QK_A + 2 * H_A * DV_A + 2 * H_A + 2 * QK_B + 2 * H_B * DV_B + GLA_RANK + 2 * D_MODEL

kernel_name = "hybrid_mlstm_gla_adaln_step"


def _rmsnorm(x, g):
    xf = x.astype(jnp.float32)
    y = xf * lax.rsqrt(jnp.mean(xf * xf, axis=-1, keepdims=True) + EPS)
    return y.astype(x.dtype) * g


def _to_chunks(a, L):
    B, T = a.shape[:2]
    a = a.reshape((B, T // L, L) + a.shape[2:])
    return jnp.transpose(a, (1, 0, 3, 2) + tuple(range(4, a.ndim)))


def _from_chunks(h, B, T):
    h = jnp.transpose(h, (1, 0, 3, 2, 4))
    return h.reshape(B, T, h.shape[3], h.shape[4])


def _mlstm(q, k, v, ig, lf, C0, n0, m0):
    B, T = q.shape[:2]
    L = math.gcd(T, CHUNK)
    mask = jnp.tril(jnp.ones((L, L), dtype=bool))
    xs = (_to_chunks(q, L), _to_chunks(k, L), _to_chunks(v, L), _to_chunks(ig, L), _to_chunks(lf, L))

    def step(carry, inp):
        C, n, m = carry
        qc, kc, vc, ic, fc = inp
        b = jnp.cumsum(fc, axis=-1)
        D = jnp.where(mask, b[..., :, None] - b[..., None, :] + ic[..., None, :], -jnp.inf)
        inter = b + m[..., None]
        m_t = jnp.maximum(jnp.max(D, axis=-1), inter)
        w = jnp.exp(D - m_t[..., None])
        wi = jnp.exp(inter - m_t)
        s = jnp.einsum('bhtd,bhsd->bhts', qc, kc) * w
        num = jnp.einsum('bhts,bhsv->bhtv', s, vc) + wi[..., None] * jnp.einsum('bhtd,bhdv->bhtv', qc, C)
        den = jnp.sum(s, axis=-1) + wi * jnp.einsum('bhtd,bhd->bht', qc, n)
        h = num / jnp.maximum(jnp.abs(den), jnp.exp(-m_t))[..., None]
        bL = b[..., -1]
        dec = bL[..., None] - b + ic
        m_new = jnp.maximum(bL + m, jnp.max(dec, axis=-1))
        wk = jnp.exp(dec - m_new[..., None])
        wC = jnp.exp(bL + m - m_new)
        C_new = wC[..., None, None] * C + jnp.einsum('bhs,bhsd,bhsv->bhdv', wk, kc, vc)
        n_new = wC[..., None] * n + jnp.einsum('bhs,bhsd->bhd', wk, kc)
        return (C_new, n_new, m_new), h

    (C1, n1, m1), hs = lax.scan(step, (C0, n0, m0), xs)
    return _from_chunks(hs, B, T), C1, n1, m1


def _gla(q, k, v, la, S0):
    B, T = q.shape[:2]
    L = math.gcd(T, CHUNK)
    mask = jnp.tril(jnp.ones((L, L), dtype=bool))[..., None]
    xs = (_to_chunks(q, L), _to_chunks(k, L), _to_chunks(v, L), _to_chunks(la, L))

    def step(S, inp):
        qc, kc, vc, lc = inp
        b = jnp.cumsum(lc, axis=2)
        decay = jnp.exp(jnp.where(mask, b[:, :, :, None, :] - b[:, :, None, :, :], -jnp.inf))
        A = jnp.einsum('bhtd,bhsd,bhtsd->bhts', qc, kc, decay)
        o = jnp.einsum('bhts,bhsv->bhtv', A, vc) + jnp.einsum('bhtd,bhdv->bhtv', qc * jnp.exp(b), S)
        bL = b[:, :, -1, :]
        S_new = jnp.exp(bL)[..., None] * S + jnp.einsum('bhsd,bhsv->bhdv', kc * jnp.exp(bL[:, :, None, :] - b), vc)
        return S_new, o

    S1, os_ = lax.scan(step, S0, xs)
    return _from_chunks(os_, B, T), S1


def _head_norm(h, g):
    hf = h * lax.rsqrt(jnp.mean(h * h, axis=-1, keepdims=True) + EPS)
    return hf * g


def _layer(x, c, state, p):
    conv_buf, C0, n0, m0, S0 = state
    (g_norm1, g_norm2, w_ada, b_ada, w_in, conv_w, conv_b, b_igate, b_fgate, g_head_a,
     w_alpha, b_alpha, g_head_b, w_out, w_ffn_gate, w_ffn_up, w_ffn_down) = p
    B, T, _ = x.shape
    f32 = jnp.float32
    ada = (c @ w_ada + b_ada)[:, None, :]
    shift1, scale1, gate1, shift2, scale2, gate2 = jnp.split(ada, 6, axis=-1)

    h = _rmsnorm(x, g_norm1) * (1.0 + scale1) + shift1
    z = h @ w_in
    idx = list(np.cumsum(SPLIT_SIZES)[:-1])
    qk_a, v_a, o_a, i_a, f_a, q_b, k_b, v_b, r_b, lr_b, g_a, g_b = jnp.split(z, idx, axis=-1)

    xpad = jnp.concatenate([conv_buf.astype(qk_a.dtype), qk_a], axis=1)
    conv = conv_b + sum(xpad[:, j:j + T] * conv_w[j] for j in range(CONV_W))
    new_conv = xpad[:, T:]
    qk = jax.nn.silu(conv).astype(f32)
    qa = qk[..., :QK_A].reshape(B, T, H_A, DK_A) * (DK_A ** -0.5)
    ka = qk[..., QK_A:].reshape(B, T, H_A, DK_A)
    va = v_a.astype(f32).reshape(B, T, H_A, DV_A)
    ig = (i_a + b_igate).astype(f32)
    lf = jax.nn.log_sigmoid((f_a + b_fgate).astype(f32))
    ha, C1, n1, m1 = _mlstm(qa, ka, va, ig, lf, C0.astype(f32), n0.astype(f32), m0.astype(f32))
    ha = _head_norm(ha, g_head_a.astype(f32)).reshape(B, T, D_MODEL).astype(x.dtype) * jax.nn.sigmoid(o_a)

    qb = q_b.astype(f32).reshape(B, T, H_B, DK_B) * (DK_B ** -0.5)
    kb = k_b.astype(f32).reshape(B, T, H_B, DK_B)
    vb = v_b.astype(f32).reshape(B, T, H_B, DV_B)
    la = (jax.nn.log_sigmoid((lr_b @ w_alpha + b_alpha).astype(f32)) / GLA_TAU).reshape(B, T, H_B, DK_B)
    hb, S1 = _gla(qb, kb, vb, la, S0.astype(f32))
    hb = _head_norm(hb, g_head_b.astype(f32)).reshape(B, T, D_MODEL).astype(x.dtype) * jax.nn.silu(r_b)

    y = (jax.nn.sigmoid(g_a) * ha + jax.nn.sigmoid(g_b) * hb) @ w_out
    x = x + gate1 * y

    h2 = _rmsnorm(x, g_norm2) * (1.0 + scale2) + shift2
    ffn = (jax.nn.silu(h2 @ w_ffn_gate) * (h2 @ w_ffn_up)) @ w_ffn_down
    x = x + gate2 * ffn
    dt = x.dtype
    return x, (new_conv.astype(dt), C1.astype(dt), n1.astype(dt), m1.astype(dt), S1.astype(dt))


def setup_inputs(seed: int = 0) -> dict:
    key = jax.random.key(seed)
    ks = jax.random.split(key, 32)
    nrm = jax.random.normal
    D = D_MODEL
    inp = {}
    inp["x_prompt"] = nrm(ks[0], (BATCH, SEQ, D), jnp.float32)
    inp["x_sample"] = nrm(ks[1], (DEC_BATCH, DEC_SEQ, D), jnp.float32)
    inp["state_conv"] = nrm(ks[2], (DEPTH, DEC_BATCH, CONV_W - 1, 2 * QK_A), jnp.float32)
    inp["state_C"] = 0.1 * nrm(ks[3], (DEPTH, DEC_BATCH, H_A, DK_A, DV_A), jnp.float32)
    inp["state_n"] = nrm(ks[4], (DEPTH, DEC_BATCH, H_A, DK_A), jnp.float32)
    inp["state_m"] = 0.5 * nrm(ks[5], (DEPTH, DEC_BATCH, H_A), jnp.float32)
    inp["state_S"] = (DK_B ** -0.5) * nrm(ks[6], (DEPTH, DEC_BATCH, H_B, DK_B, DV_B), jnp.float32)
    inp["c_prompt"] = nrm(ks[7], (BATCH, D), jnp.float32)
    inp["c_sample"] = nrm(ks[8], (DEC_BATCH, D), jnp.float32)
    inp["g_norm1"] = 1.0 + 0.05 * nrm(ks[9], (DEPTH, D), jnp.float32)
    inp["g_norm2"] = 1.0 + 0.05 * nrm(ks[10], (DEPTH, D), jnp.float32)
    inp["w_ada"] = 0.5 * D ** -0.5 * nrm(ks[11], (DEPTH, D, 6 * D), jnp.float32)
    inp["b_ada"] = 0.02 * nrm(ks[12], (DEPTH, 6 * D), jnp.float32)
    inp["w_in"] = D ** -0.5 * nrm(ks[13], (DEPTH, D, N_IN), jnp.float32)
    inp["conv_w"] = 0.5 * nrm(ks[14], (DEPTH, CONV_W, 2 * QK_A), jnp.float32)
    inp["conv_b"] = 0.02 * nrm(ks[15], (DEPTH, 2 * QK_A), jnp.float32)
    inp["b_igate"] = 0.1 * nrm(ks[16], (DEPTH, H_A), jnp.float32)
    inp["b_fgate"] = jnp.linspace(3.0, 6.0, H_A, dtype=jnp.float32)[None, :] + 0.1 * nrm(ks[17], (DEPTH, H_A), jnp.float32)
    inp["g_head_a"] = 1.0 + 0.05 * nrm(ks[18], (DEPTH, H_A, DV_A), jnp.float32)
    inp["w_alpha"] = GLA_RANK ** -0.5 * nrm(ks[19], (DEPTH, GLA_RANK, QK_B), jnp.float32)
    inp["b_alpha"] = 2.0 + 0.1 * nrm(ks[20], (DEPTH, QK_B), jnp.float32)
    inp["g_head_b"] = 1.0 + 0.05 * nrm(ks[21], (DEPTH, H_B, DV_B), jnp.float32)
    inp["w_out"] = D ** -0.5 * nrm(ks[22], (DEPTH, D, D), jnp.float32)
    inp["w_ffn_gate"] = D ** -0.5 * nrm(ks[23], (DEPTH, D, D_FF), jnp.float32)
    inp["w_ffn_up"] = D ** -0.5 * nrm(ks[24], (DEPTH, D, D_FF), jnp.float32)
    inp["w_ffn_down"] = D_FF ** -0.5 * nrm(ks[25], (DEPTH, D_FF, D), jnp.float32)
    inp["g_final"] = 1.0 + 0.05 * nrm(ks[26], (D,), jnp.float32)
    return inp


def reference(x_prompt, x_sample, state_conv, state_C, state_n, state_m, state_S, c_prompt, c_sample,
              g_norm1, g_norm2, w_ada, b_ada, w_in, conv_w, conv_b, b_igate, b_fgate, g_head_a,
              w_alpha, b_alpha, g_head_b, w_out, w_ffn_gate, w_ffn_up, w_ffn_down, g_final):
    xp, xs = x_prompt, x_sample
    Bp = xp.shape[0]
    dt = xp.dtype
    zero_state = (jnp.zeros((Bp, CONV_W - 1, 2 * QK_A), dt), jnp.zeros((Bp, H_A, DK_A, DV_A), dt),
                  jnp.zeros((Bp, H_A, DK_A), dt), jnp.zeros((Bp, H_A), dt), jnp.zeros((Bp, H_B, DK_B, DV_B), dt))
    new_p = [[] for _ in range(5)]
    new_s = [[] for _ in range(5)]
    for l in range(DEPTH):
        p = (g_norm1[l], g_norm2[l], w_ada[l], b_ada[l], w_in[l], conv_w[l], conv_b[l], b_igate[l], b_fgate[l],
             g_head_a[l], w_alpha[l], b_alpha[l], g_head_b[l], w_out[l], w_ffn_gate[l], w_ffn_up[l], w_ffn_down[l])
        xp, st_p = _layer(xp, c_prompt, zero_state, p)
        xs, st_s = _layer(xs, c_sample, (state_conv[l], state_C[l], state_n[l], state_m[l], state_S[l]), p)
        for j in range(5):
            new_p[j].append(st_p[j])
            new_s[j].append(st_s[j])
    y_prompt = _rmsnorm(xp, g_final)
    y_sample = _rmsnorm(xs, g_final)
    conv_p, C_p, n_p, m_p, S_p = [jnp.stack(a, axis=0) for a in new_p]
    conv_s, C_s, n_s, m_s, S_s = [jnp.stack(a, axis=0) for a in new_s]
    return (y_prompt, y_sample, conv_p, C_p, n_p, m_p, S_p, conv_s, C_s, n_s, m_s, S_s)
```

```python
import functools

import jax
import jax.numpy as jnp
from jax import lax
from jax.experimental import pallas as pl
from jax.experimental.pallas import tpu as pltpu

f32 = jnp.float32
bf16 = jnp.bfloat16

D_MODEL = 1024
N_HEADS = 4
DK = 128
DV = 256
QK = N_HEADS * DK
CONV_W = 4
GLA_RANK = 16
GLA_TAU = 16.0
D_FF = 2816
EPS = 1e-6
N_ADA = 6

N_BIG = 8192
N_SMALL = 128
OFF_QKA, OFF_VA, OFF_OA, OFF_QB, OFF_KB, OFF_VB, OFF_RB, OFF_GA, OFF_GB = (
    0, 1024, 2048, 3072, 3584, 4096, 5120, 6144, 7168)
LANE_IG, LANE_FG, LANE_LR = 0, N_HEADS, 2 * N_HEADS

SUBLANES = 8
NEG = -1e30
V7X_VMEM_BYTES = 64 * 1024 * 1024
VMEM_LIMIT_BYTES = V7X_VMEM_BYTES - 8 * 1024 * 1024
FFN_COLS = 256
PROJ_COLS = 1024


def _mm(a, b):
    return jnp.dot(a, b, preferred_element_type=f32)


def _mm_nt(a, b):
    return lax.dot_general(a, b, (((1,), (1,)), ((), ())), preferred_element_type=f32)


def _mm_tn(a, b):
    return lax.dot_general(a, b, (((0,), (0,)), ((), ())), preferred_element_type=f32)


def _split3(x):
    hi = x.astype(bf16)
    r = x - hi.astype(f32)
    mid = r.astype(bf16)
    lo = (r - mid.astype(f32)).astype(bf16)
    return hi, mid, lo


def _log_sigmoid(x):
    return jnp.minimum(x, 0.0) - jnp.log1p(jnp.exp(-jnp.abs(x)))


def _rms(x):
    return x * lax.rsqrt(jnp.mean(x * x, axis=-1, keepdims=True) + EPS)


def _mod_rows(mod_ref, bt, tt):
    if bt == 1:
        return mod_ref[0, 0]
    v = mod_ref[...].reshape(bt, 1, D_MODEL)
    return jnp.broadcast_to(v, (bt, tt, D_MODEL)).reshape(bt * tt, D_MODEL)


def _ada_body(c_ref, w_ref, b_ref, o_ref):
    ch, cl, _ = _split3(c_ref[...])
    wh, wl, _ = _split3(w_ref[...])
    o_ref[...] = _mm(ch, wh) + (_mm(cl, wh) + _mm(ch, wl)) + b_ref[...]


def _ada_call(c_all, w_ada, b_ada):
    m = c_all.shape[0]
    n = w_ada.shape[1]
    return pl.pallas_call(
        _ada_body,
        out_shape=jax.ShapeDtypeStruct((m, n), f32),
        grid=(n // D_MODEL,),
        in_specs=[
            pl.BlockSpec((m, D_MODEL), lambda i: (0, 0)),
            pl.BlockSpec((D_MODEL, D_MODEL), lambda i: (0, i)),
            pl.BlockSpec((1, D_MODEL), lambda i: (0, i)),
        ],
        out_specs=pl.BlockSpec((m, D_MODEL), lambda i: (0, i)),
        compiler_params=pltpu.CompilerParams(
            dimension_semantics=("arbitrary",), vmem_limit_bytes=VMEM_LIMIT_BYTES),
        name="ada_proj",
    )(c_all, w_ada, b_ada)


def _inproj_body(x_ref, shift_ref, scale_ref, g_ref, wbig_ref, wsh_ref, wsl_ref,
                 zb_ref, zs_ref, tail_ref, *, bt, tt):
    rows = bt * tt
    h = _rms(x_ref[...]) * g_ref[...]
    h = h * (1.0 + _mod_rows(scale_ref, bt, tt)) + _mod_rows(shift_ref, bt, tt)
    hh = h.astype(bf16)
    hl = (h - hh.astype(f32)).astype(bf16)
    for c in range(0, N_BIG, PROJ_COLS):
        z = _mm(hh, wbig_ref[:, c:c + PROJ_COLS])
        zb_ref[:, c:c + PROJ_COLS] = z.astype(zb_ref.dtype)
        if c == OFF_QKA:
            tail_ref[0] = z[rows - SUBLANES:rows, :]
    wsh = wsh_ref[...]
    zs_ref[...] = _mm(hh, wsh) + (_mm(hl, wsh) + _mm(hh, wsl_ref[...]))


def _const_spec(shape):
    nd = len(shape)
    return pl.BlockSpec(shape, lambda i, j: (0,) * nd, pipeline_mode=pl.Buffered(1))


def _mod_spec(bt, k):
    return pl.BlockSpec((bt, 1, 1, D_MODEL), lambda i, j: (i, k, 0, 0))


def _inproj_call(x2d, ada4, g_norm, w_big, ws_hi, ws_lo, *, nb, nt, bt, tt, zdtype):
    rows = bt * tt
    total = x2d.shape[0]
    row_spec = lambda n: pl.BlockSpec((rows, n), lambda i, j: (i * nt + j, 0))
    return pl.pallas_call(
        functools.partial(_inproj_body, bt=bt, tt=tt),
        out_shape=(
            jax.ShapeDtypeStruct((total, N_BIG), zdtype),
            jax.ShapeDtypeStruct((total, N_SMALL), f32),
            jax.ShapeDtypeStruct((nb * nt, SUBLANES, PROJ_COLS), f32),
        ),
        grid=(nb, nt),
        in_specs=[
            row_spec(D_MODEL), _mod_spec(bt, 0), _mod_spec(bt, 1),
            _const_spec((1, D_MODEL)), _const_spec((D_MODEL, N_BIG)),
            _const_spec((D_MODEL, N_SMALL)), _const_spec((D_MODEL, N_SMALL)),
        ],
        out_specs=(
            row_spec(N_BIG), row_spec(N_SMALL),
            pl.BlockSpec((1, SUBLANES, PROJ_COLS), lambda i, j: (i * nt + j, 0, 0)),
        ),
        compiler_params=pltpu.CompilerParams(
            dimension_semantics=("arbitrary", "arbitrary"), vmem_limit_bytes=VMEM_LIMIT_BYTES),
        name="in_proj",
    )(x2d, ada4, ada4, g_norm, w_big, ws_hi, ws_lo)


def _mixer_body(*refs, L, vlo, vhi, sample):
    if sample:
        (zb_ref, zs_ref, conv0_ref, C0_ref, n0_ref, m0_ref, S0_ref,
         cw_ref, cb_ref, gb_ref, wal_ref, bal_ref, gha_ref, ghb_ref,
         u_ref, Co_ref, no_ref, mo_ref, So_ref,
         qa_s, ka_s, qh_s, kh_s, x_s, bg_s, ha_s, hb_s, carry_s) = refs
        Ci_ref, ni_ref, mi_ref, Si_ref = C0_ref, n0_ref, m0_ref, S0_ref
    else:
        (zb_ref, zs_ref,
         cw_ref, cb_ref, gb_ref, wal_ref, bal_ref, gha_ref, ghb_ref,
         u_ref, Co_ref, no_ref, mo_ref, So_ref,
         qa_s, ka_s, qh_s, kh_s, x_s, bg_s, ha_s, hb_s, carry_s) = refs
        Ci_ref, ni_ref, mi_ref, Si_ref = Co_ref, no_ref, mo_ref, So_ref
    rows = zb_ref.shape[0]
    n_groups = rows // L
    log2_l = L.bit_length() - 1
    masked = not (vlo == 0 and vhi == L)
    scale = DK ** -0.5

    if not sample:
        @pl.when(pl.program_id(1) == 0)
        def _zero_state():
            Co_ref[...] = jnp.zeros(Co_ref.shape, f32)
            no_ref[...] = jnp.zeros(no_ref.shape, f32)
            mo_ref[...] = jnp.zeros(mo_ref.shape, f32)
            So_ref[...] = jnp.zeros(So_ref.shape, f32)
            carry_s[...] = jnp.zeros(carry_s.shape, f32)

    row_in = lax.broadcasted_iota(jnp.int32, (rows, 1), 0) & (L - 1)
    valid = (row_in >= vlo) & (row_in < vhi)

    zq = zb_ref[:, OFF_QKA:OFF_QKA + 2 * QK].astype(f32)
    if sample:
        zq = jnp.where(row_in < CONV_W - 1, conv0_ref[...], zq)
        carry = jnp.zeros((SUBLANES, 2 * QK), f32)
    else:
        carry = carry_s[...]
        carry_s[...] = zq[rows - SUBLANES:rows, :]
    xe = jnp.concatenate([carry, zq], axis=0)
    cw = cw_ref[...]
    conv = cb_ref[...] + zq * cw[CONV_W - 1:CONV_W]
    for k in range(CONV_W - 1):
        o = SUBLANES - (CONV_W - 1) + k
        conv = conv + xe[o:o + rows] * cw[k:k + 1]
    qk = conv * jax.nn.sigmoid(conv)
    qa_s[...] = qk[:, :QK] * scale
    ka_s[...] = qk[:, QK:]

    lane = lax.broadcasted_iota(jnp.int32, (1, N_SMALL), 1)
    zs = zs_ref[...]
    pre = zs + gb_ref[...]
    gate = jnp.where(lane < LANE_FG, pre, _log_sigmoid(pre))
    gate_f = jnp.where(valid, gate, 0.0) if masked else gate
    rr = lax.broadcasted_iota(jnp.int32, (rows, rows), 0)
    cc = lax.broadcasted_iota(jnp.int32, (rows, rows), 1)
    tril = jnp.where(((rr >> log2_l) == (cc >> log2_l)) & (cc <= rr), 1.0, 0.0).astype(bf16)

    def cumsum_groups(v):
        hi, mid, lo = _split3(v)
        return _mm(tril, hi) + (_mm(tril, mid) + _mm(tril, lo))

    x_s[...] = jnp.where(lane < LANE_FG, gate, cumsum_groups(gate_f))

    la = _log_sigmoid(_mm(zs.astype(bf16), wal_ref[...]) + bal_ref[...]) * (1.0 / GLA_TAU)
    kb = zb_ref[:, OFF_KB:OFF_KB + QK].astype(f32)
    if masked:
        la = jnp.where(valid, la, 0.0)
        kb = jnp.where(valid, kb, 0.0)
    bg = cumsum_groups(la)
    bg_s[...] = bg
    qh_s[...] = zb_ref[:, OFF_QB:OFF_QB + QK].astype(f32) * scale * jnp.exp(bg)
    kh_s[...] = kb * jnp.exp(-bg)

    ti = lax.broadcasted_iota(jnp.int32, (L, L), 0)
    si = lax.broadcasted_iota(jnp.int32, (L, L), 1)
    tri = ti >= si
    sr = lax.broadcasted_iota(jnp.int32, (2 * SUBLANES, N_SMALL), 0)
    sc = lax.broadcasted_iota(jnp.int32, (2 * SUBLANES, N_SMALL), 1)
    sel = jnp.where((sr < N_HEADS) & (sc == sr + LANE_IG), 1.0,
                    jnp.where((sr < N_HEADS) & (sc == sr + LANE_FG), -1.0, 0.0)).astype(bf16)
    eye = (lax.broadcasted_iota(jnp.int32, (DK, DK), 0) == lax.broadcasted_iota(jnp.int32, (DK, DK), 1))
    lane_h = lax.broadcasted_iota(jnp.int32, (1, N_HEADS), 1)
    lrow = lax.broadcasted_iota(jnp.int32, (1, L), 1)
    lcol = lax.broadcasted_iota(jnp.int32, (L, 1), 0)
    vrow = (lrow >= vlo) & (lrow < vhi)
    vcol = (lcol >= vlo) & (lcol < vhi)

    def group(g, _):
        r0 = pl.multiple_of(g * L, L)
        rs = pl.ds(r0, L)
        sidx = g if sample else 0
        xc = x_s[rs, :]
        xh, xm, xl = _split3(xc)
        r_rows = _mm_nt(sel, xh) + (_mm_nt(sel, xm) + _mm_nt(sel, xl))
        m_prev = mi_ref[sidx]
        m_vec = m_prev
        for h in range(N_HEADS):
            ks = slice(h * DK, (h + 1) * DK)
            vs = slice(h * DV, (h + 1) * DV)
            c_st = Ci_ref[sidx, h]
            n_st = ni_ref[sidx, pl.ds(h, 1), :]
            m_st = m_prev[:, h:h + 1]
            b_col = xc[:, LANE_FG + h:LANE_FG + h + 1]
            r_col = xc[:, LANE_IG + h:LANE_IG + h + 1] - b_col
            r_row = r_rows[h:h + 1, :]
            if masked:
                r_col = jnp.where(vcol, r_col, NEG)
                r_row = jnp.where(vrow, r_row, NEG)
            dm = jnp.where(tri, b_col + r_row, NEG)
            m_t = jnp.maximum(jnp.max(dm, axis=1, keepdims=True), b_col + m_st)
            w = jnp.exp(dm - m_t)
            wi = jnp.exp(b_col + m_st - m_t)
            q = qa_s[rs, ks]
            k = ka_s[rs, ks]
            v16 = zb_ref[rs, OFF_VA + h * DV:OFF_VA + (h + 1) * DV].astype(bf16)
            q16 = q.astype(bf16)
            s = _mm_nt(q16, k.astype(bf16)) * w
            num = _mm(s.astype(bf16), v16) + wi * _mm(q16, c_st.astype(bf16))
            den = jnp.sum(s, axis=1, keepdims=True) + wi * jnp.sum(q * n_st, axis=1, keepdims=True)
            ha_s[rs, vs] = num / jnp.maximum(jnp.abs(den), jnp.exp(-m_t))
            b_last = b_col[L - 1:L, :]
            m_new = jnp.maximum(b_last + m_st, b_last + jnp.max(r_row, axis=1, keepdims=True))
            wk = jnp.exp(b_last + r_col - m_new)
            wc = jnp.exp(b_last + m_st - m_new)
            kt = k * wk
            Co_ref[sidx, h] = wc * c_st + _mm_tn(kt.astype(bf16), v16)
            no_ref[sidx, pl.ds(h, 1), :] = wc * n_st + jnp.sum(kt, axis=0, keepdims=True)
            m_vec = jnp.where(lane_h == h, m_new, m_vec)
            s_st = Si_ref[sidx, h]
            qh = qh_s[rs, ks].astype(bf16)
            kh = kh_s[rs, ks]
            vb16 = zb_ref[rs, OFF_VB + h * DV:OFF_VB + (h + 1) * DV].astype(bf16)
            a = jnp.where(tri, _mm_nt(qh, kh.astype(bf16)), 0.0)
            hb_s[rs, vs] = _mm(a.astype(bf16), vb16) + _mm(qh, s_st.astype(bf16))
            bg_tail = bg_s[pl.ds(pl.multiple_of(r0 + L - SUBLANES, SUBLANES), SUBLANES), ks]
            eb_last = jnp.exp(bg_tail[SUBLANES - 1:SUBLANES, :])
            eb_col = jnp.sum(jnp.where(eye, eb_last, 0.0), axis=1, keepdims=True)
            So_ref[sidx, h] = eb_col * s_st + _mm_tn((kh * eb_last).astype(bf16), vb16)
        mo_ref[sidx] = m_vec
        return 0

    lax.fori_loop(0, n_groups, group, 0)

    for h in range(N_HEADS):
        vs = slice(h * DV, (h + 1) * DV)
        col = lambda off: zb_ref[:, off + h * DV:off + (h + 1) * DV].astype(f32)
        a_out = _rms(ha_s[:, vs]) * gha_ref[:, vs] * jax.nn.sigmoid(col(OFF_OA))
        rb = col(OFF_RB)
        b_out = _rms(hb_s[:, vs]) * ghb_ref[:, vs] * (rb * jax.nn.sigmoid(rb))
        u = jax.nn.sigmoid(col(OFF_GA)) * a_out + jax.nn.sigmoid(col(OFF_GB)) * b_out
        u_ref[:, vs] = u.astype(u_ref.dtype)


def _mixer_call(zb, zs, init, consts, *, nb, nt, bt, L, vlo, vhi):
    sample = init is not None
    rows = bt * L if sample else zb.shape[0] // (nb * nt)
    n_state = nb * bt
    row_spec = lambda n: pl.BlockSpec((rows, n), lambda i, j: (i * nt + j, 0))
    st4 = pl.BlockSpec((bt, N_HEADS, DK, DV), lambda i, j: (i, 0, 0, 0))
    st_n = pl.BlockSpec((bt, N_HEADS, DK), lambda i, j: (i, 0, 0))
    st_m = pl.BlockSpec((bt, 1, N_HEADS), lambda i, j: (i, 0, 0))
    in_specs = [row_spec(N_BIG), row_spec(N_SMALL)]
    args = [zb, zs]
    if sample:
        in_specs += [row_spec(2 * QK), st4, st_n, st_m, st4]
        args += list(init)
    in_specs += [_const_spec(c.shape) for c in consts]
    args += list(consts)
    scratch = [
        pltpu.VMEM((rows, QK), f32), pltpu.VMEM((rows, QK), f32),
        pltpu.VMEM((rows, QK), f32), pltpu.VMEM((rows, QK), f32),
        pltpu.VMEM((rows, N_SMALL), f32), pltpu.VMEM((rows, QK), f32),
        pltpu.VMEM((rows, D_MODEL), f32), pltpu.VMEM((rows, D_MODEL), f32),
        pltpu.VMEM((SUBLANES, 2 * QK), f32),
    ]
    return pl.pallas_call(
        functools.partial(_mixer_body, L=L, vlo=vlo, vhi=vhi, sample=sample),
        out_shape=(
            jax.ShapeDtypeStruct((zb.shape[0], D_MODEL), bf16),
            jax.ShapeDtypeStruct((n_state, N_HEADS, DK, DV), f32),
            jax.ShapeDtypeStruct((n_state, N_HEADS, DK), f32),
            jax.ShapeDtypeStruct((n_state, 1, N_HEADS), f32),
            jax.ShapeDtypeStruct((n_state, N_HEADS, DK, DV), f32),
        ),
        grid=(nb, nt),
        in_specs=in_specs,
        out_specs=(row_spec(D_MODEL), st4, st_n, st_m, st4),
        scratch_shapes=scratch,
        compiler_params=pltpu.CompilerParams(
            dimension_semantics=("arbitrary", "arbitrary"), vmem_limit_bytes=VMEM_LIMIT_BYTES),
        name="mixer_sample" if sample else "mixer_prompt",
    )(*args)


def _out_body(u_ref, x_ref, gate1_ref, shift2_ref, scale2_ref, gate2_ref, gn2_ref, gfin_ref,
              wout_ref, wg_ref, wu_ref, wd_ref, y_ref, *, bt, tt):
    x1 = x_ref[...] + _mod_rows(gate1_ref, bt, tt) * _mm(u_ref[...], wout_ref[...])
    h2 = _rms(x1) * gn2_ref[...]
    h2 = (h2 * (1.0 + _mod_rows(scale2_ref, bt, tt)) + _mod_rows(shift2_ref, bt, tt)).astype(bf16)
    acc = jnp.zeros(x1.shape, f32)
    for c in range(0, D_FF, FFN_COLS):
        g = _mm(h2, wg_ref[:, c:c + FFN_COLS])
        up = _mm(h2, wu_ref[:, c:c + FFN_COLS])
        acc = acc + _mm((g * jax.nn.sigmoid(g) * up).astype(bf16), wd_ref[c:c + FFN_COLS, :])
    x2 = x1 + _mod_rows(gate2_ref, bt, tt) * acc
    y_ref[...] = _rms(x2) * gfin_ref[...]


def _out_call(u2d, x2d, ada4, g_norm2, g_final, w_out, w_gate, w_up, w_down, *, nb, nt, bt, tt):
    rows = bt * tt
    row_spec = pl.BlockSpec((rows, D_MODEL), lambda i, j: (i * nt + j, 0))
    return pl.pallas_call(
        functools.partial(_out_body, bt=bt, tt=tt),
        out_shape=jax.ShapeDtypeStruct(x2d.shape, f32),
        grid=(nb, nt),
        in_specs=[
            row_spec, row_spec, _mod_spec(bt, 2), _mod_spec(bt, 3), _mod_spec(bt, 4), _mod_spec(bt, 5),
            _const_spec((1, D_MODEL)), _const_spec((1, D_MODEL)),
            _const_spec((D_MODEL, D_MODEL)), _const_spec((D_MODEL, D_FF)),
            _const_spec((D_MODEL, D_FF)), _const_spec((D_FF, D_MODEL)),
        ],
        out_specs=row_spec,
        compiler_params=pltpu.CompilerParams(
            dimension_semantics=("arbitrary", "arbitrary"), vmem_limit_bytes=VMEM_LIMIT_BYTES),
        name="out_ffn",
    )(u2d, x2d, ada4, ada4, ada4, ada4, g_norm2, g_final, w_out, w_gate, w_up, w_down)


PROMPT_MIX_ROWS = 256
PROMPT_CHUNK = 64
PROMPT_PROJ_ROWS = 256
PROMPT_FFN_ROWS = 512
SAMPLE_L = SUBLANES
SAMPLE_MIX_GROUPS = 8
SAMPLE_DENSE_GROUPS = 32


def _prep_layer_weights(w_in, w_alpha, b_igate, b_fgate):
    s = [0, 1024, 2048, 3072, 3076, 3080, 3592, 4104, 5128, 6152, 6168, 7192, 8216]
    w_big = jnp.concatenate([w_in[:, s[0]:s[3]], w_in[:, s[5]:s[9]], w_in[:, s[10]:s[12]]], axis=1).astype(bf16)
    w_small = jnp.concatenate(
        [w_in[:, s[3]:s[5]], w_in[:, s[9]:s[10]],
         jnp.zeros((D_MODEL, N_SMALL - 2 * N_HEADS - GLA_RANK), f32)], axis=1)
    ws_hi = w_small.astype(bf16)
    ws_lo = (w_small - ws_hi.astype(f32)).astype(bf16)
    w_al = jnp.zeros((N_SMALL, QK), f32).at[LANE_LR:LANE_LR + GLA_RANK].set(w_alpha).astype(bf16)
    g_bias = jnp.concatenate([b_igate, b_fgate, jnp.zeros((N_SMALL - 2 * N_HEADS,), f32)])[None, :]
    return w_big, ws_hi, ws_lo, w_al, g_bias


def _layer(xp2d, xs2d, st_s, ada_p, ada_s, p, *, bp, tp, bs):
    (g_norm1, g_norm2, w_in, conv_w, conv_b, b_igate, b_fgate, g_head_a,
     w_alpha, b_alpha, g_head_b, w_out, w_ffn_gate, w_ffn_up, w_ffn_down, g_final_or_ones) = p
    w_big, ws_hi, ws_lo, w_al, g_bias = _prep_layer_weights(w_in, w_alpha, b_igate, b_fgate)
    consts = (conv_w, conv_b[None, :], g_bias, w_al, b_alpha[None, :],
              g_head_a.reshape(1, D_MODEL), g_head_b.reshape(1, D_MODEL))
    w_out16, w_g16, w_u16, w_d16 = (w.astype(bf16) for w in (w_out, w_ffn_gate, w_ffn_up, w_ffn_down))
    gn1, gn2, gfin = g_norm1[None, :], g_norm2[None, :], g_final_or_ones[None, :]

    ntp = tp // PROMPT_PROJ_ROWS
    zb, zs, tail = _inproj_call(xp2d, ada_p, gn1, w_big, ws_hi, ws_lo,
                                nb=bp, nt=ntp, bt=1, tt=PROMPT_PROJ_ROWS, zdtype=bf16)
    conv_p = tail.reshape(bp, ntp, SUBLANES, 2 * QK)[:, ntp - 1, SUBLANES - (CONV_W - 1):, :]
    u, c_p, n_p, m_p, s_p = _mixer_call(zb, zs, None, consts, nb=bp, nt=tp // PROMPT_MIX_ROWS, bt=1,
                                        L=PROMPT_CHUNK, vlo=0, vhi=PROMPT_CHUNK)
    yp = _out_call(u, xp2d, ada_p, gn2, gfin, w_out16, w_g16, w_u16, w_d16,
                   nb=bp, nt=tp // PROMPT_FFN_ROWS, bt=1, tt=PROMPT_FFN_ROWS)

    conv0, c0, n0, m0, s0 = st_s
    nbd = bs // SAMPLE_DENSE_GROUPS
    zb_s, zs_s, _ = _inproj_call(xs2d, ada_s, gn1, w_big, ws_hi, ws_lo,
                                 nb=nbd, nt=1, bt=SAMPLE_DENSE_GROUPS, tt=SAMPLE_L, zdtype=f32)
    lo = CONV_W - 1
    n_tok = SAMPLE_L - lo - 1
    conv_s = zb_s[:, OFF_QKA:OFF_QKA + 2 * QK].reshape(bs, SAMPLE_L, 2 * QK)[:, n_tok:lo + n_tok, :]
    conv0_rows = jnp.pad(conv0, ((0, 0), (0, SAMPLE_L - lo), (0, 0))).reshape(bs * SAMPLE_L, 2 * QK)
    u_s, c_s, n_s, m_s, s_s = _mixer_call(
        zb_s, zs_s, (conv0_rows, c0, n0, m0[:, None, :], s0), consts,
        nb=bs // SAMPLE_MIX_GROUPS, nt=1, bt=SAMPLE_MIX_GROUPS, L=SAMPLE_L, vlo=lo, vhi=lo + n_tok)
    ys = _out_call(u_s, xs2d, ada_s, gn2, gfin, w_out16, w_g16, w_u16, w_d16,
                   nb=nbd, nt=1, bt=SAMPLE_DENSE_GROUPS, tt=SAMPLE_L)
    st_p = (conv_p, c_p, n_p, m_p[:, 0, :], s_p)
    st_s_new = (conv_s, c_s, n_s, m_s[:, 0, :], s_s)
    return yp, ys, st_p, st_s_new


def kernel(x_prompt, x_sample, state_conv, state_C, state_n, state_m, state_S, c_prompt, c_sample,
           g_norm1, g_norm2, w_ada, b_ada, w_in, conv_w, conv_b, b_igate, b_fgate, g_head_a,
           w_alpha, b_alpha, g_head_b, w_out, w_ffn_gate, w_ffn_up, w_ffn_down, g_final):
    bp, tp, _ = x_prompt.shape
    bs, ts, _ = x_sample.shape
    depth = w_in.shape[0]
    assert depth == 1, "the final norm is fused into the (single) layer's FFN kernel"
    lo = CONV_W - 1
    assert ts == SAMPLE_L - lo - 1
    xp2d = x_prompt.reshape(bp * tp, D_MODEL)
    xs2d = jnp.pad(x_sample, ((0, 0), (lo, SAMPLE_L - lo - ts), (0, 0))).reshape(bs * SAMPLE_L, D_MODEL)
    c_all = jnp.concatenate([c_prompt, c_sample], axis=0)
    new_p = [[] for _ in range(5)]
    new_s = [[] for _ in range(5)]
    for l in range(depth):
        ada = _ada_call(c_all, w_ada[l], b_ada[l][None, :]).reshape(bp + bs, N_ADA, 1, D_MODEL)
        p = (g_norm1[l], g_norm2[l], w_in[l], conv_w[l], conv_b[l], b_igate[l], b_fgate[l], g_head_a[l],
             w_alpha[l], b_alpha[l], g_head_b[l], w_out[l], w_ffn_gate[l], w_ffn_up[l], w_ffn_down[l], g_final)
        st_s = (state_conv[l], state_C[l], state_n[l], state_m[l], state_S[l])
        xp2d, xs2d, st_p, st_s_new = _layer(xp2d, xs2d, st_s, ada[:bp], ada[bp:], p, bp=bp, tp=tp, bs=bs)
        for j in range(5):
            new_p[j].append(st_p[j])
            new_s[j].append(st_s_new[j])
    y_prompt = xp2d.reshape(bp, tp, D_MODEL)
    y_sample = xs2d.reshape(bs, SAMPLE_L, D_MODEL)[:, lo:lo + ts, :]
    outs_p = [jnp.stack(a, axis=0) for a in new_p]
    outs_s = [jnp.stack(a, axis=0) for a in new_s]
    return (y_prompt, y_sample, *outs_p, *outs_s)
```

```python
import functools

import jax
import jax.numpy as jnp
from jax import lax
from jax.experimental import pallas as pl
from jax.experimental.pallas import tpu as pltpu

f32 = jnp.float32
bf16 = jnp.bfloat16

D_MODEL = 1024
N_HEADS = 4
DK = 128
DV = 256
QK = N_HEADS * DK
CONV_W = 4
GLA_RANK = 16
GLA_TAU = 16.0
D_FF = 2816
EPS = 1e-6
N_ADA = 6

N_BIG = 8192
N_SMALL = 128
OFF_QKA, OFF_VA, OFF_OA, OFF_QB, OFF_KB, OFF_VB, OFF_RB, OFF_GA, OFF_GB = (
    0, 1024, 2048, 3072, 3584, 4096, 5120, 6144, 7168)
LANE_IG, LANE_FG, LANE_LR = 0, N_HEADS, 2 * N_HEADS

SUBLANES = 8
NEG = -1e30
SAFE_LOG_RANGE = 60.0
V7X_VMEM_BYTES = 64 * 1024 * 1024
VMEM_LIMIT_BYTES = V7X_VMEM_BYTES - 8 * 1024 * 1024
FFN_COLS = 256
PROJ_COLS = 1024
DVA = DV + DK


def _mm(a, b):
    return jnp.dot(a, b, preferred_element_type=f32)


def _mm_nt(a, b):
    return lax.dot_general(a, b, (((1,), (1,)), ((), ())), preferred_element_type=f32)


def _mm_tn(a, b):
    return lax.dot_general(a, b, (((0,), (0,)), ((), ())), preferred_element_type=f32)


def _split3(x):
    hi = x.astype(bf16)
    r = x - hi.astype(f32)
    mid = r.astype(bf16)
    lo = (r - mid.astype(f32)).astype(bf16)
    return hi, mid, lo


def _sigmoid(x):
    return 0.5 * jnp.tanh(0.5 * x) + 0.5


def _log_sigmoid(x):
    return jnp.minimum(x, 0.0) - jnp.log1p(jnp.exp(-jnp.abs(x)))


def _rms(x):
    return x * lax.rsqrt(jnp.mean(x * x, axis=-1, keepdims=True) + EPS)


def _row_to_col(row, eye):
    return jnp.sum(jnp.where(eye, row, 0.0), axis=1, keepdims=True)


def _diag_row(mat, eye):
    return jnp.sum(jnp.where(eye, mat, 0.0), axis=0, keepdims=True)


def _mod_rows(mod_ref, bt, tt):
    if bt == 1:
        return mod_ref[0, 0]
    v = mod_ref[...].reshape(bt, 1, D_MODEL)
    return jnp.broadcast_to(v, (bt, tt, D_MODEL)).reshape(bt * tt, D_MODEL)


def _ada_body(c_ref, w_ref, b_ref, o_ref):
    ch, cl, _ = _split3(c_ref[...])
    wh, wl, _ = _split3(w_ref[...])
    o_ref[...] = _mm(ch, wh) + (_mm(cl, wh) + _mm(ch, wl)) + b_ref[...]


def _ada_call(c_all, w_ada, b_ada):
    m = c_all.shape[0]
    n = w_ada.shape[1]
    return pl.pallas_call(
        _ada_body,
        out_shape=jax.ShapeDtypeStruct((m, n), f32),
        grid=(n // D_MODEL,),
        in_specs=[
            pl.BlockSpec((m, D_MODEL), lambda i: (0, 0)),
            pl.BlockSpec((D_MODEL, D_MODEL), lambda i: (0, i)),
            pl.BlockSpec((1, D_MODEL), lambda i: (0, i)),
        ],
        out_specs=pl.BlockSpec((m, D_MODEL), lambda i: (0, i)),
        compiler_params=pltpu.CompilerParams(
            dimension_semantics=("arbitrary",), vmem_limit_bytes=VMEM_LIMIT_BYTES),
        name="ada_proj",
    )(c_all, w_ada, b_ada)


def _inproj_body(x_ref, shift_ref, scale_ref, g_ref, wbig_ref, wsh_ref, wsl_ref,
                 zb_ref, zs_ref, tail_ref, *, bt, tt):
    rows = bt * tt
    h = _rms(x_ref[...]) * g_ref[...]
    h = h * (1.0 + _mod_rows(scale_ref, bt, tt)) + _mod_rows(shift_ref, bt, tt)
    hh = h.astype(bf16)
    hl = (h - hh.astype(f32)).astype(bf16)
    for c in range(0, N_BIG, PROJ_COLS):
        z = _mm(hh, wbig_ref[:, c:c + PROJ_COLS])
        zb_ref[:, c:c + PROJ_COLS] = z.astype(zb_ref.dtype)
        if c == OFF_QKA:
            tail_ref[0] = z[rows - SUBLANES:rows, :]
    wsh = wsh_ref[...]
    zs_ref[...] = _mm(hh, wsh) + (_mm(hl, wsh) + _mm(hh, wsl_ref[...]))


def _const_spec(shape):
    nd = len(shape)
    return pl.BlockSpec(shape, lambda i, j: (0,) * nd, pipeline_mode=pl.Buffered(1))


def _mod_spec(bt, k):
    return pl.BlockSpec((bt, 1, 1, D_MODEL), lambda i, j: (i, k, 0, 0))


def _inproj_call(x2d, ada4, g_norm, w_big, ws_hi, ws_lo, *, nb, nt, bt, tt, zdtype):
    rows = bt * tt
    total = x2d.shape[0]
    row_spec = lambda n: pl.BlockSpec((rows, n), lambda i, j: (i * nt + j, 0))
    return pl.pallas_call(
        functools.partial(_inproj_body, bt=bt, tt=tt),
        out_shape=(
            jax.ShapeDtypeStruct((total, N_BIG), zdtype),
            jax.ShapeDtypeStruct((total, N_SMALL), f32),
            jax.ShapeDtypeStruct((nb * nt, SUBLANES, PROJ_COLS), f32),
        ),
        grid=(nb, nt),
        in_specs=[
            row_spec(D_MODEL), _mod_spec(bt, 0), _mod_spec(bt, 1),
            _const_spec((1, D_MODEL)), _const_spec((D_MODEL, N_BIG)),
            _const_spec((D_MODEL, N_SMALL)), _const_spec((D_MODEL, N_SMALL)),
        ],
        out_specs=(
            row_spec(N_BIG), row_spec(N_SMALL),
            pl.BlockSpec((1, SUBLANES, PROJ_COLS), lambda i, j: (i * nt + j, 0, 0)),
        ),
        compiler_params=pltpu.CompilerParams(
            dimension_semantics=("arbitrary", "arbitrary"), vmem_limit_bytes=VMEM_LIMIT_BYTES),
        name="in_proj",
    )(x2d, ada4, ada4, g_norm, w_big, ws_hi, ws_lo)


def _mixer_body(*refs, L, vlo, vhi, sample):
    if sample:
        (zb_ref, zs_ref, conv0_ref, C0_ref, n0_ref, m0_ref, S0_ref,
         cw_ref, cb_ref, gb_ref, wal_ref, bal_ref, gha_ref, ghb_ref,
         u_ref, Co_ref, no_ref, mo_ref, So_ref,
         qa_s, ka_s, qh_s, kh_s, kb_s, x_s, r_s, bg_s, ha_s, hb_s, carry_s, caug_s) = refs
    else:
        (zb_ref, zs_ref,
         cw_ref, cb_ref, gb_ref, wal_ref, bal_ref, gha_ref, ghb_ref,
         u_ref, Co_ref, no_ref, mo_ref, So_ref,
         qa_s, ka_s, qh_s, kh_s, kb_s, x_s, r_s, bg_s, ha_s, hb_s, carry_s, caug_s) = refs
    rows = zb_ref.shape[0]
    n_groups = rows // L
    log2_l = L.bit_length() - 1
    masked = not (vlo == 0 and vhi == L)
    scale = DK ** -0.5

    if not sample:
        @pl.when(pl.program_id(1) == 0)
        def _zero_state():
            caug_s[...] = jnp.zeros(caug_s.shape, f32)
            mo_ref[...] = jnp.zeros(mo_ref.shape, f32)
            So_ref[...] = jnp.zeros(So_ref.shape, f32)
            carry_s[...] = jnp.zeros(carry_s.shape, f32)

    row_in = lax.broadcasted_iota(jnp.int32, (rows, 1), 0) & (L - 1)
    valid = (row_in >= vlo) & (row_in < vhi)

    zq = zb_ref[:, OFF_QKA:OFF_QKA + 2 * QK].astype(f32)
    if sample:
        zq = jnp.where(row_in < CONV_W - 1, conv0_ref[...], zq)
        carry = jnp.zeros((SUBLANES, 2 * QK), f32)
    else:
        carry = carry_s[...]
        carry_s[...] = zq[rows - SUBLANES:rows, :]
    xe = jnp.concatenate([carry, zq], axis=0)
    cw = cw_ref[...]
    conv = cb_ref[...] + zq * cw[CONV_W - 1:CONV_W]
    for k in range(CONV_W - 1):
        o = SUBLANES - (CONV_W - 1) + k
        conv = conv + xe[o:o + rows] * cw[k:k + 1]
    qk = conv * _sigmoid(conv)
    qa_s[...] = qk[:, :QK] * scale
    ka_s[...] = qk[:, QK:]

    lane = lax.broadcasted_iota(jnp.int32, (1, N_SMALL), 1)
    zs = zs_ref[...]
    pre = zs + gb_ref[...]
    gate = jnp.where(lane < LANE_FG, pre, _log_sigmoid(pre))
    gate_f = jnp.where(valid, gate, 0.0) if masked else gate
    rr = lax.broadcasted_iota(jnp.int32, (rows, rows), 0)
    cc = lax.broadcasted_iota(jnp.int32, (rows, rows), 1)
    tril = jnp.where(((rr >> log2_l) == (cc >> log2_l)) & (cc <= rr), 1.0, 0.0).astype(bf16)

    def cumsum_groups(v):
        hi, mid, lo = _split3(v)
        return _mm(tril, hi) + (_mm(tril, mid) + _mm(tril, lo))

    x_all = jnp.where(lane < LANE_FG, gate, cumsum_groups(gate_f))
    x_s[...] = x_all
    r_all = x_all - pltpu.roll(x_all, N_SMALL - N_HEADS, axis=1)
    r_ok = (lane < N_HEADS) & valid if masked else jnp.broadcast_to(lane < N_HEADS, (rows, N_SMALL))
    r_s[...] = jnp.where(r_ok, r_all, NEG)
    range_m = jnp.max(jnp.where(r_ok, r_all, NEG)) - jnp.min(jnp.where(r_ok, r_all, -NEG))

    la = _log_sigmoid(_mm(zs.astype(bf16), wal_ref[...]) + bal_ref[...]) * (1.0 / GLA_TAU)
    kb = zb_ref[:, OFF_KB:OFF_KB + QK].astype(f32)
    if masked:
        la = jnp.where(valid, la, 0.0)
        kb = jnp.where(valid, kb, 0.0)
    bg = cumsum_groups(la)
    bg_s[...] = bg
    range_g = jnp.max(-bg)
    qh_s[...] = zb_ref[:, OFF_QB:OFF_QB + QK].astype(f32) * scale * jnp.exp(bg)
    kh_s[...] = kb * jnp.exp(-bg)
    fast_ok = jnp.logical_and(range_m <= SAFE_LOG_RANGE, range_g <= SAFE_LOG_RANGE)

    tri = lax.broadcasted_iota(jnp.int32, (L, L), 0) >= lax.broadcasted_iota(jnp.int32, (L, L), 1)
    eye = lax.broadcasted_iota(jnp.int32, (DK, DK), 0) == lax.broadcasted_iota(jnp.int32, (DK, DK), 1)
    lane_h = lax.broadcasted_iota(jnp.int32, (1, N_HEADS), 1)
    lrow = lax.broadcasted_iota(jnp.int32, (1, L), 1)
    lcol = lax.broadcasted_iota(jnp.int32, (L, 1), 0)
    vrow = (lrow >= vlo) & (lrow < vhi)
    ones16 = jnp.ones((L, DK), bf16)

    def load_state(g, h):
        if sample:
            n_mat = jnp.broadcast_to(_row_to_col(n0_ref[g, pl.ds(h, 1), :], eye), (DK, DK))
            return jnp.concatenate([C0_ref[g, h], n_mat], axis=1), S0_ref[g, h]
        return caug_s[h], So_ref[0, h]

    def store_state(g, h, caug, s_new):
        sidx = g if sample else 0
        if not sample:
            caug_s[h] = caug
        Co_ref[sidx, h] = caug[:, :DV]
        no_ref[sidx, pl.ds(h, 1), :] = _diag_row(caug[:, DV:], eye)
        So_ref[sidx, h] = s_new

    def group(g, exact):
        static = isinstance(g, int)
        r0 = g * L if static else pl.multiple_of(g * L, L)
        tail0 = r0 + L - SUBLANES if static else pl.multiple_of(r0 + L - SUBLANES, SUBLANES)
        rs = pl.ds(r0, L)
        xc = x_s[rs, :]
        rc = r_s[rs, :]
        if exact:
            sr = lax.broadcasted_iota(jnp.int32, (2 * SUBLANES, N_SMALL), 0)
            sc = lax.broadcasted_iota(jnp.int32, (2 * SUBLANES, N_SMALL), 1)
            sel = jnp.where((sr < N_HEADS) & (sc == sr + LANE_IG), 1.0,
                            jnp.where((sr < N_HEADS) & (sc == sr + LANE_FG), -1.0, 0.0)).astype(bf16)
            xh, xm, xl = _split3(xc)
            r_rows = _mm_nt(sel, xh) + (_mm_nt(sel, xm) + _mm_nt(sel, xl))
        m_prev = m0_ref[g] if sample else mo_ref[0]
        m_vec = m_prev
        bg_tail = bg_s[pl.ds(tail0, SUBLANES), :]
        for h in range(N_HEADS):
            ks = slice(h * DK, (h + 1) * DK)
            vs = slice(h * DV, (h + 1) * DV)
            caug, s_st = load_state(g, h)
            m_st = m_prev[:, h:h + 1]
            b_col = xc[:, LANE_FG + h:LANE_FG + h + 1]
            r_col = rc[:, h:h + 1]
            b_last = b_col[L - 1:L, :]
            v16 = zb_ref[rs, OFF_VA + h * DV:OFF_VA + (h + 1) * DV].astype(bf16)
            vb16 = zb_ref[rs, OFF_VB + h * DV:OFF_VB + (h + 1) * DV].astype(bf16)
            eb_last = jnp.exp(bg_tail[SUBLANES - 1:SUBLANES, ks])
            eb_col = _row_to_col(eb_last, eye)
            if not exact:
                c0 = jnp.max(r_col, axis=0, keepdims=True)
                mm = jnp.maximum(c0, m_st)
                q16 = qa_s[rs, ks].astype(bf16)
                kt16 = (ka_s[rs, ks] * jnp.exp(r_col - c0)).astype(bf16)
                va = jnp.concatenate([v16, ones16], axis=1)
                s16 = jnp.where(tri, _mm_nt(q16, kt16), 0.0).astype(bf16)
                comb = jnp.exp(c0 - mm) * _mm(s16, va) + jnp.exp(m_st - mm) * _mm(q16, caug.astype(bf16))
                inv = 1.0 / jnp.maximum(jnp.abs(comb[:, DV:]), jnp.exp(-(b_col + mm)))
                ha_s[rs, h * DV:h * DV + DK] = comb[:, :DK] * inv
                ha_s[rs, h * DV + DK:(h + 1) * DV] = comb[:, DK:DV] * inv
                m_new = jnp.maximum(b_last + m_st, b_last + c0)
                caug_new = (jnp.exp(b_last + m_st - m_new) * caug
                            + jnp.exp(b_last + c0 - m_new) * _mm_tn(kt16, va))
                qh16 = qh_s[rs, ks].astype(bf16)
                kh = kh_s[rs, ks]
                a16 = jnp.where(tri, _mm_nt(qh16, kh.astype(bf16)), 0.0).astype(bf16)
                hb_s[rs, vs] = _mm(a16, vb16) + _mm(qh16, s_st.astype(bf16))
                s_new = eb_col * s_st + _mm_tn((kh * eb_last).astype(bf16), vb16)
            else:
                c_st = caug[:, :DV]
                n_row = _diag_row(caug[:, DV:], eye)
                r_row = r_rows[h:h + 1, :]
                if masked:
                    r_row = jnp.where(vrow, r_row, NEG)
                dm = jnp.where(tri, b_col + r_row, NEG)
                m_t = jnp.maximum(jnp.max(dm, axis=1, keepdims=True), b_col + m_st)
                w = jnp.exp(dm - m_t)
                wi = jnp.exp(b_col + m_st - m_t)
                q = qa_s[rs, ks]
                k = ka_s[rs, ks]
                q16 = q.astype(bf16)
                s = _mm_nt(q16, k.astype(bf16)) * w
                num = _mm(s.astype(bf16), v16) + wi * _mm(q16, c_st.astype(bf16))
                den = jnp.sum(s, axis=1, keepdims=True) + wi * jnp.sum(q * n_row, axis=1, keepdims=True)
                ha_s[rs, vs] = num / jnp.maximum(jnp.abs(den), jnp.exp(-m_t))
                m_new = jnp.maximum(b_last + m_st, b_last + jnp.max(r_row, axis=1, keepdims=True))
                wc = jnp.exp(b_last + m_st - m_new)
                kt = k * jnp.exp(b_last + r_col - m_new)
                n_new = wc * n_row + jnp.sum(kt, axis=0, keepdims=True)
                caug_new = jnp.concatenate(
                    [wc * c_st + _mm_tn(kt.astype(bf16), v16),
                     jnp.broadcast_to(_row_to_col(n_new, eye), (DK, DK))], axis=1)
                qb = zb_ref[rs, OFF_QB + h * DK:OFF_QB + (h + 1) * DK].astype(f32) * scale
                bgc = bg_s[rs, ks]
                sub = lax.broadcasted_iota(jnp.int32, (SUBLANES, 1), 0)

                def score_col(s_i, a, ks=ks, qb=qb, bgc=bgc, sub=sub):
                    s8 = lax.shift_left(lax.shift_right_logical(s_i, 3), 3)
                    base = pl.multiple_of(r0 + s8, SUBLANES)
                    pick = sub == (s_i - s8)
                    b_s = jnp.sum(jnp.where(pick, bg_s[pl.ds(base, SUBLANES), ks], 0.0), axis=0, keepdims=True)
                    k_s = jnp.sum(jnp.where(pick, kb_s[pl.ds(base, SUBLANES), ks], 0.0), axis=0, keepdims=True)
                    e = jnp.exp(jnp.where(lcol >= s_i, bgc - b_s, NEG))
                    col = jnp.sum(qb * e * k_s, axis=1, keepdims=True)
                    return jnp.where(lrow == s_i, col, a)

                a = lax.fori_loop(0, L, score_col, jnp.zeros((L, L), f32))
                hb_s[rs, vs] = (_mm(a.astype(bf16), vb16)
                                + _mm((qb * jnp.exp(bgc)).astype(bf16), s_st.astype(bf16)))
                kt_b = kb_s[rs, ks] * jnp.exp(bg_tail[SUBLANES - 1:SUBLANES, ks] - bgc)
                s_new = eb_col * s_st + _mm_tn(kt_b.astype(bf16), vb16)
            store_state(g, h, caug_new, s_new)
            m_vec = jnp.where(lane_h == h, m_new, m_vec)
        mo_ref[g if sample else 0] = m_vec

    def run_groups(exact):
        if n_groups == 1:
            group(0, exact)
        else:
            def body(g, carry_):
                group(g, exact)
                return carry_
            lax.fori_loop(0, n_groups, body, 0)

    @pl.when(fast_ok)
    def _fast():
        run_groups(False)

    @pl.when(jnp.logical_not(fast_ok))
    def _exact():
        kb_s[...] = kb
        run_groups(True)

    for h in range(N_HEADS):
        vs = slice(h * DV, (h + 1) * DV)
        col = lambda off: zb_ref[:, off + h * DV:off + (h + 1) * DV].astype(f32)
        a_out = _rms(ha_s[:, vs]) * gha_ref[:, vs] * _sigmoid(col(OFF_OA))
        rb = col(OFF_RB)
        b_out = _rms(hb_s[:, vs]) * ghb_ref[:, vs] * (rb * _sigmoid(rb))
        u = _sigmoid(col(OFF_GA)) * a_out + _sigmoid(col(OFF_GB)) * b_out
        u_ref[:, vs] = u.astype(u_ref.dtype)


def _mixer_call(zb, zs, init, consts, *, nb, nt, bt, L, vlo, vhi):
    sample = init is not None
    rows = bt * L
    n_state = nb * bt
    row_spec = lambda n: pl.BlockSpec((rows, n), lambda i, j: (i * nt + j, 0))
    st4 = pl.BlockSpec((bt, N_HEADS, DK, DV), lambda i, j: (i, 0, 0, 0))
    st_n = pl.BlockSpec((bt, N_HEADS, DK), lambda i, j: (i, 0, 0))
    st_m = pl.BlockSpec((bt, 1, N_HEADS), lambda i, j: (i, 0, 0))
    in_specs = [row_spec(N_BIG), row_spec(N_SMALL)]
    args = [zb, zs]
    if sample:
        in_specs += [row_spec(2 * QK), st4, st_n, st_m, st4]
        args += list(init)
    in_specs += [_const_spec(c.shape) for c in consts]
    args += list(consts)
    scratch = [pltpu.VMEM((rows, QK), f32)] * 5 + [
        pltpu.VMEM((rows, N_SMALL), f32), pltpu.VMEM((rows, N_SMALL), f32), pltpu.VMEM((rows, QK), f32),
        pltpu.VMEM((rows, D_MODEL), f32), pltpu.VMEM((rows, D_MODEL), f32),
        pltpu.VMEM((SUBLANES, 2 * QK), f32), pltpu.VMEM((N_HEADS, DK, DVA), f32),
    ]
    return pl.pallas_call(
        functools.partial(_mixer_body, L=L, vlo=vlo, vhi=vhi, sample=sample),
        out_shape=(
            jax.ShapeDtypeStruct((zb.shape[0], D_MODEL), bf16),
            jax.ShapeDtypeStruct((n_state, N_HEADS, DK, DV), f32),
            jax.ShapeDtypeStruct((n_state, N_HEADS, DK), f32),
            jax.ShapeDtypeStruct((n_state, 1, N_HEADS), f32),
            jax.ShapeDtypeStruct((n_state, N_HEADS, DK, DV), f32),
        ),
        grid=(nb, nt),
        in_specs=in_specs,
        out_specs=(row_spec(D_MODEL), st4, st_n, st_m, st4),
        scratch_shapes=scratch,
        compiler_params=pltpu.CompilerParams(
            dimension_semantics=("arbitrary", "arbitrary"), vmem_limit_bytes=VMEM_LIMIT_BYTES),
        name="mixer_sample" if sample else "mixer_prompt",
    )(*args)


def _out_body(u_ref, x_ref, gate1_ref, shift2_ref, scale2_ref, gate2_ref, gn2_ref, gfin_ref,
              wout_ref, wg_ref, wu_ref, wd_ref, y_ref, *, bt, tt):
    x1 = x_ref[...] + _mod_rows(gate1_ref, bt, tt) * _mm(u_ref[...], wout_ref[...])
    h2 = _rms(x1) * gn2_ref[...]
    h2 = (h2 * (1.0 + _mod_rows(scale2_ref, bt, tt)) + _mod_rows(shift2_ref, bt, tt)).astype(bf16)
    acc = jnp.zeros(x1.shape, f32)
    for c in range(0, D_FF, FFN_COLS):
        g = _mm(h2, wg_ref[:, c:c + FFN_COLS])
        up = _mm(h2, wu_ref[:, c:c + FFN_COLS])
        acc = acc + _mm((g * _sigmoid(g) * up).astype(bf16), wd_ref[c:c + FFN_COLS, :])
    x2 = x1 + _mod_rows(gate2_ref, bt, tt) * acc
    y_ref[...] = _rms(x2) * gfin_ref[...]


def _out_call(u2d, x2d, ada4, g_norm2, g_final, w_out, w_gate, w_up, w_down, *, nb, nt, bt, tt):
    rows = bt * tt
    row_spec = pl.BlockSpec((rows, D_MODEL), lambda i, j: (i * nt + j, 0))
    return pl.pallas_call(
        functools.partial(_out_body, bt=bt, tt=tt),
        out_shape=jax.ShapeDtypeStruct(x2d.shape, f32),
        grid=(nb, nt),
        in_specs=[
            row_spec, row_spec, _mod_spec(bt, 2), _mod_spec(bt, 3), _mod_spec(bt, 4), _mod_spec(bt, 5),
            _const_spec((1, D_MODEL)), _const_spec((1, D_MODEL)),
            _const_spec((D_MODEL, D_MODEL)), _const_spec((D_MODEL, D_FF)),
            _const_spec((D_MODEL, D_FF)), _const_spec((D_FF, D_MODEL)),
        ],
        out_specs=row_spec,
        compiler_params=pltpu.CompilerParams(
            dimension_semantics=("arbitrary", "arbitrary"), vmem_limit_bytes=VMEM_LIMIT_BYTES),
        name="out_ffn",
    )(u2d, x2d, ada4, ada4, ada4, ada4, g_norm2, g_final, w_out, w_gate, w_up, w_down)


PROMPT_CHUNK = 256
PROMPT_PROJ_ROWS = 256
PROMPT_FFN_ROWS = 512
SAMPLE_L = SUBLANES
SAMPLE_MIX_GROUPS = 8
SAMPLE_DENSE_GROUPS = 32


def _prep_layer_weights(w_in, w_alpha, b_igate, b_fgate):
    s = [0, 1024, 2048, 3072, 3076, 3080, 3592, 4104, 5128, 6152, 6168, 7192, 8216]
    w_big = jnp.concatenate([w_in[:, s[0]:s[3]], w_in[:, s[5]:s[9]], w_in[:, s[10]:s[12]]], axis=1).astype(bf16)
    w_small = jnp.concatenate(
        [w_in[:, s[3]:s[5]], w_in[:, s[9]:s[10]],
         jnp.zeros((D_MODEL, N_SMALL - 2 * N_HEADS - GLA_RANK), f32)], axis=1)
    ws_hi = w_small.astype(bf16)
    ws_lo = (w_small - ws_hi.astype(f32)).astype(bf16)
    w_al = jnp.zeros((N_SMALL, QK), f32).at[LANE_LR:LANE_LR + GLA_RANK].set(w_alpha).astype(bf16)
    g_bias = jnp.concatenate([b_igate, b_fgate, jnp.zeros((N_SMALL - 2 * N_HEADS,), f32)])[None, :]
    return w_big, ws_hi, ws_lo, w_al, g_bias


def _layer(xp2d, xs2d, st_s, ada_p, ada_s, p, *, bp, tp, bs):
    (g_norm1, g_norm2, w_in, conv_w, conv_b, b_igate, b_fgate, g_head_a,
     w_alpha, b_alpha, g_head_b, w_out, w_ffn_gate, w_ffn_up, w_ffn_down, g_final_or_ones) = p
    w_big, ws_hi, ws_lo, w_al, g_bias = _prep_layer_weights(w_in, w_alpha, b_igate, b_fgate)
    consts = (conv_w, conv_b[None, :], g_bias, w_al, b_alpha[None, :],
              g_head_a.reshape(1, D_MODEL), g_head_b.reshape(1, D_MODEL))
    w_out16, w_g16, w_u16, w_d16 = (w.astype(bf16) for w in (w_out, w_ffn_gate, w_ffn_up, w_ffn_down))
    gn1, gn2, gfin = g_norm1[None, :], g_norm2[None, :], g_final_or_ones[None, :]

    ntp = tp // PROMPT_PROJ_ROWS
    zb, zs, tail = _inproj_call(xp2d, ada_p, gn1, w_big, ws_hi, ws_lo,
                                nb=bp, nt=ntp, bt=1, tt=PROMPT_PROJ_ROWS, zdtype=bf16)
    conv_p = tail.reshape(bp, ntp, SUBLANES, 2 * QK)[:, ntp - 1, SUBLANES - (CONV_W - 1):, :]
    u, c_p, n_p, m_p, s_p = _mixer_call(zb, zs, None, consts, nb=bp, nt=tp // PROMPT_CHUNK, bt=1,
                                        L=PROMPT_CHUNK, vlo=0, vhi=PROMPT_CHUNK)
    yp = _out_call(u, xp2d, ada_p, gn2, gfin, w_out16, w_g16, w_u16, w_d16,
                   nb=bp, nt=tp // PROMPT_FFN_ROWS, bt=1, tt=PROMPT_FFN_ROWS)

    conv0, c0, n0, m0, s0 = st_s
    nbd = bs // SAMPLE_DENSE_GROUPS
    zb_s, zs_s, _ = _inproj_call(xs2d, ada_s, gn1, w_big, ws_hi, ws_lo,
                                 nb=nbd, nt=1, bt=SAMPLE_DENSE_GROUPS, tt=SAMPLE_L, zdtype=f32)
    lo = CONV_W - 1
    n_tok = SAMPLE_L - lo - 1
    conv_s = zb_s[:, OFF_QKA:OFF_QKA + 2 * QK].reshape(bs, SAMPLE_L, 2 * QK)[:, n_tok:lo + n_tok, :]
    conv0_rows = jnp.pad(conv0, ((0, 0), (0, SAMPLE_L - lo), (0, 0))).reshape(bs * SAMPLE_L, 2 * QK)
    u_s, c_s, n_s, m_s, s_s = _mixer_call(
        zb_s, zs_s, (conv0_rows, c0, n0, m0[:, None, :], s0), consts,
        nb=bs // SAMPLE_MIX_GROUPS, nt=1, bt=SAMPLE_MIX_GROUPS, L=SAMPLE_L, vlo=lo, vhi=lo + n_tok)
    ys = _out_call(u_s, xs2d, ada_s, gn2, gfin, w_out16, w_g16, w_u16, w_d16,
                   nb=nbd, nt=1, bt=SAMPLE_DENSE_GROUPS, tt=SAMPLE_L)
    st_p = (conv_p, c_p, n_p, m_p[:, 0, :], s_p)
    st_s_new = (conv_s, c_s, n_s, m_s[:, 0, :], s_s)
    return yp, ys, st_p, st_s_new


def kernel(x_prompt, x_sample, state_conv, state_C, state_n, state_m, state_S, c_prompt, c_sample,
           g_norm1, g_norm2, w_ada, b_ada, w_in, conv_w, conv_b, b_igate, b_fgate, g_head_a,
           w_alpha, b_alpha, g_head_b, w_out, w_ffn_gate, w_ffn_up, w_ffn_down, g_final):
    bp, tp, _ = x_prompt.shape
    bs, ts, _ = x_sample.shape
    depth = w_in.shape[0]
    assert depth == 1, "the final norm is fused into the (single) layer's FFN kernel"
    lo = CONV_W - 1
    assert ts == SAMPLE_L - lo - 1
    xp2d = x_prompt.reshape(bp * tp, D_MODEL)
    xs2d = jnp.pad(x_sample, ((0, 0), (lo, SAMPLE_L - lo - ts), (0, 0))).reshape(bs * SAMPLE_L, D_MODEL)
    c_all = jnp.concatenate([c_prompt, c_sample], axis=0)
    new_p = [[] for _ in range(5)]
    new_s = [[] for _ in range(5)]
    for l in range(depth):
        ada = _ada_call(c_all, w_ada[l], b_ada[l][None, :]).reshape(bp + bs, N_ADA, 1, D_MODEL)
        p = (g_norm1[l], g_norm2[l], w_in[l], conv_w[l], conv_b[l], b_igate[l], b_fgate[l], g_head_a[l],
             w_alpha[l], b_alpha[l], g_head_b[l], w_out[l], w_ffn_gate[l], w_ffn_up[l], w_ffn_down[l], g_final)
        st_s = (state_conv[l], state_C[l], state_n[l], state_m[l], state_S[l])
        xp2d, xs2d, st_p, st_s_new = _layer(xp2d, xs2d, st_s, ada[:bp], ada[bp:], p, bp=bp, tp=tp, bs=bs)
        for j in range(5):
            new_p[j].append(st_p[j])
            new_s[j].append(st_s_new[j])
    y_prompt = xp2d.reshape(bp, tp, D_MODEL)
    y_sample = xs2d.reshape(bs, SAMPLE_L, D_MODEL)[:, lo:lo + ts, :]
    outs_p = [jnp.stack(a, axis=0) for a in new_p]
    outs_s = [jnp.stack(a, axis=0) for a in new_s]
    return (y_prompt, y_sample, *outs_p, *outs_s)
```

```python
import functools

import jax
import jax.numpy as jnp
from jax import lax
from jax.experimental import pallas as pl
from jax.experimental.pallas import tpu as pltpu

f32 = jnp.float32
bf16 = jnp.bfloat16

D_MODEL = 1024
N_HEADS = 4
DK = 128
DV = 256
QK = N_HEADS * DK
CONV_W = 4
GLA_RANK = 16
GLA_TAU = 16.0
D_FF = 2816
EPS = 1e-6
N_ADA = 6

N_BIG = 8192
N_SMALL = 128
OFF_QKA, OFF_VA, OFF_OA, OFF_QB, OFF_KB, OFF_VB, OFF_RB, OFF_GA, OFF_GB = (
    0, 1024, 2048, 3072, 3584, 4096, 5120, 6144, 7168)
LANE_IG, LANE_FG, LANE_LR = 0, N_HEADS, 2 * N_HEADS

SUBLANES = 8
NEG = -1e30
SAFE_LOG_RANGE = 60.0
V7X_VMEM_BYTES = 64 * 1024 * 1024
VMEM_LIMIT_BYTES = V7X_VMEM_BYTES - 8 * 1024 * 1024
FFN_COLS = 256
PROJ_COLS = 1024
DVA = DV + DK


def _mm(a, b):
    return jnp.dot(a, b, preferred_element_type=f32)


def _mm_nt(a, b):
    return lax.dot_general(a, b, (((1,), (1,)), ((), ())), preferred_element_type=f32)


def _mm_tn(a, b):
    return lax.dot_general(a, b, (((0,), (0,)), ((), ())), preferred_element_type=f32)


def _split3(x):
    hi = x.astype(bf16)
    r = x - hi.astype(f32)
    mid = r.astype(bf16)
    lo = (r - mid.astype(f32)).astype(bf16)
    return hi, mid, lo


def _sigmoid(x):
    return 0.5 * jnp.tanh(0.5 * x) + 0.5


def _log_sigmoid(x):
    return jnp.minimum(x, 0.0) - jnp.log(1.0 + jnp.exp(-jnp.abs(x)))


def _rms(x):
    return x * lax.rsqrt(jnp.mean(x * x, axis=-1, keepdims=True) + EPS)


def _row_to_col(row, eye):
    return jnp.sum(jnp.where(eye, row, 0.0), axis=1, keepdims=True)


def _diag_row(mat, eye):
    return jnp.sum(jnp.where(eye, mat, 0.0), axis=0, keepdims=True)


def _mod_rows(mod_ref, bt, tt):
    if bt == 1:
        return mod_ref[0, 0]
    v = mod_ref[...].reshape(bt, 1, D_MODEL)
    return jnp.broadcast_to(v, (bt, tt, D_MODEL)).reshape(bt * tt, D_MODEL)


def _ada_body(c_ref, w_ref, b_ref, o_ref):
    ch, cl, _ = _split3(c_ref[...])
    wh, wl, _ = _split3(w_ref[...])
    o_ref[...] = _mm(ch, wh) + (_mm(cl, wh) + _mm(ch, wl)) + b_ref[...]


def _ada_call(c_all, w_ada, b_ada):
    m = c_all.shape[0]
    n = w_ada.shape[1]
    return pl.pallas_call(
        _ada_body,
        out_shape=jax.ShapeDtypeStruct((m, n), f32),
        grid=(n // D_MODEL,),
        in_specs=[
            pl.BlockSpec((m, D_MODEL), lambda i: (0, 0)),
            pl.BlockSpec((D_MODEL, D_MODEL), lambda i: (0, i)),
            pl.BlockSpec((1, D_MODEL), lambda i: (0, i)),
        ],
        out_specs=pl.BlockSpec((m, D_MODEL), lambda i: (0, i)),
        compiler_params=pltpu.CompilerParams(
            dimension_semantics=("arbitrary",), vmem_limit_bytes=VMEM_LIMIT_BYTES),
        name="ada_proj",
    )(c_all, w_ada, b_ada)


def _inproj_parts(x_ref, shift_ref, scale_ref, g_ref, wbig_ref, wsh_ref, wsl_ref,
                  zb_ref, zs_ref, tail_ref, *, bt, tt):
    rows = bt * tt
    h = _rms(x_ref[...]) * g_ref[...]
    h = h * (1.0 + _mod_rows(scale_ref, bt, tt)) + _mod_rows(shift_ref, bt, tt)
    hh = h.astype(bf16)
    hl = (h - hh.astype(f32)).astype(bf16)

    def narrow():
        wsh = wsh_ref[...]
        zs_ref[...] = _mm(hh, wsh) + (_mm(hl, wsh) + _mm(hh, wsl_ref[...]))

    def wide(idx):
        c = idx * PROJ_COLS
        z = _mm(hh, wbig_ref[:, c:c + PROJ_COLS])
        if isinstance(zb_ref, (list, tuple)):
            zb_ref[idx][...] = z
        else:
            zb_ref[:, c:c + PROJ_COLS] = z.astype(zb_ref.dtype)
        if c == OFF_QKA:
            tail_ref[0] = z[rows - SUBLANES:rows, :]

    return narrow, wide


def _inproj_body(*refs, bt, tt):
    narrow, wide = _inproj_parts(*refs, bt=bt, tt=tt)
    narrow()
    for idx in range(N_BIG // PROJ_COLS):
        wide(idx)


def _const_spec(shape):
    nd = len(shape)
    return pl.BlockSpec(shape, lambda i, j: (0,) * nd, pipeline_mode=pl.Buffered(1))


def _mod_spec(bt, k):
    return pl.BlockSpec((bt, 1, 1, D_MODEL), lambda i, j: (i, k, 0, 0))


def _inproj_call(x2d, ada4, g_norm, w_big, ws_hi, ws_lo, *, nb, nt, bt, tt, zdtype):
    rows = bt * tt
    total = x2d.shape[0]
    row_spec = lambda n: pl.BlockSpec((rows, n), lambda i, j: (i * nt + j, 0))
    return pl.pallas_call(
        functools.partial(_inproj_body, bt=bt, tt=tt),
        out_shape=(
            jax.ShapeDtypeStruct((total, N_BIG), zdtype),
            jax.ShapeDtypeStruct((total, N_SMALL), f32),
            jax.ShapeDtypeStruct((nb * nt, SUBLANES, PROJ_COLS), f32),
        ),
        grid=(nb, nt),
        in_specs=[
            row_spec(D_MODEL), _mod_spec(bt, 0), _mod_spec(bt, 1),
            _const_spec((1, D_MODEL)), _const_spec((D_MODEL, N_BIG)),
            _const_spec((D_MODEL, N_SMALL)), _const_spec((D_MODEL, N_SMALL)),
        ],
        out_specs=(
            row_spec(N_BIG), row_spec(N_SMALL),
            pl.BlockSpec((1, SUBLANES, PROJ_COLS), lambda i, j: (i * nt + j, 0, 0)),
        ),
        compiler_params=pltpu.CompilerParams(
            dimension_semantics=("arbitrary", "arbitrary"), vmem_limit_bytes=VMEM_LIMIT_BYTES),
        name="in_proj",
    )(x2d, ada4, ada4, g_norm, w_big, ws_hi, ws_lo)


N_MIXER_CONSTS = 7
N_MIXER_INIT = 5
N_MIXER_OUTS = 5


def _mixer_scratch(rows):
    return [pltpu.VMEM((rows, QK), f32)] * 5 + [
        pltpu.VMEM((rows, N_SMALL), f32), pltpu.VMEM((rows, N_SMALL), f32), pltpu.VMEM((rows, QK), f32),
    ] + [pltpu.VMEM((rows, D_MODEL), f32)] * 4 + [
        pltpu.VMEM((SUBLANES, 2 * QK), f32), pltpu.VMEM((N_HEADS, DK, DVA), f32),
    ]


def _mixer_body(*refs, L, vlo, vhi, sample):
    zb_ref, zs_ref = refs[:2]
    k = 2
    init = None
    if sample:
        init = refs[k:k + N_MIXER_INIT]
        k += N_MIXER_INIT
    consts = refs[k:k + N_MIXER_CONSTS]
    k += N_MIXER_CONSTS
    outs = refs[k:k + N_MIXER_OUTS]
    _mixer_compute(zb_ref, zs_ref, init, consts, outs, refs[k + N_MIXER_OUTS:], L=L, vlo=vlo, vhi=vhi)


def _fused_body(x_ref, shift_ref, scale_ref, g_ref, wbig_ref, wsh_ref, wsl_ref, *refs, L):
    consts = refs[:N_MIXER_CONSTS]
    tail_ref = refs[N_MIXER_CONSTS]
    outs = refs[N_MIXER_CONSTS + 1:N_MIXER_CONSTS + 1 + N_MIXER_OUTS]
    k = N_MIXER_CONSTS + 1 + N_MIXER_OUTS
    zb_s = list(refs[k:k + N_BIG // PROJ_COLS])
    zs_s = refs[k + N_BIG // PROJ_COLS]
    scratch = refs[k + N_BIG // PROJ_COLS + 1:]
    _zero_carried_state(outs, scratch)
    narrow, wide = _inproj_parts(x_ref, shift_ref, scale_ref, g_ref, wbig_ref, wsh_ref, wsl_ref,
                                 zb_s, zs_s, tail_ref, bt=1, tt=L)

    def produce(stage):
        if stage == "mlstm_qk":
            narrow()
            wide(OFF_QKA // PROJ_COLS)
        elif stage == "gla_qk":
            wide(OFF_QB // PROJ_COLS)
        elif stage == "gates":
            for off in (OFF_GA, OFF_OA, OFF_GB, OFF_RB):
                wide(off // PROJ_COLS)
        elif stage == "values":
            wide(OFF_VA // PROJ_COLS)
            wide(OFF_VB // PROJ_COLS)

    _mixer_compute(zb_s, zs_s, None, consts, outs, scratch, L=L, vlo=0, vhi=L, zero_init=False,
                   produce=produce)


def _zero_carried_state(outs, scratch):
    _, _, _, mo_ref, So_ref = outs
    carry_s, caug_s = scratch[-2:]

    @pl.when(pl.program_id(1) == 0)
    def _zero():
        caug_s[...] = jnp.zeros(caug_s.shape, f32)
        mo_ref[...] = jnp.zeros(mo_ref.shape, f32)
        So_ref[...] = jnp.zeros(So_ref.shape, f32)
        carry_s[...] = jnp.zeros(carry_s.shape, f32)


def _mixer_compute(zb_ref, zs_ref, init, consts, outs, scratch, *, L, vlo, vhi, zero_init=True,
                   produce=lambda stage: None):
    sample = init is not None
    if sample:
        conv0_ref, C0_ref, n0_ref, m0_ref, S0_ref = init
    cw_ref, cb_ref, gb_ref, wal_ref, bal_ref, gha_ref, ghb_ref = consts
    u_ref, Co_ref, no_ref, mo_ref, So_ref = outs
    qa_s, ka_s, qh_s, kh_s, kb_s, x_s, r_s, bg_s, ha_s, hb_s, pa_s, pb_s, carry_s, caug_s = scratch
    rows = zs_ref.shape[0]
    every = slice(None)

    def zcols(rsel, off, width):
        if isinstance(zb_ref, (list, tuple)):
            grp, o = divmod(off, PROJ_COLS)
            assert o + width <= PROJ_COLS
            return zb_ref[grp][rsel, o:o + width]
        return zb_ref[rsel, off:off + width]
    n_groups = rows // L
    log2_l = L.bit_length() - 1
    masked = not (vlo == 0 and vhi == L)
    scale = DK ** -0.5

    if not sample and zero_init:
        _zero_carried_state(outs, scratch)

    row_in = lax.broadcasted_iota(jnp.int32, (rows, 1), 0) & (L - 1)
    valid = (row_in >= vlo) & (row_in < vhi)

    produce("mlstm_qk")
    zq = zcols(every, OFF_QKA, 2 * QK).astype(f32)
    if sample:
        zq = jnp.where(row_in < CONV_W - 1, conv0_ref[...], zq)
        carry = jnp.zeros((SUBLANES, 2 * QK), f32)
    else:
        carry = carry_s[...]
        carry_s[...] = zq[rows - SUBLANES:rows, :]
    xe = jnp.concatenate([carry, zq], axis=0)
    cw = cw_ref[...]
    conv = cb_ref[...] + zq * cw[CONV_W - 1:CONV_W]
    for k in range(CONV_W - 1):
        o = SUBLANES - (CONV_W - 1) + k
        conv = conv + xe[o:o + rows] * cw[k:k + 1]
    qk = conv * _sigmoid(conv)
    qa_s[...] = qk[:, :QK] * scale
    ka_s[...] = qk[:, QK:]

    lane = lax.broadcasted_iota(jnp.int32, (1, N_SMALL), 1)
    zs = zs_ref[...]
    pre = zs + gb_ref[...]
    gate = jnp.where(lane < LANE_FG, pre, _log_sigmoid(pre))
    gate_f = jnp.where(valid, gate, 0.0) if masked else gate
    rr = lax.broadcasted_iota(jnp.int32, (rows, rows), 0)
    cc = lax.broadcasted_iota(jnp.int32, (rows, rows), 1)
    tril_mask = ((rr >> log2_l) == (cc >> log2_l)) & (cc <= rr)
    tril = jnp.where(tril_mask, 1.0, 0.0).astype(bf16)

    def cumsum_groups(v):
        hi, mid, lo = _split3(v)
        return _mm(tril, hi) + (_mm(tril, mid) + _mm(tril, lo))

    x_all = jnp.where(lane < LANE_FG, gate, cumsum_groups(gate_f))
    x_s[...] = x_all
    b_al = pltpu.roll(x_all, N_SMALL - N_HEADS, axis=1)
    r_all = x_all - b_al
    r_ok = (lane < N_HEADS) & valid if masked else jnp.broadcast_to(lane < N_HEADS, (rows, N_SMALL))
    r_s[...] = jnp.where(r_ok, r_all, NEG)
    range_m = jnp.max(jnp.where(r_ok, r_all, NEG)) - jnp.min(jnp.where(r_ok, r_all, -NEG))

    produce("gla_qk")
    la = _log_sigmoid(_mm(zs.astype(bf16), wal_ref[...]) + bal_ref[...]) * (1.0 / GLA_TAU)
    kb = zcols(every, OFF_KB, QK).astype(f32)
    if masked:
        la = jnp.where(valid, la, 0.0)
        kb = jnp.where(valid, kb, 0.0)
    bg = cumsum_groups(la)
    bg_s[...] = bg
    range_g = jnp.max(-bg)
    qh_s[...] = zcols(every, OFF_QB, QK).astype(f32) * scale * jnp.exp(bg)
    kh_s[...] = kb * jnp.exp(-bg)
    fast_ok = jnp.logical_and(range_m <= SAFE_LOG_RANGE, range_g <= SAFE_LOG_RANGE)

    produce("gates")
    col = lambda off: zcols(every, off, D_MODEL).astype(f32)
    pa_s[...] = _sigmoid(col(OFF_GA)) * _sigmoid(col(OFF_OA)) * gha_ref[...]
    rb = col(OFF_RB)
    pb_s[...] = _sigmoid(col(OFF_GB)) * (rb * _sigmoid(rb)) * ghb_ref[...]

    tri = lax.broadcasted_iota(jnp.int32, (L, L), 0) >= lax.broadcasted_iota(jnp.int32, (L, L), 1)
    eye = lax.broadcasted_iota(jnp.int32, (DK, DK), 0) == lax.broadcasted_iota(jnp.int32, (DK, DK), 1)
    lane_h = lane
    lrow = lax.broadcasted_iota(jnp.int32, (1, L), 1)
    lcol = lax.broadcasted_iota(jnp.int32, (L, 1), 0)
    vrow = (lrow >= vlo) & (lrow < vhi)
    ones16 = jnp.ones((L, DK), bf16)

    def load_state(g, h):
        if sample:
            n_mat = jnp.broadcast_to(_row_to_col(n0_ref[g, pl.ds(h, 1), :], eye), (DK, DK))
            return jnp.concatenate([C0_ref[g, h], n_mat], axis=1), S0_ref[g, h]
        return caug_s[h], So_ref[0, h]

    def store_state(g, h, caug, s_new):
        sidx = g if sample else 0
        if not sample:
            caug_s[h] = caug
        Co_ref[sidx, h] = caug[:, :DV]
        no_ref[sidx, pl.ds(h, 1), :] = _diag_row(caug[:, DV:], eye)
        So_ref[sidx, h] = s_new

    def group(g, exact):
        static = isinstance(g, int)
        r0 = g * L if static else pl.multiple_of(g * L, L)
        tail0 = r0 + L - SUBLANES if static else pl.multiple_of(r0 + L - SUBLANES, SUBLANES)
        rs = pl.ds(r0, L)
        xc = x_s[rs, :]
        rc = r_s[rs, :]
        if exact:
            sr = lax.broadcasted_iota(jnp.int32, (2 * SUBLANES, N_SMALL), 0)
            sc = lax.broadcasted_iota(jnp.int32, (2 * SUBLANES, N_SMALL), 1)
            sel = jnp.where((sr < N_HEADS) & (sc == sr + LANE_IG), 1.0,
                            jnp.where((sr < N_HEADS) & (sc == sr + LANE_FG), -1.0, 0.0)).astype(bf16)
            xh, xm, xl = _split3(xc)
            r_rows = _mm_nt(sel, xh) + (_mm_nt(sel, xm) + _mm_nt(sel, xl))
        m_prev = m0_ref[g] if sample else mo_ref[0]
        m_vec = m_prev
        bg_tail = bg_s[pl.ds(tail0, SUBLANES), :]
        for h in range(N_HEADS):
            ks = slice(h * DK, (h + 1) * DK)
            vs = slice(h * DV, (h + 1) * DV)
            caug, s_st = load_state(g, h)
            m_st = m_prev[:, h:h + 1]
            b_col = xc[:, LANE_FG + h:LANE_FG + h + 1]
            r_col = rc[:, h:h + 1]
            b_last = b_col[L - 1:L, :]
            v16 = zcols(rs, OFF_VA + h * DV, DV).astype(bf16)
            vb16 = zcols(rs, OFF_VB + h * DV, DV).astype(bf16)
            eb_last = jnp.exp(bg_tail[SUBLANES - 1:SUBLANES, ks])
            eb_col = _row_to_col(eb_last, eye)
            if not exact:
                c0 = jnp.max(r_col, axis=0, keepdims=True)
                mm = jnp.maximum(c0, m_st)
                q16 = qa_s[rs, ks].astype(bf16)
                kt16 = (ka_s[rs, ks] * jnp.exp(r_col - c0)).astype(bf16)
                va = jnp.concatenate([v16, ones16], axis=1)
                s16 = jnp.where(tri, _mm_nt(q16, kt16), 0.0).astype(bf16)
                comb = jnp.exp(c0 - mm) * _mm(s16, va) + jnp.exp(m_st - mm) * _mm(q16, caug.astype(bf16))
                inv = 1.0 / jnp.maximum(jnp.abs(comb[:, DV:]), jnp.exp(-(b_col + mm)))
                ha_s[rs, h * DV:h * DV + DK] = comb[:, :DK] * inv
                ha_s[rs, h * DV + DK:(h + 1) * DV] = comb[:, DK:DV] * inv
                m_new = jnp.maximum(b_last + m_st, b_last + c0)
                caug_new = (jnp.exp(b_last + m_st - m_new) * caug
                            + jnp.exp(b_last + c0 - m_new) * _mm_tn(kt16, va))
                qh16 = qh_s[rs, ks].astype(bf16)
                kh = kh_s[rs, ks]
                a16 = jnp.where(tri, _mm_nt(qh16, kh.astype(bf16)), 0.0).astype(bf16)
                hb_s[rs, vs] = _mm(a16, vb16) + _mm(qh16, s_st.astype(bf16))
                s_new = eb_col * s_st + _mm_tn((kh * eb_last).astype(bf16), vb16)
            else:
                c_st = caug[:, :DV]
                n_row = _diag_row(caug[:, DV:], eye)
                r_row = r_rows[h:h + 1, :]
                if masked:
                    r_row = jnp.where(vrow, r_row, NEG)
                dm = jnp.where(tri, b_col + r_row, NEG)
                m_t = jnp.maximum(jnp.max(dm, axis=1, keepdims=True), b_col + m_st)
                w = jnp.exp(dm - m_t)
                wi = jnp.exp(b_col + m_st - m_t)
                q = qa_s[rs, ks]
                k = ka_s[rs, ks]
                q16 = q.astype(bf16)
                s = _mm_nt(q16, k.astype(bf16)) * w
                num = _mm(s.astype(bf16), v16) + wi * _mm(q16, c_st.astype(bf16))
                den = jnp.sum(s, axis=1, keepdims=True) + wi * jnp.sum(q * n_row, axis=1, keepdims=True)
                ha_s[rs, vs] = num / jnp.maximum(jnp.abs(den), jnp.exp(-m_t))
                m_new = jnp.maximum(b_last + m_st, b_last + jnp.max(r_row, axis=1, keepdims=True))
                wc = jnp.exp(b_last + m_st - m_new)
                kt = k * jnp.exp(b_last + r_col - m_new)
                n_new = wc * n_row + jnp.sum(kt, axis=0, keepdims=True)
                caug_new = jnp.concatenate(
                    [wc * c_st + _mm_tn(kt.astype(bf16), v16),
                     jnp.broadcast_to(_row_to_col(n_new, eye), (DK, DK))], axis=1)
                qb = zcols(rs, OFF_QB + h * DK, DK).astype(f32) * scale
                bgc = bg_s[rs, ks]
                sub = lax.broadcasted_iota(jnp.int32, (SUBLANES, 1), 0)

                def score_col(s_i, a, ks=ks, qb=qb, bgc=bgc, sub=sub):
                    s8 = lax.shift_left(lax.shift_right_logical(s_i, 3), 3)
                    base = pl.multiple_of(r0 + s8, SUBLANES)
                    pick = sub == (s_i - s8)
                    b_s = jnp.sum(jnp.where(pick, bg_s[pl.ds(base, SUBLANES), ks], 0.0), axis=0, keepdims=True)
                    k_s = jnp.sum(jnp.where(pick, kb_s[pl.ds(base, SUBLANES), ks], 0.0), axis=0, keepdims=True)
                    e = jnp.exp(jnp.where(lcol >= s_i, bgc - b_s, NEG))
                    col = jnp.sum(qb * e * k_s, axis=1, keepdims=True)
                    return jnp.where(lrow == s_i, col, a)

                a = lax.fori_loop(0, L, score_col, jnp.zeros((L, L), f32))
                hb_s[rs, vs] = (_mm(a.astype(bf16), vb16)
                                + _mm((qb * jnp.exp(bgc)).astype(bf16), s_st.astype(bf16)))
                kt_b = kb_s[rs, ks] * jnp.exp(bg_tail[SUBLANES - 1:SUBLANES, ks] - bgc)
                s_new = eb_col * s_st + _mm_tn(kt_b.astype(bf16), vb16)
            store_state(g, h, caug_new, s_new)
            m_vec = jnp.where(lane_h == h, m_new, m_vec)
        mo_ref[g if sample else 0] = m_vec

    def groups_fast_batched():
        r_all_m = r_s[...]
        c0_l, m_l, mm_l, fac, wc = [], [], [], [], []
        for g in range(n_groups):
            gs = slice(g * L, (g + 1) * L)
            c0_g = jnp.max(r_all_m[gs], axis=0, keepdims=True)
            m_g = m0_ref[g]
            b_last = b_al[gs][L - 1:L, :]
            m_new = jnp.maximum(b_last + m_g, b_last + c0_g)
            mo_ref[g] = m_new
            c0_l.append(c0_g)
            m_l.append(m_g)
            mm_l.append(jnp.maximum(c0_g, m_g))
            fac.append(jnp.exp(b_last + c0_g - m_new))
            wc.append(jnp.exp(b_last + m_g - m_new))
        rows_of = lambda lst: jnp.concatenate([jnp.broadcast_to(v, (L, N_SMALL)) for v in lst], axis=0)
        c0_r, m_r, mm_r = rows_of(c0_l), rows_of(m_l), rows_of(mm_l)
        beta = jnp.exp(r_all_m - c0_r)
        gi_r = jnp.exp(c0_r - mm_r)
        wi_r = jnp.exp(m_r - mm_r)
        floor = jnp.exp(-(b_al + mm_r))
        ones_g = jnp.ones((L, DK), bf16)
        for h in range(N_HEADS):
            ks = slice(h * DK, (h + 1) * DK)
            vs = slice(h * DV, (h + 1) * DV)
            kt = ka_s[:, ks] * beta[:, h:h + 1]
            v = zcols(every, OFF_VA + h * DV, DV)
            vb = zcols(every, OFF_VB + h * DV, DV)
            q16 = qa_s[:, ks].astype(bf16)
            va = jnp.concatenate([v.astype(bf16), jnp.ones((rows, DK), bf16)], axis=1)
            s16 = jnp.where(tril_mask, _mm_nt(q16, kt.astype(bf16)), 0.0).astype(bf16)
            ni = _mm(s16, va)
            qh16 = qh_s[:, ks].astype(bf16)
            kh = kh_s[:, ks]
            a16 = jnp.where(tril_mask, _mm_nt(qh16, kh.astype(bf16)), 0.0).astype(bf16)
            o_intra = _mm(a16, vb.astype(bf16))
            ci_parts, oi_parts = [], []
            for g in range(n_groups):
                gs = slice(g * L, (g + 1) * L)
                caug, s_st = load_state(g, h)
                va_g = jnp.concatenate([v[gs].astype(bf16), ones_g], axis=1)
                ci_parts.append(_mm(qa_s[gs, ks].astype(bf16), caug.astype(bf16)))
                caug_new = wc[g][:, h:h + 1] * caug + fac[g][:, h:h + 1] * _mm_tn(kt[gs].astype(bf16), va_g)
                oi_parts.append(_mm(qh_s[gs, ks].astype(bf16), s_st.astype(bf16)))
                eb_last = jnp.exp(bg_s[gs, ks][L - 1:L, :])
                s_new = (_row_to_col(eb_last, eye) * s_st
                         + _mm_tn((kh[gs] * eb_last).astype(bf16), vb[gs].astype(bf16)))
                store_state(g, h, caug_new, s_new)
            comb = gi_r[:, h:h + 1] * ni + wi_r[:, h:h + 1] * jnp.concatenate(ci_parts, axis=0)
            inv = 1.0 / jnp.maximum(jnp.abs(comb[:, DV:]), floor[:, h:h + 1])
            ha_s[:, h * DV:h * DV + DK] = comb[:, :DK] * inv
            ha_s[:, h * DV + DK:(h + 1) * DV] = comb[:, DK:DV] * inv
            hb_s[:, vs] = o_intra + jnp.concatenate(oi_parts, axis=0)

    def run_groups(exact):
        if sample and not exact and L == SUBLANES:
            groups_fast_batched()
        elif exact and n_groups > 1:
            def body(g, carry_):
                group(g, exact)
                return carry_
            lax.fori_loop(0, n_groups, body, 0)
        else:
            for g in range(n_groups):
                group(g, exact)

    produce("values")

    @pl.when(fast_ok)
    def _fast():
        run_groups(False)

    @pl.when(jnp.logical_not(fast_ok))
    def _exact():
        kb_s[...] = kb
        run_groups(True)

    for h in range(N_HEADS):
        vs = slice(h * DV, (h + 1) * DV)
        u = pa_s[:, vs] * _rms(ha_s[:, vs]) + pb_s[:, vs] * _rms(hb_s[:, vs])
        u_ref[:, vs] = u.astype(u_ref.dtype)


def _mixer_call(zb, zs, init, consts, *, nb, nt, bt, L, vlo, vhi):
    sample = init is not None
    rows = bt * L
    n_state = nb * bt
    row_spec = lambda n: pl.BlockSpec((rows, n), lambda i, j: (i * nt + j, 0))
    st4 = pl.BlockSpec((bt, N_HEADS, DK, DV), lambda i, j: (i, 0, 0, 0))
    st_n = pl.BlockSpec((bt, N_HEADS, DK), lambda i, j: (i, 0, 0))
    st_m = pl.BlockSpec((bt, 1, N_SMALL), lambda i, j: (i, 0, 0))
    in_specs = [row_spec(N_BIG), row_spec(N_SMALL)]
    args = [zb, zs]
    if sample:
        in_specs += [row_spec(2 * QK), st4, st_n, st_m, st4]
        args += list(init)
    in_specs += [_const_spec(c.shape) for c in consts]
    args += list(consts)
    return pl.pallas_call(
        functools.partial(_mixer_body, L=L, vlo=vlo, vhi=vhi, sample=sample),
        out_shape=(
            jax.ShapeDtypeStruct((zb.shape[0], D_MODEL), bf16),
            jax.ShapeDtypeStruct((n_state, N_HEADS, DK, DV), f32),
            jax.ShapeDtypeStruct((n_state, N_HEADS, DK), f32),
            jax.ShapeDtypeStruct((n_state, 1, N_SMALL), f32),
            jax.ShapeDtypeStruct((n_state, N_HEADS, DK, DV), f32),
        ),
        grid=(nb, nt),
        in_specs=in_specs,
        out_specs=(row_spec(D_MODEL), st4, st_n, st_m, st4),
        scratch_shapes=_mixer_scratch(rows),
        compiler_params=pltpu.CompilerParams(
            dimension_semantics=("arbitrary", "arbitrary"), vmem_limit_bytes=VMEM_LIMIT_BYTES),
        name="mixer_sample" if sample else "mixer_prompt",
    )(*args)


def _fused_call(x2d, ada4, g_norm, w_big, ws_hi, ws_lo, consts, *, nb, nt, L):
    row_spec = lambda n: pl.BlockSpec((L, n), lambda i, j: (i * nt + j, 0))
    st4 = pl.BlockSpec((1, N_HEADS, DK, DV), lambda i, j: (i, 0, 0, 0))
    st_n = pl.BlockSpec((1, N_HEADS, DK), lambda i, j: (i, 0, 0))
    st_m = pl.BlockSpec((1, 1, N_SMALL), lambda i, j: (i, 0, 0))
    return pl.pallas_call(
        functools.partial(_fused_body, L=L),
        out_shape=(
            jax.ShapeDtypeStruct((nb * nt, SUBLANES, PROJ_COLS), f32),
            jax.ShapeDtypeStruct((x2d.shape[0], D_MODEL), bf16),
            jax.ShapeDtypeStruct((nb, N_HEADS, DK, DV), f32),
            jax.ShapeDtypeStruct((nb, N_HEADS, DK), f32),
            jax.ShapeDtypeStruct((nb, 1, N_SMALL), f32),
            jax.ShapeDtypeStruct((nb, N_HEADS, DK, DV), f32),
        ),
        grid=(nb, nt),
        in_specs=[
            row_spec(D_MODEL), _mod_spec(1, 0), _mod_spec(1, 1),
            _const_spec((1, D_MODEL)), _const_spec((D_MODEL, N_BIG)),
            _const_spec((D_MODEL, N_SMALL)), _const_spec((D_MODEL, N_SMALL)),
        ] + [_const_spec(c.shape) for c in consts],
        out_specs=(
            pl.BlockSpec((1, SUBLANES, PROJ_COLS), lambda i, j: (i * nt + j, 0, 0)),
            row_spec(D_MODEL), st4, st_n, st_m, st4,
        ),
        scratch_shapes=([pltpu.VMEM((L, PROJ_COLS), f32)] * (N_BIG // PROJ_COLS)
                        + [pltpu.VMEM((L, N_SMALL), f32)] + _mixer_scratch(L)),
        compiler_params=pltpu.CompilerParams(
            dimension_semantics=("arbitrary", "arbitrary"), vmem_limit_bytes=VMEM_LIMIT_BYTES),
        name="proj_mixer_prompt",
    )(x2d, ada4, ada4, g_norm, w_big, ws_hi, ws_lo, *consts)


def _out_body(u_ref, x_ref, gate1_ref, shift2_ref, scale2_ref, gate2_ref, gn2_ref, gfin_ref,
              wout_ref, wg_ref, wu_ref, wd_ref, y_ref, *, bt, tt):
    x1 = x_ref[...] + _mod_rows(gate1_ref, bt, tt) * _mm(u_ref[...], wout_ref[...])
    h2 = _rms(x1) * gn2_ref[...]
    h2 = (h2 * (1.0 + _mod_rows(scale2_ref, bt, tt)) + _mod_rows(shift2_ref, bt, tt)).astype(bf16)
    acc = jnp.zeros(x1.shape, f32)
    for c in range(0, D_FF, FFN_COLS):
        g = _mm(h2, wg_ref[:, c:c + FFN_COLS])
        up = _mm(h2, wu_ref[:, c:c + FFN_COLS])
        acc = acc + _mm((g * _sigmoid(g) * up).astype(bf16), wd_ref[c:c + FFN_COLS, :])
    x2 = x1 + _mod_rows(gate2_ref, bt, tt) * acc
    y_ref[...] = _rms(x2) * gfin_ref[...]


def _out_call(u2d, x2d, ada4, g_norm2, g_final, w_out, w_gate, w_up, w_down, *, nb, nt, bt, tt):
    rows = bt * tt
    row_spec = pl.BlockSpec((rows, D_MODEL), lambda i, j: (i * nt + j, 0))
    return pl.pallas_call(
        functools.partial(_out_body, bt=bt, tt=tt),
        out_shape=jax.ShapeDtypeStruct(x2d.shape, f32),
        grid=(nb, nt),
        in_specs=[
            row_spec, row_spec, _mod_spec(bt, 2), _mod_spec(bt, 3), _mod_spec(bt, 4), _mod_spec(bt, 5),
            _const_spec((1, D_MODEL)), _const_spec((1, D_MODEL)),
            _const_spec((D_MODEL, D_MODEL)), _const_spec((D_MODEL, D_FF)),
            _const_spec((D_MODEL, D_FF)), _const_spec((D_FF, D_MODEL)),
        ],
        out_specs=row_spec,
        compiler_params=pltpu.CompilerParams(
            dimension_semantics=("arbitrary", "arbitrary"), vmem_limit_bytes=VMEM_LIMIT_BYTES),
        name="out_ffn",
    )(u2d, x2d, ada4, ada4, ada4, ada4, g_norm2, g_final, w_out, w_gate, w_up, w_down)


PROMPT_CHUNK = 256
PROMPT_FFN_ROWS = 512
SAMPLE_L = SUBLANES
SAMPLE_MIX_GROUPS = 8
SAMPLE_DENSE_GROUPS = 32


def _prep_layer_weights(w_in, w_alpha, b_igate, b_fgate):
    s = [0, 1024, 2048, 3072, 3076, 3080, 3592, 4104, 5128, 6152, 6168, 7192, 8216]
    w_big = jnp.concatenate([w_in[:, s[0]:s[3]], w_in[:, s[5]:s[9]], w_in[:, s[10]:s[12]]], axis=1).astype(bf16)
    w_small = jnp.concatenate(
        [w_in[:, s[3]:s[5]], w_in[:, s[9]:s[10]],
         jnp.zeros((D_MODEL, N_SMALL - 2 * N_HEADS - GLA_RANK), f32)], axis=1)
    ws_hi = w_small.astype(bf16)
    ws_lo = (w_small - ws_hi.astype(f32)).astype(bf16)
    w_al = jnp.zeros((N_SMALL, QK), f32).at[LANE_LR:LANE_LR + GLA_RANK].set(w_alpha).astype(bf16)
    g_bias = jnp.concatenate([b_igate, b_fgate, jnp.zeros((N_SMALL - 2 * N_HEADS,), f32)])[None, :]
    return w_big, ws_hi, ws_lo, w_al, g_bias


def _layer(xp2d, xs2d, st_s, ada_p, ada_s, p, *, bp, tp, bs):
    (g_norm1, g_norm2, w_in, conv_w, conv_b, b_igate, b_fgate, g_head_a,
     w_alpha, b_alpha, g_head_b, w_out, w_ffn_gate, w_ffn_up, w_ffn_down, g_final_or_ones) = p
    w_big, ws_hi, ws_lo, w_al, g_bias = _prep_layer_weights(w_in, w_alpha, b_igate, b_fgate)
    consts = (conv_w, conv_b[None, :], g_bias, w_al, b_alpha[None, :],
              g_head_a.reshape(1, D_MODEL), g_head_b.reshape(1, D_MODEL))
    w_out16, w_g16, w_u16, w_d16 = (w.astype(bf16) for w in (w_out, w_ffn_gate, w_ffn_up, w_ffn_down))
    gn1, gn2, gfin = g_norm1[None, :], g_norm2[None, :], g_final_or_ones[None, :]

    ntp = tp // PROMPT_CHUNK
    tail, u, c_p, n_p, m_p, s_p = _fused_call(xp2d, ada_p, gn1, w_big, ws_hi, ws_lo, consts,
                                              nb=bp, nt=ntp, L=PROMPT_CHUNK)
    conv_p = tail.reshape(bp, ntp, SUBLANES, 2 * QK)[:, ntp - 1, SUBLANES - (CONV_W - 1):, :]
    yp = _out_call(u, xp2d, ada_p, gn2, gfin, w_out16, w_g16, w_u16, w_d16,
                   nb=bp, nt=tp // PROMPT_FFN_ROWS, bt=1, tt=PROMPT_FFN_ROWS)

    conv0, c0, n0, m0, s0 = st_s
    nbd = bs // SAMPLE_DENSE_GROUPS
    zb_s, zs_s, _ = _inproj_call(xs2d, ada_s, gn1, w_big, ws_hi, ws_lo,
                                 nb=nbd, nt=1, bt=SAMPLE_DENSE_GROUPS, tt=SAMPLE_L, zdtype=f32)
    lo = CONV_W - 1
    n_tok = SAMPLE_L - lo - 1
    conv_s = zb_s[:, OFF_QKA:OFF_QKA + 2 * QK].reshape(bs, SAMPLE_L, 2 * QK)[:, n_tok:lo + n_tok, :]
    conv0_rows = jnp.pad(conv0, ((0, 0), (0, SAMPLE_L - lo), (0, 0))).reshape(bs * SAMPLE_L, 2 * QK)
    m0_lanes = jnp.pad(m0[:, None, :], ((0, 0), (0, 0), (0, N_SMALL - N_HEADS)))
    u_s, c_s, n_s, m_s, s_s = _mixer_call(
        zb_s, zs_s, (conv0_rows, c0, n0, m0_lanes, s0), consts,
        nb=bs // SAMPLE_MIX_GROUPS, nt=1, bt=SAMPLE_MIX_GROUPS, L=SAMPLE_L, vlo=lo, vhi=lo + n_tok)
    ys = _out_call(u_s, xs2d, ada_s, gn2, gfin, w_out16, w_g16, w_u16, w_d16,
                   nb=nbd, nt=1, bt=SAMPLE_DENSE_GROUPS, tt=SAMPLE_L)
    st_p = (conv_p, c_p, n_p, m_p[:, 0, :N_HEADS], s_p)
    st_s_new = (conv_s, c_s, n_s, m_s[:, 0, :N_HEADS], s_s)
    return yp, ys, st_p, st_s_new


def kernel(x_prompt, x_sample, state_conv, state_C, state_n, state_m, state_S, c_prompt, c_sample,
           g_norm1, g_norm2, w_ada, b_ada, w_in, conv_w, conv_b, b_igate, b_fgate, g_head_a,
           w_alpha, b_alpha, g_head_b, w_out, w_ffn_gate, w_ffn_up, w_ffn_down, g_final):
    bp, tp, _ = x_prompt.shape
    bs, ts, _ = x_sample.shape
    depth = w_in.shape[0]
    assert depth == 1, "the final norm is fused into the (single) layer's FFN kernel"
    lo = CONV_W - 1
    assert ts == SAMPLE_L - lo - 1
    xp2d = x_prompt.reshape(bp * tp, D_MODEL)
    xs2d = jnp.pad(x_sample, ((0, 0), (lo, SAMPLE_L - lo - ts), (0, 0))).reshape(bs * SAMPLE_L, D_MODEL)
    c_all = jnp.concatenate([c_prompt, c_sample], axis=0)
    new_p = [[] for _ in range(5)]
    new_s = [[] for _ in range(5)]
    for l in range(depth):
        ada = _ada_call(c_all, w_ada[l], b_ada[l][None, :]).reshape(bp + bs, N_ADA, 1, D_MODEL)
        p = (g_norm1[l], g_norm2[l], w_in[l], conv_w[l], conv_b[l], b_igate[l], b_fgate[l], g_head_a[l],
             w_alpha[l], b_alpha[l], g_head_b[l], w_out[l], w_ffn_gate[l], w_ffn_up[l], w_ffn_down[l], g_final)
        st_s = (state_conv[l], state_C[l], state_n[l], state_m[l], state_S[l])
        xp2d, xs2d, st_p, st_s_new = _layer(xp2d, xs2d, st_s, ada[:bp], ada[bp:], p, bp=bp, tp=tp, bs=bs)
        for j in range(5):
            new_p[j].append(st_p[j])
            new_s[j].append(st_s_new[j])
    y_prompt = xp2d.reshape(bp, tp, D_MODEL)
    y_sample = xs2d.reshape(bs, SAMPLE_L, D_MODEL)[:, lo:lo + ts, :]
    outs_p = [jnp.stack(a, axis=0) for a in new_p]
    outs_s = [jnp.stack(a, axis=0) for a in new_s]
    return (y_prompt, y_sample, *outs_p, *outs_s)
```

```python
import functools

import jax
import jax.numpy as jnp
from jax import lax
from jax.experimental import pallas as pl
from jax.experimental.pallas import tpu as pltpu

f32 = jnp.float32
bf16 = jnp.bfloat16

D_MODEL = 1024
N_HEADS = 4
DK = 128
DV = 256
QK = N_HEADS * DK
CONV_W = 4
GLA_RANK = 16
GLA_TAU = 16.0
D_FF = 2816
EPS = 1e-6
N_ADA = 6

N_BIG = 8192
N_SMALL = 128
OFF_QKA, OFF_VA, OFF_OA, OFF_QB, OFF_KB, OFF_VB, OFF_RB, OFF_GA, OFF_GB = (
    0, 1024, 2048, 3072, 3584, 4096, 5120, 6144, 7168)
LANE_IG, LANE_FG, LANE_LR = 0, N_HEADS, 2 * N_HEADS

SUBLANES = 8
NEG = -1e30
SAFE_LOG_RANGE = 60.0
V7X_VMEM_BYTES = 64 * 1024 * 1024
VMEM_LIMIT_BYTES = V7X_VMEM_BYTES - 8 * 1024 * 1024
FFN_COLS = 256
PROJ_COLS = 1024
DVA = DV + DK


def _mm(a, b):
    return jnp.dot(a, b, preferred_element_type=f32)


def _mm_nt(a, b):
    return lax.dot_general(a, b, (((1,), (1,)), ((), ())), preferred_element_type=f32)


def _mm_tn(a, b):
    return lax.dot_general(a, b, (((0,), (0,)), ((), ())), preferred_element_type=f32)


def _split3(x):
    hi = x.astype(bf16)
    r = x - hi.astype(f32)
    mid = r.astype(bf16)
    lo = (r - mid.astype(f32)).astype(bf16)
    return hi, mid, lo


def _sigmoid(x):
    return 0.5 * jnp.tanh(0.5 * x) + 0.5


def _log_sigmoid(x):
    return jnp.minimum(x, 0.0) - jnp.log(1.0 + jnp.exp(-jnp.abs(x)))


def _rms(x):
    return x * lax.rsqrt(jnp.mean(x * x, axis=-1, keepdims=True) + EPS)


def _row_to_col(row, eye):
    return jnp.sum(jnp.where(eye, row, 0.0), axis=1, keepdims=True)


def _diag_row(mat, eye):
    return jnp.sum(jnp.where(eye, mat, 0.0), axis=0, keepdims=True)


def _mod_rows(mod_ref, bt, tt):
    if bt == 1:
        return mod_ref[0, 0]
    v = mod_ref[...].reshape(bt, 1, D_MODEL)
    return jnp.broadcast_to(v, (bt, tt, D_MODEL)).reshape(bt * tt, D_MODEL)


def _ada_body(c_ref, w_ref, b_ref, o_ref):
    ch, cl, _ = _split3(c_ref[...])
    wh, wl, _ = _split3(w_ref[...])
    o_ref[...] = _mm(ch, wh) + (_mm(cl, wh) + _mm(ch, wl)) + b_ref[...]


def _ada_call(c_all, w_ada, b_ada):
    m = c_all.shape[0]
    n = w_ada.shape[1]
    return pl.pallas_call(
        _ada_body,
        out_shape=jax.ShapeDtypeStruct((m, n), f32),
        grid=(n // D_MODEL,),
        in_specs=[
            pl.BlockSpec((m, D_MODEL), lambda i: (0, 0)),
            pl.BlockSpec((D_MODEL, D_MODEL), lambda i: (0, i)),
            pl.BlockSpec((1, D_MODEL), lambda i: (0, i)),
        ],
        out_specs=pl.BlockSpec((m, D_MODEL), lambda i: (0, i)),
        compiler_params=pltpu.CompilerParams(
            dimension_semantics=("arbitrary",), vmem_limit_bytes=VMEM_LIMIT_BYTES),
        name="ada_proj",
    )(c_all, w_ada, b_ada)


def _inproj_parts(x_ref, shift_ref, scale_ref, g_ref, wbig_ref, wsh_ref, wsl_ref,
                  zb_ref, zs_ref, tail_ref, *, bt, tt):
    rows = bt * tt
    h = _rms(x_ref[...]) * g_ref[...]
    h = h * (1.0 + _mod_rows(scale_ref, bt, tt)) + _mod_rows(shift_ref, bt, tt)
    hh = h.astype(bf16)
    hl = (h - hh.astype(f32)).astype(bf16)

    def narrow():
        wsh = wsh_ref[...]
        zs_ref[...] = _mm(hh, wsh) + (_mm(hl, wsh) + _mm(hh, wsl_ref[...]))

    def wide(idx):
        c = idx * PROJ_COLS
        z = _mm(hh, wbig_ref[:, c:c + PROJ_COLS])
        if isinstance(zb_ref, (list, tuple)):
            zb_ref[idx][...] = z
        else:
            zb_ref[:, c:c + PROJ_COLS] = z.astype(zb_ref.dtype)
        if c == OFF_QKA:
            tail_ref[0] = z[rows - SUBLANES:rows, :]

    return narrow, wide


def _inproj_body(*refs, bt, tt):
    narrow, wide = _inproj_parts(*refs, bt=bt, tt=tt)
    narrow()
    for idx in range(N_BIG // PROJ_COLS):
        wide(idx)


def _const_spec(shape):
    nd = len(shape)
    return pl.BlockSpec(shape, lambda i, j: (0,) * nd, pipeline_mode=pl.Buffered(1))


def _mod_spec(bt, k):
    return pl.BlockSpec((bt, 1, 1, D_MODEL), lambda i, j: (i, k, 0, 0))


def _inproj_call(x2d, ada4, g_norm, w_big, ws_hi, ws_lo, *, nb, nt, bt, tt, zdtype):
    rows = bt * tt
    total = x2d.shape[0]
    row_spec = lambda n: pl.BlockSpec((rows, n), lambda i, j: (i * nt + j, 0))
    return pl.pallas_call(
        functools.partial(_inproj_body, bt=bt, tt=tt),
        out_shape=(
            jax.ShapeDtypeStruct((total, N_BIG), zdtype),
            jax.ShapeDtypeStruct((total, N_SMALL), f32),
            jax.ShapeDtypeStruct((nb * nt, SUBLANES, PROJ_COLS), f32),
        ),
        grid=(nb, nt),
        in_specs=[
            row_spec(D_MODEL), _mod_spec(bt, 0), _mod_spec(bt, 1),
            _const_spec((1, D_MODEL)), _const_spec((D_MODEL, N_BIG)),
            _const_spec((D_MODEL, N_SMALL)), _const_spec((D_MODEL, N_SMALL)),
        ],
        out_specs=(
            row_spec(N_BIG), row_spec(N_SMALL),
            pl.BlockSpec((1, SUBLANES, PROJ_COLS), lambda i, j: (i * nt + j, 0, 0)),
        ),
        compiler_params=pltpu.CompilerParams(
            dimension_semantics=("arbitrary", "arbitrary"), vmem_limit_bytes=VMEM_LIMIT_BYTES),
        name="in_proj",
    )(x2d, ada4, ada4, g_norm, w_big, ws_hi, ws_lo)


N_MIXER_CONSTS = 7
N_MIXER_INIT = 5
N_MIXER_OUTS = 5


def _mixer_scratch(rows):
    return [pltpu.VMEM((rows, QK), f32)] * 5 + [
        pltpu.VMEM((rows, N_SMALL), f32), pltpu.VMEM((rows, N_SMALL), f32), pltpu.VMEM((rows, QK), f32),
    ] + [pltpu.VMEM((rows, D_MODEL), f32)] * 4 + [
        pltpu.VMEM((SUBLANES, 2 * QK), f32), pltpu.VMEM((N_HEADS, DK, DVA), f32),
    ]


def _mixer_body(*refs, L, vlo, vhi, sample):
    zb_ref, zs_ref = refs[:2]
    k = 2
    init = None
    if sample:
        init = refs[k:k + N_MIXER_INIT]
        k += N_MIXER_INIT
    consts = refs[k:k + N_MIXER_CONSTS]
    k += N_MIXER_CONSTS
    outs = refs[k:k + N_MIXER_OUTS]
    _mixer_compute(zb_ref, zs_ref, init, consts, outs, refs[k + N_MIXER_OUTS:], L=L, vlo=vlo, vhi=vhi)


def _fused_body(x_ref, shift_ref, scale_ref, g_ref, wbig_ref, wsh_ref, wsl_ref, *refs, L):
    consts = refs[:N_MIXER_CONSTS]
    tail_ref = refs[N_MIXER_CONSTS]
    outs = refs[N_MIXER_CONSTS + 1:N_MIXER_CONSTS + 1 + N_MIXER_OUTS]
    k = N_MIXER_CONSTS + 1 + N_MIXER_OUTS
    zb_s = list(refs[k:k + N_BIG // PROJ_COLS])
    zs_s = refs[k + N_BIG // PROJ_COLS]
    scratch = refs[k + N_BIG // PROJ_COLS + 1:]
    _zero_carried_state(outs, scratch)
    narrow, wide = _inproj_parts(x_ref, shift_ref, scale_ref, g_ref, wbig_ref, wsh_ref, wsl_ref,
                                 zb_s, zs_s, tail_ref, bt=1, tt=L)

    plan = {
        "start": (OFF_QKA, OFF_QB),
        "gate_cumsum": (OFF_GA, OFF_GB),
        "gla_cumsum": (OFF_OA, OFF_RB),
        "gate_products": (OFF_VA, OFF_VB),
    }

    def produce(stage):
        if stage == "start":
            narrow()
        for off in plan[stage]:
            wide(off // PROJ_COLS)

    _mixer_compute(zb_s, zs_s, None, consts, outs, scratch, L=L, vlo=0, vhi=L, zero_init=False,
                   produce=produce)


def _zero_carried_state(outs, scratch):
    _, _, _, mo_ref, So_ref = outs
    carry_s, caug_s = scratch[-2:]

    @pl.when(pl.program_id(1) == 0)
    def _zero():
        caug_s[...] = jnp.zeros(caug_s.shape, f32)
        mo_ref[...] = jnp.zeros(mo_ref.shape, f32)
        So_ref[...] = jnp.zeros(So_ref.shape, f32)
        carry_s[...] = jnp.zeros(carry_s.shape, f32)


def _mixer_compute(zb_ref, zs_ref, init, consts, outs, scratch, *, L, vlo, vhi, zero_init=True,
                   produce=lambda stage: None):
    sample = init is not None
    if sample:
        conv0_ref, C0_ref, n0_ref, m0_ref, S0_ref = init
    cw_ref, cb_ref, gb_ref, wal_ref, bal_ref, gha_ref, ghb_ref = consts
    u_ref, Co_ref, no_ref, mo_ref, So_ref = outs
    qa_s, ka_s, qh_s, kh_s, kb_s, x_s, r_s, bg_s, ha_s, hb_s, pa_s, pb_s, carry_s, caug_s = scratch
    rows = zs_ref.shape[0]
    every = slice(None)

    def zcols(rsel, off, width):
        if isinstance(zb_ref, (list, tuple)):
            grp, o = divmod(off, PROJ_COLS)
            assert o + width <= PROJ_COLS
            return zb_ref[grp][rsel, o:o + width]
        return zb_ref[rsel, off:off + width]
    n_groups = rows // L
    log2_l = L.bit_length() - 1
    masked = not (vlo == 0 and vhi == L)
    scale = DK ** -0.5

    if not sample and zero_init:
        _zero_carried_state(outs, scratch)

    row_in = lax.broadcasted_iota(jnp.int32, (rows, 1), 0) & (L - 1)
    valid = (row_in >= vlo) & (row_in < vhi)

    produce("start")
    zq =zcols(every, OFF_QKA, 2 * QK).astype(f32)
    if sample:
        zq = jnp.where(row_in < CONV_W - 1, conv0_ref[...], zq)
        carry = jnp.zeros((SUBLANES, 2 * QK), f32)
    else:
        carry = carry_s[...]
        carry_s[...] = zq[rows - SUBLANES:rows, :]
    xe = jnp.concatenate([carry, zq], axis=0)
    cw = cw_ref[...]
    conv = cb_ref[...] + zq * cw[CONV_W - 1:CONV_W]
    for k in range(CONV_W - 1):
        o = SUBLANES - (CONV_W - 1) + k
        conv = conv + xe[o:o + rows] * cw[k:k + 1]
    qk = conv * _sigmoid(conv)
    qa_s[...] = qk[:, :QK] * scale
    ka_s[...] = qk[:, QK:]

    lane = lax.broadcasted_iota(jnp.int32, (1, N_SMALL), 1)
    zs = zs_ref[...]
    pre = zs + gb_ref[...]
    gate = jnp.where(lane < LANE_FG, pre, _log_sigmoid(pre))
    gate_f = jnp.where(valid, gate, 0.0) if masked else gate
    rr = lax.broadcasted_iota(jnp.int32, (rows, rows), 0)
    cc = lax.broadcasted_iota(jnp.int32, (rows, rows), 1)
    tril_mask = ((rr >> log2_l) == (cc >> log2_l)) & (cc <= rr)
    tril = jnp.where(tril_mask, 1.0, 0.0).astype(bf16)

    def cumsum_groups(v):
        hi, mid, lo = _split3(v)
        return _mm(tril, hi) + (_mm(tril, mid) + _mm(tril, lo))

    produce("gate_cumsum")
    x_all = jnp.where(lane < LANE_FG, gate, cumsum_groups(gate_f))
    x_s[...] = x_all
    b_al = pltpu.roll(x_all, N_SMALL - N_HEADS, axis=1)
    r_all = x_all - b_al
    r_ok = (lane < N_HEADS) & valid if masked else jnp.broadcast_to(lane < N_HEADS, (rows, N_SMALL))
    r_s[...] = jnp.where(r_ok, r_all, NEG)
    range_m = jnp.max(jnp.where(r_ok, r_all, NEG)) - jnp.min(jnp.where(r_ok, r_all, -NEG))

    la = _log_sigmoid(_mm(zs.astype(bf16), wal_ref[...]) + bal_ref[...]) * (1.0 / GLA_TAU)
    kb = zcols(every, OFF_KB, QK).astype(f32)
    if masked:
        la = jnp.where(valid, la, 0.0)
        kb = jnp.where(valid, kb, 0.0)
    produce("gla_cumsum")
    bg = cumsum_groups(la)
    bg_s[...] = bg
    range_g = jnp.max(-bg)
    qh_s[...] = zcols(every, OFF_QB, QK).astype(f32) * scale * jnp.exp(bg)
    kh_s[...] = kb * jnp.exp(-bg)
    fast_ok = jnp.logical_and(range_m <= SAFE_LOG_RANGE, range_g <= SAFE_LOG_RANGE)

    produce("gate_products")
    col = lambda off: zcols(every, off, D_MODEL).astype(f32)
    pa_s[...] = _sigmoid(col(OFF_GA)) * _sigmoid(col(OFF_OA)) * gha_ref[...]
    rb = col(OFF_RB)
    pb_s[...] = _sigmoid(col(OFF_GB)) * (rb * _sigmoid(rb)) * ghb_ref[...]

    tri = lax.broadcasted_iota(jnp.int32, (L, L), 0) >= lax.broadcasted_iota(jnp.int32, (L, L), 1)
    eye = lax.broadcasted_iota(jnp.int32, (DK, DK), 0) == lax.broadcasted_iota(jnp.int32, (DK, DK), 1)
    lane_h = lane
    lrow = lax.broadcasted_iota(jnp.int32, (1, L), 1)
    lcol = lax.broadcasted_iota(jnp.int32, (L, 1), 0)
    vrow = (lrow >= vlo) & (lrow < vhi)
    ones16 = jnp.ones((L, DK), bf16)

    def load_state(g, h):
        if sample:
            n_mat = jnp.broadcast_to(_row_to_col(n0_ref[g, pl.ds(h, 1), :], eye), (DK, DK))
            return jnp.concatenate([C0_ref[g, h], n_mat], axis=1), S0_ref[g, h]
        return caug_s[h], So_ref[0, h]

    def store_state(g, h, caug, s_new):
        sidx = g if sample else 0
        if not sample:
            caug_s[h] = caug
        Co_ref[sidx, h] = caug[:, :DV]
        no_ref[sidx, pl.ds(h, 1), :] = _diag_row(caug[:, DV:], eye)
        So_ref[sidx, h] = s_new

    def group(g, exact):
        static = isinstance(g, int)
        r0 = g * L if static else pl.multiple_of(g * L, L)
        tail0 = r0 + L - SUBLANES if static else pl.multiple_of(r0 + L - SUBLANES, SUBLANES)
        rs = pl.ds(r0, L)
        xc = x_s[rs, :]
        rc = r_s[rs, :]
        if exact:
            sr = lax.broadcasted_iota(jnp.int32, (2 * SUBLANES, N_SMALL), 0)
            sc = lax.broadcasted_iota(jnp.int32, (2 * SUBLANES, N_SMALL), 1)
            sel = jnp.where((sr < N_HEADS) & (sc == sr + LANE_IG), 1.0,
                            jnp.where((sr < N_HEADS) & (sc == sr + LANE_FG), -1.0, 0.0)).astype(bf16)
            xh, xm, xl = _split3(xc)
            r_rows = _mm_nt(sel, xh) + (_mm_nt(sel, xm) + _mm_nt(sel, xl))
        m_prev = m0_ref[g] if sample else mo_ref[0]
        m_vec = m_prev
        bg_tail = bg_s[pl.ds(tail0, SUBLANES), :]
        for h in range(N_HEADS):
            ks = slice(h * DK, (h + 1) * DK)
            vs = slice(h * DV, (h + 1) * DV)
            caug, s_st = load_state(g, h)
            m_st = m_prev[:, h:h + 1]
            b_col = xc[:, LANE_FG + h:LANE_FG + h + 1]
            r_col = rc[:, h:h + 1]
            b_last = b_col[L - 1:L, :]
            v16 = zcols(rs, OFF_VA + h * DV, DV).astype(bf16)
            vb16 = zcols(rs, OFF_VB + h * DV, DV).astype(bf16)
            eb_last = jnp.exp(bg_tail[SUBLANES - 1:SUBLANES, ks])
            eb_col = _row_to_col(eb_last, eye)
            if not exact:
                c0 = jnp.max(r_col, axis=0, keepdims=True)
                mm = jnp.maximum(c0, m_st)
                q16 = qa_s[rs, ks].astype(bf16)
                kt16 = (ka_s[rs, ks] * jnp.exp(r_col - c0)).astype(bf16)
                va = jnp.concatenate([v16, ones16], axis=1)
                s16 = jnp.where(tri, _mm_nt(q16, kt16), 0.0).astype(bf16)
                comb = jnp.exp(c0 - mm) * _mm(s16, va) + jnp.exp(m_st - mm) * _mm(q16, caug.astype(bf16))
                inv = 1.0 / jnp.maximum(jnp.abs(comb[:, DV:]), jnp.exp(-(b_col + mm)))
                ha_s[rs, h * DV:h * DV + DK] = comb[:, :DK] * inv
                ha_s[rs, h * DV + DK:(h + 1) * DV] = comb[:, DK:DV] * inv
                m_new = jnp.maximum(b_last + m_st, b_last + c0)
                caug_new = (jnp.exp(b_last + m_st - m_new) * caug
                            + jnp.exp(b_last + c0 - m_new) * _mm_tn(kt16, va))
                qh16 = qh_s[rs, ks].astype(bf16)
                kh = kh_s[rs, ks]
                a16 = jnp.where(tri, _mm_nt(qh16, kh.astype(bf16)), 0.0).astype(bf16)
                hb_s[rs, vs] = _mm(a16, vb16) + _mm(qh16, s_st.astype(bf16))
                s_new = eb_col * s_st + _mm_tn((kh * eb_last).astype(bf16), vb16)
            else:
                c_st = caug[:, :DV]
                n_row = _diag_row(caug[:, DV:], eye)
                r_row = r_rows[h:h + 1, :]
                if masked:
                    r_row = jnp.where(vrow, r_row, NEG)
                dm = jnp.where(tri, b_col + r_row, NEG)
                m_t = jnp.maximum(jnp.max(dm, axis=1, keepdims=True), b_col + m_st)
                w = jnp.exp(dm - m_t)
                wi = jnp.exp(b_col + m_st - m_t)
                q = qa_s[rs, ks]
                k = ka_s[rs, ks]
                q16 = q.astype(bf16)
                s = _mm_nt(q16, k.astype(bf16)) * w
                num = _mm(s.astype(bf16), v16) + wi * _mm(q16, c_st.astype(bf16))
                den = jnp.sum(s, axis=1, keepdims=True) + wi * jnp.sum(q * n_row, axis=1, keepdims=True)
                ha_s[rs, vs] = num / jnp.maximum(jnp.abs(den), jnp.exp(-m_t))
                m_new = jnp.maximum(b_last + m_st, b_last + jnp.max(r_row, axis=1, keepdims=True))
                wc = jnp.exp(b_last + m_st - m_new)
                kt = k * jnp.exp(b_last + r_col - m_new)
                n_new = wc * n_row + jnp.sum(kt, axis=0, keepdims=True)
                caug_new = jnp.concatenate(
                    [wc * c_st + _mm_tn(kt.astype(bf16), v16),
                     jnp.broadcast_to(_row_to_col(n_new, eye), (DK, DK))], axis=1)
                qb = zcols(rs, OFF_QB + h * DK, DK).astype(f32) * scale
                bgc = bg_s[rs, ks]
                sub = lax.broadcasted_iota(jnp.int32, (SUBLANES, 1), 0)

                def score_col(s_i, a, ks=ks, qb=qb, bgc=bgc, sub=sub):
                    s8 = lax.shift_left(lax.shift_right_logical(s_i, 3), 3)
                    base = pl.multiple_of(r0 + s8, SUBLANES)
                    pick = sub == (s_i - s8)
                    b_s = jnp.sum(jnp.where(pick, bg_s[pl.ds(base, SUBLANES), ks], 0.0), axis=0, keepdims=True)
                    k_s = jnp.sum(jnp.where(pick, kb_s[pl.ds(base, SUBLANES), ks], 0.0), axis=0, keepdims=True)
                    e = jnp.exp(jnp.where(lcol >= s_i, bgc - b_s, NEG))
                    col = jnp.sum(qb * e * k_s, axis=1, keepdims=True)
                    return jnp.where(lrow == s_i, col, a)

                a = lax.fori_loop(0, L, score_col, jnp.zeros((L, L), f32))
                hb_s[rs, vs] = (_mm(a.astype(bf16), vb16)
                                + _mm((qb * jnp.exp(bgc)).astype(bf16), s_st.astype(bf16)))
                kt_b = kb_s[rs, ks] * jnp.exp(bg_tail[SUBLANES - 1:SUBLANES, ks] - bgc)
                s_new = eb_col * s_st + _mm_tn(kt_b.astype(bf16), vb16)
            store_state(g, h, caug_new, s_new)
            m_vec = jnp.where(lane_h == h, m_new, m_vec)
        mo_ref[g if sample else 0] = m_vec

    def groups_fast_batched():
        r_all_m = r_s[...]
        c0_l, m_l, mm_l, fac, wc = [], [], [], [], []
        for g in range(n_groups):
            gs = slice(g * L, (g + 1) * L)
            c0_g = jnp.max(r_all_m[gs], axis=0, keepdims=True)
            m_g = m0_ref[g]
            b_last = b_al[gs][L - 1:L, :]
            m_new = jnp.maximum(b_last + m_g, b_last + c0_g)
            mo_ref[g] = m_new
            c0_l.append(c0_g)
            m_l.append(m_g)
            mm_l.append(jnp.maximum(c0_g, m_g))
            fac.append(jnp.exp(b_last + c0_g - m_new))
            wc.append(jnp.exp(b_last + m_g - m_new))
        rows_of = lambda lst: jnp.concatenate([jnp.broadcast_to(v, (L, N_SMALL)) for v in lst], axis=0)
        c0_r, m_r, mm_r = rows_of(c0_l), rows_of(m_l), rows_of(mm_l)
        beta = jnp.exp(r_all_m - c0_r)
        gi_r = jnp.exp(c0_r - mm_r)
        wi_r = jnp.exp(m_r - mm_r)
        floor = jnp.exp(-(b_al + mm_r))
        ones_g = jnp.ones((L, DK), bf16)
        for h in range(N_HEADS):
            ks = slice(h * DK, (h + 1) * DK)
            vs = slice(h * DV, (h + 1) * DV)
            kt = ka_s[:, ks] * beta[:, h:h + 1]
            v = zcols(every, OFF_VA + h * DV, DV)
            vb = zcols(every, OFF_VB + h * DV, DV)
            q16 = qa_s[:, ks].astype(bf16)
            va = jnp.concatenate([v.astype(bf16), jnp.ones((rows, DK), bf16)], axis=1)
            s16 = jnp.where(tril_mask, _mm_nt(q16, kt.astype(bf16)), 0.0).astype(bf16)
            ni = _mm(s16, va)
            qh16 = qh_s[:, ks].astype(bf16)
            kh = kh_s[:, ks]
            a16 = jnp.where(tril_mask, _mm_nt(qh16, kh.astype(bf16)), 0.0).astype(bf16)
            o_intra = _mm(a16, vb.astype(bf16))
            ci_parts, oi_parts = [], []
            for g in range(n_groups):
                gs = slice(g * L, (g + 1) * L)
                caug, s_st = load_state(g, h)
                va_g = jnp.concatenate([v[gs].astype(bf16), ones_g], axis=1)
                ci_parts.append(_mm(qa_s[gs, ks].astype(bf16), caug.astype(bf16)))
                caug_new = wc[g][:, h:h + 1] * caug + fac[g][:, h:h + 1] * _mm_tn(kt[gs].astype(bf16), va_g)
                oi_parts.append(_mm(qh_s[gs, ks].astype(bf16), s_st.astype(bf16)))
                eb_last = jnp.exp(bg_s[gs, ks][L - 1:L, :])
                s_new = (_row_to_col(eb_last, eye) * s_st
                         + _mm_tn((kh[gs] * eb_last).astype(bf16), vb[gs].astype(bf16)))
                store_state(g, h, caug_new, s_new)
            comb = gi_r[:, h:h + 1] * ni + wi_r[:, h:h + 1] * jnp.concatenate(ci_parts, axis=0)
            inv = 1.0 / jnp.maximum(jnp.abs(comb[:, DV:]), floor[:, h:h + 1])
            ha_s[:, h * DV:h * DV + DK] = comb[:, :DK] * inv
            ha_s[:, h * DV + DK:(h + 1) * DV] = comb[:, DK:DV] * inv
            hb_s[:, vs] = o_intra + jnp.concatenate(oi_parts, axis=0)

    def run_groups(exact):
        if sample and not exact and L == SUBLANES:
            groups_fast_batched()
        elif exact and n_groups > 1:
            def body(g, carry_):
                group(g, exact)
                return carry_
            lax.fori_loop(0, n_groups, body, 0)
        else:
            for g in range(n_groups):
                group(g, exact)

    @pl.when(fast_ok)
    def _fast():
        run_groups(False)

    @pl.when(jnp.logical_not(fast_ok))
    def _exact():
        kb_s[...] = kb
        run_groups(True)

    for h in range(N_HEADS):
        vs = slice(h * DV, (h + 1) * DV)
        u = pa_s[:, vs] * _rms(ha_s[:, vs]) + pb_s[:, vs] * _rms(hb_s[:, vs])
        u_ref[:, vs] = u.astype(u_ref.dtype)


def _mixer_call(zb, zs, init, consts, *, nb, nt, bt, L, vlo, vhi):
    sample = init is not None
    rows = bt * L
    n_state = nb * bt
    row_spec = lambda n: pl.BlockSpec((rows, n), lambda i, j: (i * nt + j, 0))
    st4 = pl.BlockSpec((bt, N_HEADS, DK, DV), lambda i, j: (i, 0, 0, 0))
    st_n = pl.BlockSpec((bt, N_HEADS, DK), lambda i, j: (i, 0, 0))
    st_m = pl.BlockSpec((bt, 1, N_SMALL), lambda i, j: (i, 0, 0))
    in_specs = [row_spec(N_BIG), row_spec(N_SMALL)]
    args = [zb, zs]
    if sample:
        in_specs += [row_spec(2 * QK), st4, st_n, st_m, st4]
        args += list(init)
    in_specs += [_const_spec(c.shape) for c in consts]
    args += list(consts)
    return pl.pallas_call(
        functools.partial(_mixer_body, L=L, vlo=vlo, vhi=vhi, sample=sample),
        out_shape=(
            jax.ShapeDtypeStruct((zb.shape[0], D_MODEL), bf16),
            jax.ShapeDtypeStruct((n_state, N_HEADS, DK, DV), f32),
            jax.ShapeDtypeStruct((n_state, N_HEADS, DK), f32),
            jax.ShapeDtypeStruct((n_state, 1, N_SMALL), f32),
            jax.ShapeDtypeStruct((n_state, N_HEADS, DK, DV), f32),
        ),
        grid=(nb, nt),
        in_specs=in_specs,
        out_specs=(row_spec(D_MODEL), st4, st_n, st_m, st4),
        scratch_shapes=_mixer_scratch(rows),
        compiler_params=pltpu.CompilerParams(
            dimension_semantics=("arbitrary", "arbitrary"), vmem_limit_bytes=VMEM_LIMIT_BYTES),
        name="mixer_sample" if sample else "mixer_prompt",
    )(*args)


def _fused_call(x2d, ada4, g_norm, w_big, ws_hi, ws_lo, consts, *, nb, nt, L):
    row_spec = lambda n: pl.BlockSpec((L, n), lambda i, j: (i * nt + j, 0))
    st4 = pl.BlockSpec((1, N_HEADS, DK, DV), lambda i, j: (i, 0, 0, 0))
    st_n = pl.BlockSpec((1, N_HEADS, DK), lambda i, j: (i, 0, 0))
    st_m = pl.BlockSpec((1, 1, N_SMALL), lambda i, j: (i, 0, 0))
    return pl.pallas_call(
        functools.partial(_fused_body, L=L),
        out_shape=(
            jax.ShapeDtypeStruct((nb * nt, SUBLANES, PROJ_COLS), f32),
            jax.ShapeDtypeStruct((x2d.shape[0], D_MODEL), bf16),
            jax.ShapeDtypeStruct((nb, N_HEADS, DK, DV), f32),
            jax.ShapeDtypeStruct((nb, N_HEADS, DK), f32),
            jax.ShapeDtypeStruct((nb, 1, N_SMALL), f32),
            jax.ShapeDtypeStruct((nb, N_HEADS, DK, DV), f32),
        ),
        grid=(nb, nt),
        in_specs=[
            row_spec(D_MODEL), _mod_spec(1, 0), _mod_spec(1, 1),
            _const_spec((1, D_MODEL)), _const_spec((D_MODEL, N_BIG)),
            _const_spec((D_MODEL, N_SMALL)), _const_spec((D_MODEL, N_SMALL)),
        ] + [_const_spec(c.shape) for c in consts],
        out_specs=(
            pl.BlockSpec((1, SUBLANES, PROJ_COLS), lambda i, j: (i * nt + j, 0, 0)),
            row_spec(D_MODEL), st4, st_n, st_m, st4,
        ),
        scratch_shapes=([pltpu.VMEM((L, PROJ_COLS), f32)] * (N_BIG // PROJ_COLS)
                        + [pltpu.VMEM((L, N_SMALL), f32)] + _mixer_scratch(L)),
        compiler_params=pltpu.CompilerParams(
            dimension_semantics=("arbitrary", "arbitrary"), vmem_limit_bytes=VMEM_LIMIT_BYTES),
        name="proj_mixer_prompt",
    )(x2d, ada4, ada4, g_norm, w_big, ws_hi, ws_lo, *consts)


def _out_body(u_ref, x_ref, gate1_ref, shift2_ref, scale2_ref, gate2_ref, gn2_ref, gfin_ref,
              wout_ref, wg_ref, wu_ref, wd_ref, y_ref, *, bt, tt):
    x1 = x_ref[...] + _mod_rows(gate1_ref, bt, tt) * _mm(u_ref[...], wout_ref[...])
    h2 = _rms(x1) * gn2_ref[...]
    h2 = (h2 * (1.0 + _mod_rows(scale2_ref, bt, tt)) + _mod_rows(shift2_ref, bt, tt)).astype(bf16)
    acc = jnp.zeros(x1.shape, f32)
    for c in range(0, D_FF, FFN_COLS):
        g = _mm(h2, wg_ref[:, c:c + FFN_COLS])
        up = _mm(h2, wu_ref[:, c:c + FFN_COLS])
        acc = acc + _mm((g * _sigmoid(g) * up).astype(bf16), wd_ref[c:c + FFN_COLS, :])
    x2 = x1 + _mod_rows(gate2_ref, bt, tt) * acc
    y_ref[...] = _rms(x2) * gfin_ref[...]


def _out_call(u2d, x2d, ada4, g_norm2, g_final, w_out, w_gate, w_up, w_down, *, nb, nt, bt, tt):
    rows = bt * tt
    row_spec = pl.BlockSpec((rows, D_MODEL), lambda i, j: (i * nt + j, 0))
    return pl.pallas_call(
        functools.partial(_out_body, bt=bt, tt=tt),
        out_shape=jax.ShapeDtypeStruct(x2d.shape, f32),
        grid=(nb, nt),
        in_specs=[
            row_spec, row_spec, _mod_spec(bt, 2), _mod_spec(bt, 3), _mod_spec(bt, 4), _mod_spec(bt, 5),
            _const_spec((1, D_MODEL)), _const_spec((1, D_MODEL)),
            _const_spec((D_MODEL, D_MODEL)), _const_spec((D_MODEL, D_FF)),
            _const_spec((D_MODEL, D_FF)), _const_spec((D_FF, D_MODEL)),
        ],
        out_specs=row_spec,
        compiler_params=pltpu.CompilerParams(
            dimension_semantics=("arbitrary", "arbitrary"), vmem_limit_bytes=VMEM_LIMIT_BYTES),
        name="out_ffn",
    )(u2d, x2d, ada4, ada4, ada4, ada4, g_norm2, g_final, w_out, w_gate, w_up, w_down)


PROMPT_CHUNK = 256
PROMPT_FFN_ROWS = 512
SAMPLE_L = SUBLANES
SAMPLE_MIX_GROUPS = 8
SAMPLE_DENSE_GROUPS = 32


W_IN_SEG_A = (0, 3072)
W_IN_GATES = 3072
W_IN_SEG_B = (3080, 6152)
W_IN_LR = 6152
W_IN_SEG_C = (6168, 8216)
LANES = 128


def _prep_body(w_ref, wb_ref, wh_ref, wl_ref):
    a0, a1 = W_IN_SEG_A
    b0, b1 = W_IN_SEG_B
    c0, c1 = W_IN_SEG_C
    wb_ref[:, 0:a1 - a0] = w_ref[:, a0:a1].astype(bf16)
    wb_ref[:, OFF_QB:OFF_QB + b1 - b0] = w_ref[:, b0:b1].astype(bf16)
    wb_ref[:, OFF_GA:OFF_GA + c1 - c0] = w_ref[:, c0:c1].astype(bf16)
    lane = lax.broadcasted_iota(jnp.int32, (1, LANES), 1)
    gates_tile = w_ref[:, W_IN_GATES:W_IN_GATES + LANES]
    lr_tile = w_ref[:, W_IN_LR - LANE_LR:W_IN_LR - LANE_LR + LANES]
    small = jnp.where(lane < LANE_LR, gates_tile, jnp.where(lane < LANE_LR + GLA_RANK, lr_tile, 0.0))
    hi = small.astype(bf16)
    wh_ref[...] = hi
    wl_ref[...] = (small - hi.astype(f32)).astype(bf16)


def _prep_call(w_in):
    assert W_IN_GATES % LANES == 0 and (W_IN_LR - LANE_LR) % LANES == 0
    n_in = w_in.shape[1]
    rows = LANES
    return pl.pallas_call(
        _prep_body,
        out_shape=(
            jax.ShapeDtypeStruct((D_MODEL, N_BIG), bf16),
            jax.ShapeDtypeStruct((D_MODEL, N_SMALL), bf16),
            jax.ShapeDtypeStruct((D_MODEL, N_SMALL), bf16),
        ),
        grid=(D_MODEL // rows,),
        in_specs=[pl.BlockSpec((rows, n_in), lambda i: (i, 0))],
        out_specs=(
            pl.BlockSpec((rows, N_BIG), lambda i: (i, 0)),
            pl.BlockSpec((rows, N_SMALL), lambda i: (i, 0)),
            pl.BlockSpec((rows, N_SMALL), lambda i: (i, 0)),
        ),
        compiler_params=pltpu.CompilerParams(
            dimension_semantics=("arbitrary",), vmem_limit_bytes=VMEM_LIMIT_BYTES),
        name="w_in_prep",
    )(w_in)


def _prep_layer_weights(w_in, w_alpha, b_igate, b_fgate):
    w_big, ws_hi, ws_lo = _prep_call(w_in)
    w_al = jnp.zeros((N_SMALL, QK), f32).at[LANE_LR:LANE_LR + GLA_RANK].set(w_alpha).astype(bf16)
    g_bias = jnp.concatenate([b_igate, b_fgate, jnp.zeros((N_SMALL - 2 * N_HEADS,), f32)])[None, :]
    return w_big, ws_hi, ws_lo, w_al, g_bias


def _layer(xp2d, xs2d, st_s, ada_p, ada_s, p, *, bp, tp, bs):
    (g_norm1, g_norm2, w_in, conv_w, conv_b, b_igate, b_fgate, g_head_a,
     w_alpha, b_alpha, g_head_b, w_out, w_ffn_gate, w_ffn_up, w_ffn_down, g_final_or_ones) = p
    w_big, ws_hi, ws_lo, w_al, g_bias = _prep_layer_weights(w_in, w_alpha, b_igate, b_fgate)
    consts = (conv_w, conv_b[None, :], g_bias, w_al, b_alpha[None, :],
              g_head_a.reshape(1, D_MODEL), g_head_b.reshape(1, D_MODEL))
    w_out16, w_g16, w_u16, w_d16 = (w.astype(bf16) for w in (w_out, w_ffn_gate, w_ffn_up, w_ffn_down))
    gn1, gn2, gfin = g_norm1[None, :], g_norm2[None, :], g_final_or_ones[None, :]

    ntp = tp // PROMPT_CHUNK
    tail, u, c_p, n_p, m_p, s_p = _fused_call(xp2d, ada_p, gn1, w_big, ws_hi, ws_lo, consts,
                                              nb=bp, nt=ntp, L=PROMPT_CHUNK)
    conv_p = tail.reshape(bp, ntp, SUBLANES, 2 * QK)[:, ntp - 1, SUBLANES - (CONV_W - 1):, :]
    yp = _out_call(u, xp2d, ada_p, gn2, gfin, w_out16, w_g16, w_u16, w_d16,
                   nb=bp, nt=tp // PROMPT_FFN_ROWS, bt=1, tt=PROMPT_FFN_ROWS)

    conv0, c0, n0, m0, s0 = st_s
    nbd = bs // SAMPLE_DENSE_GROUPS
    zb_s, zs_s, _ = _inproj_call(xs2d, ada_s, gn1, w_big, ws_hi, ws_lo,
                                 nb=nbd, nt=1, bt=SAMPLE_DENSE_GROUPS, tt=SAMPLE_L, zdtype=f32)
    lo = CONV_W - 1
    n_tok = SAMPLE_L - lo - 1
    conv_s = zb_s[:, OFF_QKA:OFF_QKA + 2 * QK].reshape(bs, SAMPLE_L, 2 * QK)[:, n_tok:lo + n_tok, :]
    conv0_rows = jnp.pad(conv0, ((0, 0), (0, SAMPLE_L - lo), (0, 0))).reshape(bs * SAMPLE_L, 2 * QK)
    m0_lanes = jnp.pad(m0[:, None, :], ((0, 0), (0, 0), (0, N_SMALL - N_HEADS)))
    u_s, c_s, n_s, m_s, s_s = _mixer_call(
        zb_s, zs_s, (conv0_rows, c0, n0, m0_lanes, s0), consts,
        nb=bs // SAMPLE_MIX_GROUPS, nt=1, bt=SAMPLE_MIX_GROUPS, L=SAMPLE_L, vlo=lo, vhi=lo + n_tok)
    ys = _out_call(u_s, xs2d, ada_s, gn2, gfin, w_out16, w_g16, w_u16, w_d16,
                   nb=nbd, nt=1, bt=SAMPLE_DENSE_GROUPS, tt=SAMPLE_L)
    st_p = (conv_p, c_p, n_p, m_p[:, 0, :N_HEADS], s_p)
    st_s_new = (conv_s, c_s, n_s, m_s[:, 0, :N_HEADS], s_s)
    return yp, ys, st_p, st_s_new


def kernel(x_prompt, x_sample, state_conv, state_C, state_n, state_m, state_S, c_prompt, c_sample,
           g_norm1, g_norm2, w_ada, b_ada, w_in, conv_w, conv_b, b_igate, b_fgate, g_head_a,
           w_alpha, b_alpha, g_head_b, w_out, w_ffn_gate, w_ffn_up, w_ffn_down, g_final):
    bp, tp, _ = x_prompt.shape
    bs, ts, _ = x_sample.shape
    depth = w_in.shape[0]
    assert depth == 1, "the final norm is fused into the (single) layer's FFN kernel"
    lo = CONV_W - 1
    assert ts == SAMPLE_L - lo - 1
    xp2d = x_prompt.reshape(bp * tp, D_MODEL)
    xs2d = jnp.pad(x_sample, ((0, 0), (lo, SAMPLE_L - lo - ts), (0, 0))).reshape(bs * SAMPLE_L, D_MODEL)
    c_all = jnp.concatenate([c_prompt, c_sample], axis=0)
    new_p = [[] for _ in range(5)]
    new_s = [[] for _ in range(5)]
    for l in range(depth):
        ada = _ada_call(c_all, w_ada[l], b_ada[l][None, :]).reshape(bp + bs, N_ADA, 1, D_MODEL)
        p = (g_norm1[l], g_norm2[l], w_in[l], conv_w[l], conv_b[l], b_igate[l], b_fgate[l], g_head_a[l],
             w_alpha[l], b_alpha[l], g_head_b[l], w_out[l], w_ffn_gate[l], w_ffn_up[l], w_ffn_down[l], g_final)
        st_s = (state_conv[l], state_C[l], state_n[l], state_m[l], state_S[l])
        xp2d, xs2d, st_p, st_s_new = _layer(xp2d, xs2d, st_s, ada[:bp], ada[bp:], p, bp=bp, tp=tp, bs=bs)
        for j in range(5):
            new_p[j].append(st_p[j])
            new_s[j].append(st_s_new[j])
    y_prompt = xp2d.reshape(bp, tp, D_MODEL)
    y_sample = xs2d.reshape(bs, SAMPLE_L, D_MODEL)[:, lo:lo + ts, :]
    outs_p = [jnp.stack(a, axis=0) for a in new_p]
    outs_s = [jnp.stack(a, axis=0) for a in new_s]
    return (y_prompt, y_sample, *outs_p, *outs_s)
```

```python
import functools

import jax
import jax.numpy as jnp
from jax import lax
from jax.experimental import pallas as pl
from jax.experimental.pallas import tpu as pltpu

f32 = jnp.float32
bf16 = jnp.bfloat16

D_MODEL = 1024
N_HEADS = 4
DK = 128
DV = 256
QK = N_HEADS * DK
CONV_W = 4
GLA_RANK = 16
GLA_TAU = 16.0
D_FF = 2816
EPS = 1e-6
N_ADA = 6

N_BIG = 8192
N_SMALL = 128
OFF_QKA, OFF_VA, OFF_OA, OFF_QB, OFF_KB, OFF_VB, OFF_RB, OFF_GA, OFF_GB = (
    0, 1024, 2048, 3072, 3584, 4096, 5120, 6144, 7168)
LANE_IG, LANE_FG, LANE_LR = 0, N_HEADS, 2 * N_HEADS

SUBLANES = 8
NEG = -1e30
SAFE_LOG_RANGE = 60.0
V7X_VMEM_BYTES = 64 * 1024 * 1024
VMEM_LIMIT_BYTES = V7X_VMEM_BYTES - 8 * 1024 * 1024
FFN_COLS = 256
PROJ_COLS = 1024
DVA = DV + DK


def _mm(a, b):
    return jnp.dot(a, b, preferred_element_type=f32)


def _mm_nt(a, b):
    return lax.dot_general(a, b, (((1,), (1,)), ((), ())), preferred_element_type=f32)


def _mm_tn(a, b):
    return lax.dot_general(a, b, (((0,), (0,)), ((), ())), preferred_element_type=f32)


def _split3(x):
    hi = x.astype(bf16)
    r = x - hi.astype(f32)
    mid = r.astype(bf16)
    lo = (r - mid.astype(f32)).astype(bf16)
    return hi, mid, lo


def _sigmoid(x):
    return 0.5 * jnp.tanh(0.5 * x) + 0.5


def _log_sigmoid(x):
    return jnp.minimum(x, 0.0) - jnp.log(1.0 + jnp.exp(-jnp.abs(x)))


def _rms(x):
    return x * lax.rsqrt(jnp.mean(x * x, axis=-1, keepdims=True) + EPS)


def _row_to_col(row, eye):
    return jnp.sum(jnp.where(eye, row, 0.0), axis=1, keepdims=True)


def _diag_row(mat, eye):
    return jnp.sum(jnp.where(eye, mat, 0.0), axis=0, keepdims=True)


def _mod_rows(mod_ref, bt, tt):
    if bt == 1:
        return mod_ref[0, 0]
    v = mod_ref[...].reshape(bt, 1, D_MODEL)
    return jnp.broadcast_to(v, (bt, tt, D_MODEL)).reshape(bt * tt, D_MODEL)


def _ada_body(c_ref, w_ref, b_ref, o_ref):
    ch, cl, _ = _split3(c_ref[...])
    wh, wl, _ = _split3(w_ref[...])
    o_ref[:, 0, 0, :] = _mm(ch, wh) + (_mm(cl, wh) + _mm(ch, wl)) + b_ref[...]


def _ada_call(c_all, w_ada, b_ada):
    m = c_all.shape[0]
    n = w_ada.shape[1]
    return pl.pallas_call(
        _ada_body,
        out_shape=jax.ShapeDtypeStruct((m, n // D_MODEL, 1, D_MODEL), f32),
        grid=(n // D_MODEL,),
        in_specs=[
            pl.BlockSpec((m, D_MODEL), lambda i: (0, 0)),
            pl.BlockSpec((D_MODEL, D_MODEL), lambda i: (0, i)),
            pl.BlockSpec((1, D_MODEL), lambda i: (0, i)),
        ],
        out_specs=pl.BlockSpec((m, 1, 1, D_MODEL), lambda i: (0, i, 0, 0)),
        compiler_params=pltpu.CompilerParams(
            dimension_semantics=("arbitrary",), vmem_limit_bytes=VMEM_LIMIT_BYTES),
        name="ada_proj",
    )(c_all, w_ada, b_ada)


def _inproj_parts(x_ref, shift_ref, scale_ref, g_ref, wbig_ref, wsh_ref, wsl_ref,
                  zb_ref, zs_ref, tail_ref, *, bt, tt):
    rows = bt * tt
    h = _rms(x_ref[...]) * g_ref[...]
    h = h * (1.0 + _mod_rows(scale_ref, bt, tt)) + _mod_rows(shift_ref, bt, tt)
    hh = h.astype(bf16)
    hl = (h - hh.astype(f32)).astype(bf16)

    def narrow():
        wsh = wsh_ref[...]
        zs_ref[...] = _mm(hh, wsh) + (_mm(hl, wsh) + _mm(hh, wsl_ref[...]))

    def wide(idx):
        c = idx * PROJ_COLS
        z = _mm(hh, wbig_ref[:, c:c + PROJ_COLS])
        if isinstance(zb_ref, (list, tuple)):
            zb_ref[idx][...] = z
        else:
            zb_ref[:, c:c + PROJ_COLS] = z.astype(zb_ref.dtype)
        if c == OFF_QKA:
            tail_ref[0] = z[rows - SUBLANES:rows, :]

    return narrow, wide


def _inproj_body(*refs, bt, tt):
    narrow, wide = _inproj_parts(*refs, bt=bt, tt=tt)
    narrow()
    for idx in range(N_BIG // PROJ_COLS):
        wide(idx)


def _const_spec(shape):
    nd = len(shape)
    return pl.BlockSpec(shape, lambda i, j: (0,) * nd, pipeline_mode=pl.Buffered(1))


def _mod_spec(bt, k, row0):
    assert row0 % bt == 0
    return pl.BlockSpec((bt, 1, 1, D_MODEL), lambda i, j: (row0 // bt + i, k, 0, 0))


def _inproj_call(x2d, ada4, g_norm, w_big, ws_hi, ws_lo, *, nb, nt, bt, tt, zdtype):
    ada, row0 = ada4
    rows = bt * tt
    total = x2d.shape[0]
    row_spec = lambda n: pl.BlockSpec((rows, n), lambda i, j: (i * nt + j, 0))
    return pl.pallas_call(
        functools.partial(_inproj_body, bt=bt, tt=tt),
        out_shape=(
            jax.ShapeDtypeStruct((total, N_BIG), zdtype),
            jax.ShapeDtypeStruct((total, N_SMALL), f32),
            jax.ShapeDtypeStruct((nb * nt, SUBLANES, PROJ_COLS), f32),
        ),
        grid=(nb, nt),
        in_specs=[
            row_spec(D_MODEL), _mod_spec(bt, 0, row0), _mod_spec(bt, 1, row0),
            _const_spec((1, D_MODEL)), _const_spec((D_MODEL, N_BIG)),
            _const_spec((D_MODEL, N_SMALL)), _const_spec((D_MODEL, N_SMALL)),
        ],
        out_specs=(
            row_spec(N_BIG), row_spec(N_SMALL),
            pl.BlockSpec((1, SUBLANES, PROJ_COLS), lambda i, j: (i * nt + j, 0, 0)),
        ),
        compiler_params=pltpu.CompilerParams(
            dimension_semantics=("arbitrary", "arbitrary"), vmem_limit_bytes=VMEM_LIMIT_BYTES),
        name="in_proj",
    )(x2d, ada, ada, g_norm, w_big, ws_hi, ws_lo)


N_MIXER_CONSTS = 7
N_MIXER_INIT = 5
N_MIXER_OUTS = 5


def _mixer_scratch(rows):
    return [pltpu.VMEM((rows, QK), f32)] * 5 + [
        pltpu.VMEM((rows, N_SMALL), f32), pltpu.VMEM((rows, N_SMALL), f32), pltpu.VMEM((rows, QK), f32),
    ] + [pltpu.VMEM((rows, D_MODEL), f32)] * 4 + [
        pltpu.VMEM((SUBLANES, 2 * QK), f32), pltpu.VMEM((N_HEADS, DK, DVA), f32),
    ]


def _mixer_body(*refs, L, vlo, vhi, sample):
    zb_ref, zs_ref = refs[:2]
    k = 2
    init = None
    if sample:
        init = refs[k:k + N_MIXER_INIT]
        k += N_MIXER_INIT
    consts = refs[k:k + N_MIXER_CONSTS]
    k += N_MIXER_CONSTS
    outs = refs[k:k + N_MIXER_OUTS]
    _mixer_compute(zb_ref, zs_ref, init, consts, outs, refs[k + N_MIXER_OUTS:], L=L, vlo=vlo, vhi=vhi)


def _fused_body(x_ref, shift_ref, scale_ref, g_ref, wbig_ref, wsh_ref, wsl_ref, *refs, L):
    consts = refs[:N_MIXER_CONSTS]
    tail_ref = refs[N_MIXER_CONSTS]
    outs = refs[N_MIXER_CONSTS + 1:N_MIXER_CONSTS + 1 + N_MIXER_OUTS]
    k = N_MIXER_CONSTS + 1 + N_MIXER_OUTS
    zb_s = list(refs[k:k + N_BIG // PROJ_COLS])
    zs_s = refs[k + N_BIG // PROJ_COLS]
    scratch = refs[k + N_BIG // PROJ_COLS + 1:]
    _zero_carried_state(outs, scratch)
    narrow, wide = _inproj_parts(x_ref, shift_ref, scale_ref, g_ref, wbig_ref, wsh_ref, wsl_ref,
                                 zb_s, zs_s, tail_ref, bt=1, tt=L)

    plan = {
        "start": (OFF_QKA, OFF_QB),
        "gate_cumsum": (OFF_GA, OFF_GB),
        "gla_cumsum": (OFF_OA, OFF_RB),
        "gate_products": (OFF_VA, OFF_VB),
    }

    def produce(stage):
        if stage == "start":
            narrow()
        for off in plan[stage]:
            wide(off // PROJ_COLS)

    _mixer_compute(zb_s, zs_s, None, consts, outs, scratch, L=L, vlo=0, vhi=L, zero_init=False,
                   produce=produce)


def _zero_carried_state(outs, scratch):
    _, _, _, mo_ref, So_ref = outs
    carry_s, caug_s = scratch[-2:]

    @pl.when(pl.program_id(1) == 0)
    def _zero():
        caug_s[...] = jnp.zeros(caug_s.shape, f32)
        mo_ref[...] = jnp.zeros(mo_ref.shape, f32)
        So_ref[...] = jnp.zeros(So_ref.shape, f32)
        carry_s[...] = jnp.zeros(carry_s.shape, f32)


def _mixer_compute(zb_ref, zs_ref, init, consts, outs, scratch, *, L, vlo, vhi, zero_init=True,
                   produce=lambda stage: None):
    sample = init is not None
    if sample:
        conv0_ref, C0_ref, n0_ref, m0_ref, S0_ref = init
    cw_ref, cb_ref, gb_ref, wal_ref, bal_ref, gha_ref, ghb_ref = consts
    u_ref, Co_ref, no_ref, mo_ref, So_ref = outs
    qa_s, ka_s, qh_s, kh_s, kb_s, x_s, r_s, bg_s, ha_s, hb_s, pa_s, pb_s, carry_s, caug_s = scratch
    rows = zs_ref.shape[0]
    every = slice(None)

    def zcols(rsel, off, width):
        if isinstance(zb_ref, (list, tuple)):
            grp, o = divmod(off, PROJ_COLS)
            assert o + width <= PROJ_COLS
            return zb_ref[grp][rsel, o:o + width]
        return zb_ref[rsel, off:off + width]
    n_groups = rows // L
    log2_l = L.bit_length() - 1
    masked = not (vlo == 0 and vhi == L)
    scale = DK ** -0.5

    if not sample and zero_init:
        _zero_carried_state(outs, scratch)

    row_in = lax.broadcasted_iota(jnp.int32, (rows, 1), 0) & (L - 1)
    valid = (row_in >= vlo) & (row_in < vhi)

    produce("start")
    zq =zcols(every, OFF_QKA, 2 * QK).astype(f32)
    if sample:
        zq = jnp.where(row_in < CONV_W - 1, conv0_ref[...], zq)
        carry = jnp.zeros((SUBLANES, 2 * QK), f32)
    else:
        carry = carry_s[...]
        carry_s[...] = zq[rows - SUBLANES:rows, :]
    xe = jnp.concatenate([carry, zq], axis=0)
    cw = cw_ref[...]
    conv = cb_ref[...] + zq * cw[CONV_W - 1:CONV_W]
    for k in range(CONV_W - 1):
        o = SUBLANES - (CONV_W - 1) + k
        conv = conv + xe[o:o + rows] * cw[k:k + 1]
    qk = conv * _sigmoid(conv)
    qa_s[...] = qk[:, :QK] * scale
    ka_s[...] = qk[:, QK:]

    lane = lax.broadcasted_iota(jnp.int32, (1, N_SMALL), 1)
    zs = zs_ref[...]
    pre = zs + gb_ref[...]
    gate = jnp.where(lane < LANE_FG, pre, _log_sigmoid(pre))
    gate_f = jnp.where(valid, gate, 0.0) if masked else gate
    rr = lax.broadcasted_iota(jnp.int32, (rows, rows), 0)
    cc = lax.broadcasted_iota(jnp.int32, (rows, rows), 1)
    tril_mask = ((rr >> log2_l) == (cc >> log2_l)) & (cc <= rr)
    tril = jnp.where(tril_mask, 1.0, 0.0).astype(bf16)

    def cumsum_groups(v):
        hi, mid, lo = _split3(v)
        return _mm(tril, hi) + (_mm(tril, mid) + _mm(tril, lo))

    produce("gate_cumsum")
    x_all = jnp.where(lane < LANE_FG, gate, cumsum_groups(gate_f))
    x_s[...] = x_all
    b_al = pltpu.roll(x_all, N_SMALL - N_HEADS, axis=1)
    r_all = x_all - b_al
    r_ok = (lane < N_HEADS) & valid if masked else jnp.broadcast_to(lane < N_HEADS, (rows, N_SMALL))
    r_s[...] = jnp.where(r_ok, r_all, NEG)
    range_m = jnp.max(jnp.where(r_ok, r_all, NEG)) - jnp.min(jnp.where(r_ok, r_all, -NEG))

    la = _log_sigmoid(_mm(zs.astype(bf16), wal_ref[...]) + bal_ref[...]) * (1.0 / GLA_TAU)
    kb = zcols(every, OFF_KB, QK).astype(f32)
    if masked:
        la = jnp.where(valid, la, 0.0)
        kb = jnp.where(valid, kb, 0.0)
    produce("gla_cumsum")
    bg = cumsum_groups(la)
    bg_s[...] = bg
    range_g = jnp.max(-bg)
    qh_s[...] = zcols(every, OFF_QB, QK).astype(f32) * scale * jnp.exp(bg)
    kh_s[...] = kb * jnp.exp(-bg)
    fast_ok = jnp.logical_and(range_m <= SAFE_LOG_RANGE, range_g <= SAFE_LOG_RANGE)

    produce("gate_products")
    col = lambda off: zcols(every, off, D_MODEL).astype(f32)
    pa_s[...] = _sigmoid(col(OFF_GA)) * _sigmoid(col(OFF_OA)) * gha_ref[...]
    rb = col(OFF_RB)
    pb_s[...] = _sigmoid(col(OFF_GB)) * (rb * _sigmoid(rb)) * ghb_ref[...]

    tri = lax.broadcasted_iota(jnp.int32, (L, L), 0) >= lax.broadcasted_iota(jnp.int32, (L, L), 1)
    eye = lax.broadcasted_iota(jnp.int32, (DK, DK), 0) == lax.broadcasted_iota(jnp.int32, (DK, DK), 1)
    lane_h = lane
    lrow = lax.broadcasted_iota(jnp.int32, (1, L), 1)
    lcol = lax.broadcasted_iota(jnp.int32, (L, 1), 0)
    vrow = (lrow >= vlo) & (lrow < vhi)
    ones16 = jnp.ones((L, DK), bf16)

    def load_state(g, h):
        if sample:
            n_mat = jnp.broadcast_to(_row_to_col(n0_ref[g, pl.ds(h, 1), :], eye), (DK, DK))
            return jnp.concatenate([C0_ref[g, h], n_mat], axis=1), S0_ref[g, h]
        return caug_s[h], So_ref[0, h]

    def store_state(g, h, caug, s_new):
        sidx = g if sample else 0
        if not sample:
            caug_s[h] = caug
        Co_ref[sidx, h] = caug[:, :DV]
        no_ref[sidx, pl.ds(h, 1), :] = _diag_row(caug[:, DV:], eye)
        So_ref[sidx, h] = s_new

    def group(g, exact):
        static = isinstance(g, int)
        r0 = g * L if static else pl.multiple_of(g * L, L)
        tail0 = r0 + L - SUBLANES if static else pl.multiple_of(r0 + L - SUBLANES, SUBLANES)
        rs = pl.ds(r0, L)
        xc = x_s[rs, :]
        rc = r_s[rs, :]
        if exact:
            sr = lax.broadcasted_iota(jnp.int32, (2 * SUBLANES, N_SMALL), 0)
            sc = lax.broadcasted_iota(jnp.int32, (2 * SUBLANES, N_SMALL), 1)
            sel = jnp.where((sr < N_HEADS) & (sc == sr + LANE_IG), 1.0,
                            jnp.where((sr < N_HEADS) & (sc == sr + LANE_FG), -1.0, 0.0)).astype(bf16)
            xh, xm, xl = _split3(xc)
            r_rows = _mm_nt(sel, xh) + (_mm_nt(sel, xm) + _mm_nt(sel, xl))
        m_prev = m0_ref[g] if sample else mo_ref[0]
        m_vec = m_prev
        bg_tail = bg_s[pl.ds(tail0, SUBLANES), :]
        for h in range(N_HEADS):
            ks = slice(h * DK, (h + 1) * DK)
            vs = slice(h * DV, (h + 1) * DV)
            caug, s_st = load_state(g, h)
            m_st = m_prev[:, h:h + 1]
            b_col = xc[:, LANE_FG + h:LANE_FG + h + 1]
            r_col = rc[:, h:h + 1]
            b_last = b_col[L - 1:L, :]
            v16 = zcols(rs, OFF_VA + h * DV, DV).astype(bf16)
            vb16 = zcols(rs, OFF_VB + h * DV, DV).astype(bf16)
            eb_last = jnp.exp(bg_tail[SUBLANES - 1:SUBLANES, ks])
            eb_col = _row_to_col(eb_last, eye)
            if not exact:
                c0 = jnp.max(r_col, axis=0, keepdims=True)
                mm = jnp.maximum(c0, m_st)
                q16 = qa_s[rs, ks].astype(bf16)
                kt16 = (ka_s[rs, ks] * jnp.exp(r_col - c0)).astype(bf16)
                va = jnp.concatenate([v16, ones16], axis=1)
                s16 = jnp.where(tri, _mm_nt(q16, kt16), 0.0).astype(bf16)
                comb = jnp.exp(c0 - mm) * _mm(s16, va) + jnp.exp(m_st - mm) * _mm(q16, caug.astype(bf16))
                inv = 1.0 / jnp.maximum(jnp.abs(comb[:, DV:]), jnp.exp(-(b_col + mm)))
                ha_s[rs, h * DV:h * DV + DK] = comb[:, :DK] * inv
                ha_s[rs, h * DV + DK:(h + 1) * DV] = comb[:, DK:DV] * inv
                m_new = jnp.maximum(b_last + m_st, b_last + c0)
                caug_new = (jnp.exp(b_last + m_st - m_new) * caug
                            + jnp.exp(b_last + c0 - m_new) * _mm_tn(kt16, va))
                qh16 = qh_s[rs, ks].astype(bf16)
                kh = kh_s[rs, ks]
                a16 = jnp.where(tri, _mm_nt(qh16, kh.astype(bf16)), 0.0).astype(bf16)
                hb_s[rs, vs] = _mm(a16, vb16) + _mm(qh16, s_st.astype(bf16))
                s_new = eb_col * s_st + _mm_tn((kh * eb_last).astype(bf16), vb16)
            else:
                c_st = caug[:, :DV]
                n_row = _diag_row(caug[:, DV:], eye)
                r_row = r_rows[h:h + 1, :]
                if masked:
                    r_row = jnp.where(vrow, r_row, NEG)
                dm = jnp.where(tri, b_col + r_row, NEG)
                m_t = jnp.maximum(jnp.max(dm, axis=1, keepdims=True), b_col + m_st)
                w = jnp.exp(dm - m_t)
                wi = jnp.exp(b_col + m_st - m_t)
                q = qa_s[rs, ks]
                k = ka_s[rs, ks]
                q16 = q.astype(bf16)
                s = _mm_nt(q16, k.astype(bf16)) * w
                num = _mm(s.astype(bf16), v16) + wi * _mm(q16, c_st.astype(bf16))
                den = jnp.sum(s, axis=1, keepdims=True) + wi * jnp.sum(q * n_row, axis=1, keepdims=True)
                ha_s[rs, vs] = num / jnp.maximum(jnp.abs(den), jnp.exp(-m_t))
                m_new = jnp.maximum(b_last + m_st, b_last + jnp.max(r_row, axis=1, keepdims=True))
                wc = jnp.exp(b_last + m_st - m_new)
                kt = k * jnp.exp(b_last + r_col - m_new)
                n_new = wc * n_row + jnp.sum(kt, axis=0, keepdims=True)
                caug_new = jnp.concatenate(
                    [wc * c_st + _mm_tn(kt.astype(bf16), v16),
                     jnp.broadcast_to(_row_to_col(n_new, eye), (DK, DK))], axis=1)
                qb = zcols(rs, OFF_QB + h * DK, DK).astype(f32) * scale
                bgc = bg_s[rs, ks]
                sub = lax.broadcasted_iota(jnp.int32, (SUBLANES, 1), 0)

                def score_col(s_i, a, ks=ks, qb=qb, bgc=bgc, sub=sub):
                    s8 = lax.shift_left(lax.shift_right_logical(s_i, 3), 3)
                    base = pl.multiple_of(r0 + s8, SUBLANES)
                    pick = sub == (s_i - s8)
                    b_s = jnp.sum(jnp.where(pick, bg_s[pl.ds(base, SUBLANES), ks], 0.0), axis=0, keepdims=True)
                    k_s = jnp.sum(jnp.where(pick, kb_s[pl.ds(base, SUBLANES), ks], 0.0), axis=0, keepdims=True)
                    e = jnp.exp(jnp.where(lcol >= s_i, bgc - b_s, NEG))
                    col = jnp.sum(qb * e * k_s, axis=1, keepdims=True)
                    return jnp.where(lrow == s_i, col, a)

                a = lax.fori_loop(0, L, score_col, jnp.zeros((L, L), f32))
                hb_s[rs, vs] = (_mm(a.astype(bf16), vb16)
                                + _mm((qb * jnp.exp(bgc)).astype(bf16), s_st.astype(bf16)))
                kt_b = kb_s[rs, ks] * jnp.exp(bg_tail[SUBLANES - 1:SUBLANES, ks] - bgc)
                s_new = eb_col * s_st + _mm_tn(kt_b.astype(bf16), vb16)
            store_state(g, h, caug_new, s_new)
            m_vec = jnp.where(lane_h == h, m_new, m_vec)
        mo_ref[g if sample else 0] = m_vec

    def groups_fast_batched():
        r_all_m = r_s[...]
        c0_l, m_l, mm_l, fac, wc = [], [], [], [], []
        for g in range(n_groups):
            gs = slice(g * L, (g + 1) * L)
            c0_g = jnp.max(r_all_m[gs], axis=0, keepdims=True)
            m_g = m0_ref[g]
            b_last = b_al[gs][L - 1:L, :]
            m_new = jnp.maximum(b_last + m_g, b_last + c0_g)
            mo_ref[g] = m_new
            c0_l.append(c0_g)
            m_l.append(m_g)
            mm_l.append(jnp.maximum(c0_g, m_g))
            fac.append(jnp.exp(b_last + c0_g - m_new))
            wc.append(jnp.exp(b_last + m_g - m_new))
        rows_of = lambda lst: jnp.concatenate([jnp.broadcast_to(v, (L, N_SMALL)) for v in lst], axis=0)
        c0_r, m_r, mm_r = rows_of(c0_l), rows_of(m_l), rows_of(mm_l)
        beta = jnp.exp(r_all_m - c0_r)
        gi_r = jnp.exp(c0_r - mm_r)
        wi_r = jnp.exp(m_r - mm_r)
        floor = jnp.exp(-(b_al + mm_r))
        ones_g = jnp.ones((L, DK), bf16)
        for h in range(N_HEADS):
            ks = slice(h * DK, (h + 1) * DK)
            vs = slice(h * DV, (h + 1) * DV)
            kt = ka_s[:, ks] * beta[:, h:h + 1]
            v = zcols(every, OFF_VA + h * DV, DV)
            vb = zcols(every, OFF_VB + h * DV, DV)
            q16 = qa_s[:, ks].astype(bf16)
            va = jnp.concatenate([v.astype(bf16), jnp.ones((rows, DK), bf16)], axis=1)
            s16 = jnp.where(tril_mask, _mm_nt(q16, kt.astype(bf16)), 0.0).astype(bf16)
            ni = _mm(s16, va)
            qh16 = qh_s[:, ks].astype(bf16)
            kh = kh_s[:, ks]
            a16 = jnp.where(tril_mask, _mm_nt(qh16, kh.astype(bf16)), 0.0).astype(bf16)
            o_intra = _mm(a16, vb.astype(bf16))
            ci_parts, oi_parts = [], []
            for g in range(n_groups):
                gs = slice(g * L, (g + 1) * L)
                caug, s_st = load_state(g, h)
                va_g = jnp.concatenate([v[gs].astype(bf16), ones_g], axis=1)
                ci_parts.append(_mm(qa_s[gs, ks].astype(bf16), caug.astype(bf16)))
                caug_new = wc[g][:, h:h + 1] * caug + fac[g][:, h:h + 1] * _mm_tn(kt[gs].astype(bf16), va_g)
                oi_parts.append(_mm(qh_s[gs, ks].astype(bf16), s_st.astype(bf16)))
                eb_last = jnp.exp(bg_s[gs, ks][L - 1:L, :])
                s_new = (_row_to_col(eb_last, eye) * s_st
                         + _mm_tn((kh[gs] * eb_last).astype(bf16), vb[gs].astype(bf16)))
                store_state(g, h, caug_new, s_new)
            comb = gi_r[:, h:h + 1] * ni + wi_r[:, h:h + 1] * jnp.concatenate(ci_parts, axis=0)
            inv = 1.0 / jnp.maximum(jnp.abs(comb[:, DV:]), floor[:, h:h + 1])
            ha_s[:, h * DV:h * DV + DK] = comb[:, :DK] * inv
            ha_s[:, h * DV + DK:(h + 1) * DV] = comb[:, DK:DV] * inv
            hb_s[:, vs] = o_intra + jnp.concatenate(oi_parts, axis=0)

    def run_groups(exact):
        if sample and not exact and L == SUBLANES:
            groups_fast_batched()
        elif exact and n_groups > 1:
            def body(g, carry_):
                group(g, exact)
                return carry_
            lax.fori_loop(0, n_groups, body, 0)
        else:
            for g in range(n_groups):
                group(g, exact)

    @pl.when(fast_ok)
    def _fast():
        run_groups(False)

    @pl.when(jnp.logical_not(fast_ok))
    def _exact():
        kb_s[...] = kb
        run_groups(True)

    for h in range(N_HEADS):
        vs = slice(h * DV, (h + 1) * DV)
        u = pa_s[:, vs] * _rms(ha_s[:, vs]) + pb_s[:, vs] * _rms(hb_s[:, vs])
        u_ref[:, vs] = u.astype(u_ref.dtype)


def _mixer_call(zb, zs, init, consts, *, nb, nt, bt, L, vlo, vhi):
    sample = init is not None
    rows = bt * L
    n_state = nb * bt
    row_spec = lambda n: pl.BlockSpec((rows, n), lambda i, j: (i * nt + j, 0))
    st4 = pl.BlockSpec((bt, N_HEADS, DK, DV), lambda i, j: (i, 0, 0, 0))
    st_n = pl.BlockSpec((bt, N_HEADS, DK), lambda i, j: (i, 0, 0))
    st_m = pl.BlockSpec((bt, 1, N_SMALL), lambda i, j: (i, 0, 0))
    in_specs = [row_spec(N_BIG), row_spec(N_SMALL)]
    args = [zb, zs]
    if sample:
        in_specs += [row_spec(2 * QK), st4, st_n, st_m, st4]
        args += list(init)
    in_specs += [_const_spec(c.shape) for c in consts]
    args += list(consts)
    return pl.pallas_call(
        functools.partial(_mixer_body, L=L, vlo=vlo, vhi=vhi, sample=sample),
        out_shape=(
            jax.ShapeDtypeStruct((zb.shape[0], D_MODEL), bf16),
            jax.ShapeDtypeStruct((n_state, N_HEADS, DK, DV), f32),
            jax.ShapeDtypeStruct((n_state, N_HEADS, DK), f32),
            jax.ShapeDtypeStruct((n_state, 1, N_SMALL), f32),
            jax.ShapeDtypeStruct((n_state, N_HEADS, DK, DV), f32),
        ),
        grid=(nb, nt),
        in_specs=in_specs,
        out_specs=(row_spec(D_MODEL), st4, st_n, st_m, st4),
        scratch_shapes=_mixer_scratch(rows),
        compiler_params=pltpu.CompilerParams(
            dimension_semantics=("arbitrary", "arbitrary"), vmem_limit_bytes=VMEM_LIMIT_BYTES),
        name="mixer_sample" if sample else "mixer_prompt",
    )(*args)


def _fused_call(x2d, ada4, g_norm, w_big, ws_hi, ws_lo, consts, *, nb, nt, L):
    ada, row0 = ada4
    row_spec = lambda n: pl.BlockSpec((L, n), lambda i, j: (i * nt + j, 0))
    st4 = pl.BlockSpec((1, N_HEADS, DK, DV), lambda i, j: (i, 0, 0, 0))
    st_n = pl.BlockSpec((1, N_HEADS, DK), lambda i, j: (i, 0, 0))
    st_m = pl.BlockSpec((1, 1, N_SMALL), lambda i, j: (i, 0, 0))
    return pl.pallas_call(
        functools.partial(_fused_body, L=L),
        out_shape=(
            jax.ShapeDtypeStruct((nb * nt, SUBLANES, PROJ_COLS), f32),
            jax.ShapeDtypeStruct((x2d.shape[0], D_MODEL), bf16),
            jax.ShapeDtypeStruct((nb, N_HEADS, DK, DV), f32),
            jax.ShapeDtypeStruct((nb, N_HEADS, DK), f32),
            jax.ShapeDtypeStruct((nb, 1, N_SMALL), f32),
            jax.ShapeDtypeStruct((nb, N_HEADS, DK, DV), f32),
        ),
        grid=(nb, nt),
        in_specs=[
            row_spec(D_MODEL), _mod_spec(1, 0, row0), _mod_spec(1, 1, row0),
            _const_spec((1, D_MODEL)), _const_spec((D_MODEL, N_BIG)),
            _const_spec((D_MODEL, N_SMALL)), _const_spec((D_MODEL, N_SMALL)),
        ] + [_const_spec(c.shape) for c in consts],
        out_specs=(
            pl.BlockSpec((1, SUBLANES, PROJ_COLS), lambda i, j: (i * nt + j, 0, 0)),
            row_spec(D_MODEL), st4, st_n, st_m, st4,
        ),
        scratch_shapes=([pltpu.VMEM((L, PROJ_COLS), f32)] * (N_BIG // PROJ_COLS)
                        + [pltpu.VMEM((L, N_SMALL), f32)] + _mixer_scratch(L)),
        compiler_params=pltpu.CompilerParams(
            dimension_semantics=("arbitrary", "arbitrary"), vmem_limit_bytes=VMEM_LIMIT_BYTES),
        name="proj_mixer_prompt",
    )(x2d, ada, ada, g_norm, w_big, ws_hi, ws_lo, *consts)


def _out_body(u_ref, x_ref, gate1_ref, shift2_ref, scale2_ref, gate2_ref, gn2_ref, gfin_ref,
              wout_ref, wg_ref, wu_ref, wd_ref, y_ref, *, bt, tt):
    x1 = x_ref[...] + _mod_rows(gate1_ref, bt, tt) * _mm(u_ref[...], wout_ref[...])
    h2 = _rms(x1) * gn2_ref[...]
    h2 = (h2 * (1.0 + _mod_rows(scale2_ref, bt, tt)) + _mod_rows(shift2_ref, bt, tt)).astype(bf16)
    acc = jnp.zeros(x1.shape, f32)
    for c in range(0, D_FF, FFN_COLS):
        g = _mm(h2, wg_ref[:, c:c + FFN_COLS])
        up = _mm(h2, wu_ref[:, c:c + FFN_COLS])
        acc = acc + _mm((g * _sigmoid(g) * up).astype(bf16), wd_ref[c:c + FFN_COLS, :])
    x2 = x1 + _mod_rows(gate2_ref, bt, tt) * acc
    y_ref[...] = _rms(x2) * gfin_ref[...]


def _out_call(u2d, x2d, ada4, g_norm2, g_final, w_out, w_gate, w_up, w_down, *, nb, nt, bt, tt):
    ada, row0 = ada4
    rows = bt * tt
    row_spec = pl.BlockSpec((rows, D_MODEL), lambda i, j: (i * nt + j, 0))
    return pl.pallas_call(
        functools.partial(_out_body, bt=bt, tt=tt),
        out_shape=jax.ShapeDtypeStruct(x2d.shape, f32),
        grid=(nb, nt),
        in_specs=[
            row_spec, row_spec,
            _mod_spec(bt, 2, row0), _mod_spec(bt, 3, row0), _mod_spec(bt, 4, row0), _mod_spec(bt, 5, row0),
            _const_spec((1, D_MODEL)), _const_spec((1, D_MODEL)),
            _const_spec((D_MODEL, D_MODEL)), _const_spec((D_MODEL, D_FF)),
            _const_spec((D_MODEL, D_FF)), _const_spec((D_FF, D_MODEL)),
        ],
        out_specs=row_spec,
        compiler_params=pltpu.CompilerParams(
            dimension_semantics=("arbitrary", "arbitrary"), vmem_limit_bytes=VMEM_LIMIT_BYTES),
        name="out_ffn",
    )(u2d, x2d, ada, ada, ada, ada, g_norm2, g_final, w_out, w_gate, w_up, w_down)


PROMPT_CHUNK = 256
PROMPT_FFN_ROWS = 512
SAMPLE_L = SUBLANES
SAMPLE_MIX_GROUPS = 8
SAMPLE_DENSE_GROUPS = 32


W_IN_SEG_A = (0, 3072)
W_IN_GATES = 3072
W_IN_SEG_B = (3080, 6152)
W_IN_LR = 6152
W_IN_SEG_C = (6168, 8216)
LANES = 128


PREP_COLS = 512
SHIFT_B = W_IN_SEG_B[0] - OFF_QB
SHIFT_C = W_IN_SEG_C[0] - OFF_GA
PREP_HALO = 32


def _prep_body(a_ref, b_ref, gt_ref, lt_ref, wb_ref, wh_ref, wl_ref):
    j = pl.program_id(0)
    both = jnp.concatenate([a_ref[...], b_ref[0:PREP_HALO, :]], axis=0)

    def emit(shift):
        wb_ref[...] = both[shift:shift + PREP_COLS, :].T.astype(bf16)

    pl.when(j < OFF_QB // PREP_COLS)(lambda: emit(0))
    pl.when((j >= OFF_QB // PREP_COLS) & (j < OFF_GA // PREP_COLS))(lambda: emit(SHIFT_B))
    pl.when(j >= OFF_GA // PREP_COLS)(lambda: emit(SHIFT_C))

    @pl.when(j == 0)
    def _narrow():
        row = lax.broadcasted_iota(jnp.int32, (LANES, 1), 0)
        small = jnp.where(row < LANE_LR, gt_ref[...], jnp.where(row < LANE_LR + GLA_RANK, lt_ref[...], 0.0)).T
        hi = small.astype(bf16)
        wh_ref[...] = hi
        wl_ref[...] = (small - hi.astype(f32)).astype(bf16)


def _prep_call(w_in):
    assert W_IN_GATES % LANES == 0 and (W_IN_LR - LANE_LR) % LANES == 0
    assert max(SHIFT_B, SHIFT_C) <= PREP_HALO and SHIFT_B % SUBLANES == 0 and SHIFT_C % SUBLANES == 0
    wt = jnp.transpose(w_in)
    last = (wt.shape[0] - 1) // PREP_COLS
    const = lambda shape: pl.BlockSpec(shape, lambda j: (0, 0))
    return pl.pallas_call(
        _prep_body,
        out_shape=(
            jax.ShapeDtypeStruct((D_MODEL, N_BIG), bf16),
            jax.ShapeDtypeStruct((D_MODEL, N_SMALL), bf16),
            jax.ShapeDtypeStruct((D_MODEL, N_SMALL), bf16),
        ),
        grid=(N_BIG // PREP_COLS,),
        in_specs=[
            pl.BlockSpec((PREP_COLS, D_MODEL), lambda j: (j, 0)),
            pl.BlockSpec((PREP_COLS, D_MODEL), lambda j: (jnp.minimum(j + 1, last), 0)),
            pl.BlockSpec((LANES, D_MODEL), lambda j: (W_IN_GATES // LANES, 0)),
            pl.BlockSpec((LANES, D_MODEL), lambda j: ((W_IN_LR - LANE_LR) // LANES, 0)),
        ],
        out_specs=(
            pl.BlockSpec((D_MODEL, PREP_COLS), lambda j: (0, j)),
            const((D_MODEL, N_SMALL)), const((D_MODEL, N_SMALL)),
        ),
        compiler_params=pltpu.CompilerParams(
            dimension_semantics=("arbitrary",), vmem_limit_bytes=VMEM_LIMIT_BYTES),
        name="w_in_prep",
    )(wt, wt, wt, wt)


def _prep_layer_weights(w_in, w_alpha, b_igate, b_fgate):
    w_big, ws_hi, ws_lo = _prep_call(w_in)
    w_al = jnp.zeros((N_SMALL, QK), f32).at[LANE_LR:LANE_LR + GLA_RANK].set(w_alpha).astype(bf16)
    g_bias = jnp.concatenate([b_igate, b_fgate, jnp.zeros((N_SMALL - 2 * N_HEADS,), f32)])[None, :]
    return w_big, ws_hi, ws_lo, w_al, g_bias


def _layer(xp2d, xs2d, st_s, ada_p, ada_s, p, *, bp, tp, bs):
    (g_norm1, g_norm2, w_in, conv_w, conv_b, b_igate, b_fgate, g_head_a,
     w_alpha, b_alpha, g_head_b, w_out, w_ffn_gate, w_ffn_up, w_ffn_down, g_final_or_ones) = p
    w_big, ws_hi, ws_lo, w_al, g_bias = _prep_layer_weights(w_in, w_alpha, b_igate, b_fgate)
    consts = (conv_w, conv_b[None, :], g_bias, w_al, b_alpha[None, :],
              g_head_a.reshape(1, D_MODEL), g_head_b.reshape(1, D_MODEL))
    w_out16, w_g16, w_u16, w_d16 = (w.astype(bf16) for w in (w_out, w_ffn_gate, w_ffn_up, w_ffn_down))
    gn1, gn2, gfin = g_norm1[None, :], g_norm2[None, :], g_final_or_ones[None, :]

    ntp = tp // PROMPT_CHUNK
    tail, u, c_p, n_p, m_p, s_p = _fused_call(xp2d, ada_p, gn1, w_big, ws_hi, ws_lo, consts,
                                              nb=bp, nt=ntp, L=PROMPT_CHUNK)
    conv_p = tail.reshape(bp, ntp, SUBLANES, 2 * QK)[:, ntp - 1, SUBLANES - (CONV_W - 1):, :]
    yp = _out_call(u, xp2d, ada_p, gn2, gfin, w_out16, w_g16, w_u16, w_d16,
                   nb=bp, nt=tp // PROMPT_FFN_ROWS, bt=1, tt=PROMPT_FFN_ROWS)

    conv0, c0, n0, m0, s0 = st_s
    nbd = bs // SAMPLE_DENSE_GROUPS
    zb_s, zs_s, _ = _inproj_call(xs2d, ada_s, gn1, w_big, ws_hi, ws_lo,
                                 nb=nbd, nt=1, bt=SAMPLE_DENSE_GROUPS, tt=SAMPLE_L, zdtype=f32)
    lo = CONV_W - 1
    n_tok = SAMPLE_L - lo - 1
    conv_s = zb_s[:, OFF_QKA:OFF_QKA + 2 * QK].reshape(bs, SAMPLE_L, 2 * QK)[:, n_tok:lo + n_tok, :]
    conv0_rows = jnp.pad(conv0, ((0, 0), (0, SAMPLE_L - lo), (0, 0))).reshape(bs * SAMPLE_L, 2 * QK)
    m0_lanes = jnp.pad(m0[:, None, :], ((0, 0), (0, 0), (0, N_SMALL - N_HEADS)))
    u_s, c_s, n_s, m_s, s_s = _mixer_call(
        zb_s, zs_s, (conv0_rows, c0, n0, m0_lanes, s0), consts,
        nb=bs // SAMPLE_MIX_GROUPS, nt=1, bt=SAMPLE_MIX_GROUPS, L=SAMPLE_L, vlo=lo, vhi=lo + n_tok)
    ys = _out_call(u_s, xs2d, ada_s, gn2, gfin, w_out16, w_g16, w_u16, w_d16,
                   nb=nbd, nt=1, bt=SAMPLE_DENSE_GROUPS, tt=SAMPLE_L)
    st_p = (conv_p, c_p, n_p, m_p[:, 0, :N_HEADS], s_p)
    st_s_new = (conv_s, c_s, n_s, m_s[:, 0, :N_HEADS], s_s)
    return yp, ys, st_p, st_s_new


def kernel(x_prompt, x_sample, state_conv, state_C, state_n, state_m, state_S, c_prompt, c_sample,
           g_norm1, g_norm2, w_ada, b_ada, w_in, conv_w, conv_b, b_igate, b_fgate, g_head_a,
           w_alpha, b_alpha, g_head_b, w_out, w_ffn_gate, w_ffn_up, w_ffn_down, g_final):
    bp, tp, _ = x_prompt.shape
    bs, ts, _ = x_sample.shape
    depth = w_in.shape[0]
    assert depth == 1, "the final norm is fused into the (single) layer's FFN kernel"
    lo = CONV_W - 1
    assert ts == SAMPLE_L - lo - 1
    xp2d = x_prompt.reshape(bp * tp, D_MODEL)
    xs2d = jnp.pad(x_sample, ((0, 0), (lo, SAMPLE_L - lo - ts), (0, 0))).reshape(bs * SAMPLE_L, D_MODEL)
    c_all = jnp.concatenate([c_sample, c_prompt], axis=0)
    new_p = [[] for _ in range(5)]
    new_s = [[] for _ in range(5)]
    for l in range(depth):
        ada = _ada_call(c_all, w_ada[l], b_ada[l][None, :])
        p = (g_norm1[l], g_norm2[l], w_in[l], conv_w[l], conv_b[l], b_igate[l], b_fgate[l], g_head_a[l],
             w_alpha[l], b_alpha[l], g_head_b[l], w_out[l], w_ffn_gate[l], w_ffn_up[l], w_ffn_down[l], g_final)
        st_s = (state_conv[l], state_C[l], state_n[l], state_m[l], state_S[l])
        xp2d, xs2d, st_p, st_s_new = _layer(xp2d, xs2d, st_s, (ada, bs), (ada, 0), p, bp=bp, tp=tp, bs=bs)
        for j in range(5):
            new_p[j].append(st_p[j])
            new_s[j].append(st_s_new[j])
    y_prompt = xp2d.reshape(bp, tp, D_MODEL)
    y_sample = xs2d.reshape(bs, SAMPLE_L, D_MODEL)[:, lo:lo + ts, :]
    outs_p = [jnp.stack(a, axis=0) for a in new_p]
    outs_s = [jnp.stack(a, axis=0) for a in new_s]
    return (y_prompt, y_sample, *outs_p, *outs_s)
```

```python
import functools

import jax
import jax.numpy as jnp
from jax import lax
from jax.experimental import pallas as pl
from jax.experimental.pallas import tpu as pltpu

f32 = jnp.float32
bf16 = jnp.bfloat16

D_MODEL = 1024
N_HEADS = 4
DK = 128
DV = 256
QK = N_HEADS * DK
CONV_W = 4
GLA_RANK = 16
GLA_TAU = 16.0
D_FF = 2816
EPS = 1e-6
N_ADA = 6

N_BIG = 8192
N_SMALL = 128
OFF_QKA, OFF_VA, OFF_OA, OFF_QB, OFF_KB, OFF_VB, OFF_RB, OFF_GA, OFF_GB = (
    0, 1024, 2048, 3072, 3584, 4096, 5120, 6144, 7168)
LANE_IG, LANE_FG, LANE_LR = 0, N_HEADS, 2 * N_HEADS

SUBLANES = 8
NEG = -1e30
SAFE_LOG_RANGE = 60.0
V7X_VMEM_BYTES = 64 * 1024 * 1024
VMEM_LIMIT_BYTES = V7X_VMEM_BYTES - 8 * 1024 * 1024
FFN_COLS = 256
PROJ_COLS = 1024
DVA = DV + DK


def _mm(a, b):
    return jnp.dot(a, b, preferred_element_type=f32)


def _mm_nt(a, b):
    return lax.dot_general(a, b, (((1,), (1,)), ((), ())), preferred_element_type=f32)


def _mm_tn(a, b):
    return lax.dot_general(a, b, (((0,), (0,)), ((), ())), preferred_element_type=f32)


def _split3(x):
    hi = x.astype(bf16)
    r = x - hi.astype(f32)
    mid = r.astype(bf16)
    lo = (r - mid.astype(f32)).astype(bf16)
    return hi, mid, lo


def _sigmoid(x):
    return 0.5 * jnp.tanh(0.5 * x) + 0.5


def _log_sigmoid(x):
    return jnp.minimum(x, 0.0) - jnp.log(1.0 + jnp.exp(-jnp.abs(x)))


def _rms(x):
    return x * lax.rsqrt(jnp.mean(x * x, axis=-1, keepdims=True) + EPS)


def _row_to_col(row, eye):
    return jnp.sum(jnp.where(eye, row, 0.0), axis=1, keepdims=True)


def _diag_row(mat, eye):
    return jnp.sum(jnp.where(eye, mat, 0.0), axis=0, keepdims=True)


def _mod_rows(mod_ref, bt, tt):
    if bt == 1:
        return mod_ref[0, 0]
    v = mod_ref[...].reshape(bt, 1, D_MODEL)
    return jnp.broadcast_to(v, (bt, tt, D_MODEL)).reshape(bt * tt, D_MODEL)


def _ada_body(c_ref, w_ref, b_ref, o_ref):
    ch, cl, _ = _split3(c_ref[...])
    wh, wl, _ = _split3(w_ref[...])
    o_ref[:, 0, 0, :] = _mm(ch, wh) + (_mm(cl, wh) + _mm(ch, wl)) + b_ref[...]


def _ada_call(c_all, w_ada, b_ada):
    m = c_all.shape[0]
    n = w_ada.shape[1]
    return pl.pallas_call(
        _ada_body,
        out_shape=jax.ShapeDtypeStruct((m, n // D_MODEL, 1, D_MODEL), f32),
        grid=(n // D_MODEL,),
        in_specs=[
            pl.BlockSpec((m, D_MODEL), lambda i: (0, 0)),
            pl.BlockSpec((D_MODEL, D_MODEL), lambda i: (0, i)),
            pl.BlockSpec((1, D_MODEL), lambda i: (0, i)),
        ],
        out_specs=pl.BlockSpec((m, 1, 1, D_MODEL), lambda i: (0, i, 0, 0)),
        compiler_params=pltpu.CompilerParams(
            dimension_semantics=("arbitrary",), vmem_limit_bytes=VMEM_LIMIT_BYTES),
        name="ada_proj",
    )(c_all, w_ada, b_ada)


def _inproj_parts(x_ref, shift_ref, scale_ref, g_ref, wbig_ref, wsh_ref, wsl_ref,
                  zb_ref, zs_ref, tail_ref, *, bt, tt):
    rows = bt * tt
    h = _rms(x_ref[...]) * g_ref[...]
    h = h * (1.0 + _mod_rows(scale_ref, bt, tt)) + _mod_rows(shift_ref, bt, tt)
    hh = h.astype(bf16)
    hl = (h - hh.astype(f32)).astype(bf16)

    def narrow():
        wsh = wsh_ref[...]
        zs_ref[...] = _mm(hh, wsh) + (_mm(hl, wsh) + _mm(hh, wsl_ref[...]))

    def wide(idx):
        c = idx * PROJ_COLS
        z = _mm(hh, wbig_ref[:, c:c + PROJ_COLS])
        if isinstance(zb_ref, (list, tuple)):
            zb_ref[idx][...] = z
        else:
            zb_ref[:, c:c + PROJ_COLS] = z.astype(zb_ref.dtype)
        if c == OFF_QKA:
            tail_ref[0] = z[rows - SUBLANES:rows, :]

    return narrow, wide


def _inproj_body(*refs, bt, tt):
    narrow, wide = _inproj_parts(*refs, bt=bt, tt=tt)
    narrow()
    for idx in range(N_BIG // PROJ_COLS):
        wide(idx)


def _const_spec(shape):
    nd = len(shape)
    return pl.BlockSpec(shape, lambda i, j: (0,) * nd, pipeline_mode=pl.Buffered(1))


def _mod_spec(bt, k, row0):
    assert row0 % bt == 0
    return pl.BlockSpec((bt, 1, 1, D_MODEL), lambda i, j: (row0 // bt + i, k, 0, 0))


def _inproj_call(x2d, ada4, g_norm, w_big, ws_hi, ws_lo, *, nb, nt, bt, tt, zdtype):
    ada, row0 = ada4
    rows = bt * tt
    total = x2d.shape[0]
    row_spec = lambda n: pl.BlockSpec((rows, n), lambda i, j: (i * nt + j, 0))
    return pl.pallas_call(
        functools.partial(_inproj_body, bt=bt, tt=tt),
        out_shape=(
            jax.ShapeDtypeStruct((total, N_BIG), zdtype),
            jax.ShapeDtypeStruct((total, N_SMALL), f32),
            jax.ShapeDtypeStruct((nb * nt, SUBLANES, PROJ_COLS), f32),
        ),
        grid=(nb, nt),
        in_specs=[
            row_spec(D_MODEL), _mod_spec(bt, 0, row0), _mod_spec(bt, 1, row0),
            _const_spec((1, D_MODEL)), _const_spec((D_MODEL, N_BIG)),
            _const_spec((D_MODEL, N_SMALL)), _const_spec((D_MODEL, N_SMALL)),
        ],
        out_specs=(
            row_spec(N_BIG), row_spec(N_SMALL),
            pl.BlockSpec((1, SUBLANES, PROJ_COLS), lambda i, j: (i * nt + j, 0, 0)),
        ),
        compiler_params=pltpu.CompilerParams(
            dimension_semantics=("arbitrary", "arbitrary"), vmem_limit_bytes=VMEM_LIMIT_BYTES),
        name="in_proj",
    )(x2d, ada, ada, g_norm, w_big, ws_hi, ws_lo)


N_MIXER_CONSTS = 7
N_MIXER_INIT = 5
N_MIXER_OUTS = 5


def _mixer_scratch(rows):
    return [pltpu.VMEM((rows, QK), f32)] * 5 + [
        pltpu.VMEM((rows, N_SMALL), f32), pltpu.VMEM((rows, N_SMALL), f32), pltpu.VMEM((rows, QK), f32),
    ] + [pltpu.VMEM((rows, D_MODEL), f32)] * 4 + [
        pltpu.VMEM((SUBLANES, 2 * QK), f32), pltpu.VMEM((N_HEADS, DK, DVA), f32),
    ]


def _mixer_body(*refs, L, vlo, vhi, sample):
    zb_ref, zs_ref = refs[:2]
    k = 2
    init = None
    if sample:
        init = refs[k:k + N_MIXER_INIT]
        k += N_MIXER_INIT
    consts = refs[k:k + N_MIXER_CONSTS]
    k += N_MIXER_CONSTS
    outs = refs[k:k + N_MIXER_OUTS]
    _mixer_compute(zb_ref, zs_ref, init, consts, outs, refs[k + N_MIXER_OUTS:], L=L, vlo=vlo, vhi=vhi)


def _fused_body(x_ref, shift_ref, scale_ref, g_ref, wbig_ref, wsh_ref, wsl_ref, *refs, L):
    consts = refs[:N_MIXER_CONSTS]
    tail_ref = refs[N_MIXER_CONSTS]
    outs = refs[N_MIXER_CONSTS + 1:N_MIXER_CONSTS + 1 + N_MIXER_OUTS]
    k = N_MIXER_CONSTS + 1 + N_MIXER_OUTS
    zb_s = list(refs[k:k + N_BIG // PROJ_COLS])
    zs_s = refs[k + N_BIG // PROJ_COLS]
    scratch = refs[k + N_BIG // PROJ_COLS + 1:]
    _zero_carried_state(outs, scratch)
    narrow, wide = _inproj_parts(x_ref, shift_ref, scale_ref, g_ref, wbig_ref, wsh_ref, wsl_ref,
                                 zb_s, zs_s, tail_ref, bt=1, tt=L)

    plan = {
        "start": (OFF_QKA, OFF_QB),
        "gate_cumsum": (OFF_GA, OFF_GB),
        "gla_cumsum": (OFF_OA, OFF_RB),
        "gate_products": (OFF_VA, OFF_VB),
    }

    def produce(stage):
        if stage == "start":
            narrow()
        for off in plan[stage]:
            wide(off // PROJ_COLS)

    _mixer_compute(zb_s, zs_s, None, consts, outs, scratch, L=L, vlo=0, vhi=L, zero_init=False,
                   produce=produce)


def _zero_carried_state(outs, scratch):
    _, _, _, mo_ref, So_ref = outs
    carry_s, caug_s = scratch[-2:]

    @pl.when(pl.program_id(1) == 0)
    def _zero():
        caug_s[...] = jnp.zeros(caug_s.shape, f32)
        mo_ref[...] = jnp.zeros(mo_ref.shape, f32)
        So_ref[...] = jnp.zeros(So_ref.shape, f32)
        carry_s[...] = jnp.zeros(carry_s.shape, f32)


def _mixer_compute(zb_ref, zs_ref, init, consts, outs, scratch, *, L, vlo, vhi, zero_init=True,
                   produce=lambda stage: None):
    sample = init is not None
    if sample:
        conv0_ref, C0_ref, n0_ref, m0_ref, S0_ref = init
    cw_ref, cb_ref, gb_ref, wal_ref, bal_ref, gha_ref, ghb_ref = consts
    u_ref, Co_ref, no_ref, mo_ref, So_ref = outs
    qa_s, ka_s, qh_s, kh_s, kb_s, x_s, r_s, bg_s, ha_s, hb_s, pa_s, pb_s, carry_s, caug_s = scratch
    rows = zs_ref.shape[0]
    every = slice(None)

    def zcols(rsel, off, width):
        if isinstance(zb_ref, (list, tuple)):
            grp, o = divmod(off, PROJ_COLS)
            assert o + width <= PROJ_COLS
            return zb_ref[grp][rsel, o:o + width]
        return zb_ref[rsel, off:off + width]
    n_groups = rows // L
    log2_l = L.bit_length() - 1
    masked = not (vlo == 0 and vhi == L)
    scale = DK ** -0.5

    if not sample and zero_init:
        _zero_carried_state(outs, scratch)

    row_in = lax.broadcasted_iota(jnp.int32, (rows, 1), 0) & (L - 1)
    valid = (row_in >= vlo) & (row_in < vhi)

    produce("start")
    zq =zcols(every, OFF_QKA, 2 * QK).astype(f32)
    if sample:
        zq = jnp.where(row_in < CONV_W - 1, conv0_ref[...], zq)
        carry = jnp.zeros((SUBLANES, 2 * QK), f32)
    else:
        carry = carry_s[...]
        carry_s[...] = zq[rows - SUBLANES:rows, :]
    xe = jnp.concatenate([carry, zq], axis=0)
    cw = cw_ref[...]
    conv = cb_ref[...] + zq * cw[CONV_W - 1:CONV_W]
    for k in range(CONV_W - 1):
        o = SUBLANES - (CONV_W - 1) + k
        conv = conv + xe[o:o + rows] * cw[k:k + 1]
    qk = conv * _sigmoid(conv)
    qa_s[...] = qk[:, :QK] * scale
    ka_s[...] = qk[:, QK:]

    lane = lax.broadcasted_iota(jnp.int32, (1, N_SMALL), 1)
    zs = zs_ref[...]
    pre = zs + gb_ref[...]
    gate = jnp.where(lane < LANE_FG, pre, _log_sigmoid(pre))
    gate_f = jnp.where(valid, gate, 0.0) if masked else gate
    rr = lax.broadcasted_iota(jnp.int32, (rows, rows), 0)
    cc = lax.broadcasted_iota(jnp.int32, (rows, rows), 1)
    tril_mask = ((rr >> log2_l) == (cc >> log2_l)) & (cc <= rr)
    tril = jnp.where(tril_mask, 1.0, 0.0).astype(bf16)

    def cumsum_groups(v):
        hi, mid, lo = _split3(v)
        return _mm(tril, hi) + (_mm(tril, mid) + _mm(tril, lo))

    produce("gate_cumsum")
    x_all = jnp.where(lane < LANE_FG, gate, cumsum_groups(gate_f))
    x_s[...] = x_all
    b_al = pltpu.roll(x_all, N_SMALL - N_HEADS, axis=1)
    r_all = x_all - b_al
    r_ok = (lane < N_HEADS) & valid if masked else jnp.broadcast_to(lane < N_HEADS, (rows, N_SMALL))
    r_s[...] = jnp.where(r_ok, r_all, NEG)
    range_m = jnp.max(jnp.where(r_ok, r_all, NEG)) - jnp.min(jnp.where(r_ok, r_all, -NEG))

    la = _log_sigmoid(_mm(zs.astype(bf16), wal_ref[...]) + bal_ref[...]) * (1.0 / GLA_TAU)
    kb = zcols(every, OFF_KB, QK).astype(f32)
    if masked:
        la = jnp.where(valid, la, 0.0)
        kb = jnp.where(valid, kb, 0.0)
    produce("gla_cumsum")
    bg = cumsum_groups(la)
    bg_s[...] = bg
    range_g = jnp.max(-bg)
    qh_s[...] = zcols(every, OFF_QB, QK).astype(f32) * scale * jnp.exp(bg)
    kh_s[...] = kb * jnp.exp(-bg)
    fast_ok = jnp.logical_and(range_m <= SAFE_LOG_RANGE, range_g <= SAFE_LOG_RANGE)

    produce("gate_products")
    col = lambda off: zcols(every, off, D_MODEL).astype(f32)
    pa_s[...] = _sigmoid(col(OFF_GA)) * _sigmoid(col(OFF_OA)) * gha_ref[...]
    rb = col(OFF_RB)
    pb_s[...] = _sigmoid(col(OFF_GB)) * (rb * _sigmoid(rb)) * ghb_ref[...]

    tri = lax.broadcasted_iota(jnp.int32, (L, L), 0) >= lax.broadcasted_iota(jnp.int32, (L, L), 1)
    eye = lax.broadcasted_iota(jnp.int32, (DK, DK), 0) == lax.broadcasted_iota(jnp.int32, (DK, DK), 1)
    lane_h = lane
    lrow = lax.broadcasted_iota(jnp.int32, (1, L), 1)
    lcol = lax.broadcasted_iota(jnp.int32, (L, 1), 0)
    vrow = (lrow >= vlo) & (lrow < vhi)
    ones16 = jnp.ones((L, DK), bf16)

    def load_state(g, h):
        if sample:
            n_mat = jnp.broadcast_to(_row_to_col(n0_ref[g, pl.ds(h, 1), :], eye), (DK, DK))
            return jnp.concatenate([C0_ref[g, h], n_mat], axis=1), S0_ref[g, h]
        return caug_s[h], So_ref[0, h]

    def store_state(g, h, caug, s_new):
        sidx = g if sample else 0
        if not sample:
            caug_s[h] = caug
        Co_ref[sidx, h] = caug[:, :DV]
        no_ref[sidx, pl.ds(h, 1), :] = _diag_row(caug[:, DV:], eye)
        So_ref[sidx, h] = s_new

    def group_fast(g):
        r0 = g * L
        rs = pl.ds(r0, L)
        xc = x_s[rs, :]
        rc = r_s[rs, :]
        m_prev = m0_ref[g] if sample else mo_ref[0]
        bg_tail = bg_s[pl.ds(r0 + L - SUBLANES, SUBLANES), :]
        heads = []
        for h in range(N_HEADS):
            ks = slice(h * DK, (h + 1) * DK)
            caug, s_st = load_state(g, h)
            c0 = jnp.max(rc[:, h:h + 1], axis=0, keepdims=True)
            q16 = qa_s[rs, ks].astype(bf16)
            kt16 = (ka_s[rs, ks] * jnp.exp(rc[:, h:h + 1] - c0)).astype(bf16)
            qh16 = qh_s[rs, ks].astype(bf16)
            kh = kh_s[rs, ks]
            heads.append(dict(
                caug=caug, s_st=s_st, c0=c0, kt16=kt16, kh=kh,
                va=jnp.concatenate([zcols(rs, OFF_VA + h * DV, DV).astype(bf16), ones16], axis=1),
                vb16=zcols(rs, OFF_VB + h * DV, DV).astype(bf16),
                s_raw=_mm_nt(q16, kt16), a_raw=_mm_nt(qh16, kh.astype(bf16)),
                ci=_mm(q16, caug.astype(bf16)), oi=_mm(qh16, s_st.astype(bf16))))
        for p in heads:
            p["ni"] = _mm(jnp.where(tri, p["s_raw"], 0.0).astype(bf16), p["va"])
            p["o"] = _mm(jnp.where(tri, p["a_raw"], 0.0).astype(bf16), p["vb16"]) + p["oi"]
        for h, p in enumerate(heads):
            p["eb_last"] = jnp.exp(bg_tail[SUBLANES - 1:SUBLANES, h * DK:(h + 1) * DK])
            p["kv"] = _mm_tn(p["kt16"], p["va"])
            p["skv"] = _mm_tn((p["kh"] * p["eb_last"]).astype(bf16), p["vb16"])
        m_vec = m_prev
        for h, p in enumerate(heads):
            m_st = m_prev[:, h:h + 1]
            b_col = xc[:, LANE_FG + h:LANE_FG + h + 1]
            b_last = b_col[L - 1:L, :]
            c0 = p["c0"]
            mm = jnp.maximum(c0, m_st)
            comb = jnp.exp(c0 - mm) * p["ni"] + jnp.exp(m_st - mm) * p["ci"]
            inv = 1.0 / jnp.maximum(jnp.abs(comb[:, DV:]), jnp.exp(-(b_col + mm)))
            ha_s[rs, h * DV:h * DV + DK] = comb[:, :DK] * inv
            ha_s[rs, h * DV + DK:(h + 1) * DV] = comb[:, DK:DV] * inv
            hb_s[rs, h * DV:(h + 1) * DV] = p["o"]
            m_new = jnp.maximum(b_last + m_st, b_last + c0)
            caug_new = jnp.exp(b_last + m_st - m_new) * p["caug"] + jnp.exp(b_last + c0 - m_new) * p["kv"]
            s_new = _row_to_col(p["eb_last"], eye) * p["s_st"] + p["skv"]
            store_state(g, h, caug_new, s_new)
            m_vec = jnp.where(lane_h == h, m_new, m_vec)
        mo_ref[g if sample else 0] = m_vec

    def group_exact(g):
        static = isinstance(g, int)
        r0 = g * L if static else pl.multiple_of(g * L, L)
        tail0 = r0 + L - SUBLANES if static else pl.multiple_of(r0 + L - SUBLANES, SUBLANES)
        rs = pl.ds(r0, L)
        xc = x_s[rs, :]
        rc = r_s[rs, :]
        sr = lax.broadcasted_iota(jnp.int32, (2 * SUBLANES, N_SMALL), 0)
        sc = lax.broadcasted_iota(jnp.int32, (2 * SUBLANES, N_SMALL), 1)
        sel = jnp.where((sr < N_HEADS) & (sc == sr + LANE_IG), 1.0,
                        jnp.where((sr < N_HEADS) & (sc == sr + LANE_FG), -1.0, 0.0)).astype(bf16)
        xh, xm, xl = _split3(xc)
        r_rows = _mm_nt(sel, xh) + (_mm_nt(sel, xm) + _mm_nt(sel, xl))
        m_prev = m0_ref[g] if sample else mo_ref[0]
        m_vec = m_prev
        bg_tail = bg_s[pl.ds(tail0, SUBLANES), :]
        for h in range(N_HEADS):
            ks = slice(h * DK, (h + 1) * DK)
            vs = slice(h * DV, (h + 1) * DV)
            caug, s_st = load_state(g, h)
            m_st = m_prev[:, h:h + 1]
            b_col = xc[:, LANE_FG + h:LANE_FG + h + 1]
            r_col = rc[:, h:h + 1]
            b_last = b_col[L - 1:L, :]
            v16 = zcols(rs, OFF_VA + h * DV, DV).astype(bf16)
            vb16 = zcols(rs, OFF_VB + h * DV, DV).astype(bf16)
            eb_last = jnp.exp(bg_tail[SUBLANES - 1:SUBLANES, ks])
            eb_col = _row_to_col(eb_last, eye)
            c_st = caug[:, :DV]
            n_row = _diag_row(caug[:, DV:], eye)
            r_row = r_rows[h:h + 1, :]
            if masked:
                r_row = jnp.where(vrow, r_row, NEG)
            dm = jnp.where(tri, b_col + r_row, NEG)
            m_t = jnp.maximum(jnp.max(dm, axis=1, keepdims=True), b_col + m_st)
            w = jnp.exp(dm - m_t)
            wi = jnp.exp(b_col + m_st - m_t)
            q = qa_s[rs, ks]
            k = ka_s[rs, ks]
            q16 = q.astype(bf16)
            s = _mm_nt(q16, k.astype(bf16)) * w
            num = _mm(s.astype(bf16), v16) + wi * _mm(q16, c_st.astype(bf16))
            den = jnp.sum(s, axis=1, keepdims=True) + wi * jnp.sum(q * n_row, axis=1, keepdims=True)
            ha_s[rs, vs] = num / jnp.maximum(jnp.abs(den), jnp.exp(-m_t))
            m_new = jnp.maximum(b_last + m_st, b_last + jnp.max(r_row, axis=1, keepdims=True))
            wc = jnp.exp(b_last + m_st - m_new)
            kt = k * jnp.exp(b_last + r_col - m_new)
            n_new = wc * n_row + jnp.sum(kt, axis=0, keepdims=True)
            caug_new = jnp.concatenate(
                [wc * c_st + _mm_tn(kt.astype(bf16), v16),
                 jnp.broadcast_to(_row_to_col(n_new, eye), (DK, DK))], axis=1)
            qb = zcols(rs, OFF_QB + h * DK, DK).astype(f32) * scale
            bgc = bg_s[rs, ks]
            sub = lax.broadcasted_iota(jnp.int32, (SUBLANES, 1), 0)

            def score_col(s_i, a, ks=ks, qb=qb, bgc=bgc, sub=sub):
                s8 = lax.shift_left(lax.shift_right_logical(s_i, 3), 3)
                base = pl.multiple_of(r0 + s8, SUBLANES)
                pick = sub == (s_i - s8)
                b_s = jnp.sum(jnp.where(pick, bg_s[pl.ds(base, SUBLANES), ks], 0.0), axis=0, keepdims=True)
                k_s = jnp.sum(jnp.where(pick, kb_s[pl.ds(base, SUBLANES), ks], 0.0), axis=0, keepdims=True)
                e = jnp.exp(jnp.where(lcol >= s_i, bgc - b_s, NEG))
                col = jnp.sum(qb * e * k_s, axis=1, keepdims=True)
                return jnp.where(lrow == s_i, col, a)

            a = lax.fori_loop(0, L, score_col, jnp.zeros((L, L), f32))
            hb_s[rs, vs] = (_mm(a.astype(bf16), vb16)
                            + _mm((qb * jnp.exp(bgc)).astype(bf16), s_st.astype(bf16)))
            kt_b = kb_s[rs, ks] * jnp.exp(bg_tail[SUBLANES - 1:SUBLANES, ks] - bgc)
            s_new = eb_col * s_st + _mm_tn(kt_b.astype(bf16), vb16)
            store_state(g, h, caug_new, s_new)
            m_vec = jnp.where(lane_h == h, m_new, m_vec)
        mo_ref[g if sample else 0] = m_vec

    def groups_fast_batched():
        r_all_m = r_s[...]
        c0_l, m_l, mm_l, fac, wc = [], [], [], [], []
        for g in range(n_groups):
            gs = slice(g * L, (g + 1) * L)
            c0_g = jnp.max(r_all_m[gs], axis=0, keepdims=True)
            m_g = m0_ref[g]
            b_last = b_al[gs][L - 1:L, :]
            m_new = jnp.maximum(b_last + m_g, b_last + c0_g)
            mo_ref[g] = m_new
            c0_l.append(c0_g)
            m_l.append(m_g)
            mm_l.append(jnp.maximum(c0_g, m_g))
            fac.append(jnp.exp(b_last + c0_g - m_new))
            wc.append(jnp.exp(b_last + m_g - m_new))
        rows_of = lambda lst: jnp.concatenate([jnp.broadcast_to(v, (L, N_SMALL)) for v in lst], axis=0)
        c0_r, m_r, mm_r = rows_of(c0_l), rows_of(m_l), rows_of(mm_l)
        beta = jnp.exp(r_all_m - c0_r)
        gi_r = jnp.exp(c0_r - mm_r)
        wi_r = jnp.exp(m_r - mm_r)
        floor = jnp.exp(-(b_al + mm_r))
        ones_g = jnp.ones((L, DK), bf16)
        for h in range(N_HEADS):
            ks = slice(h * DK, (h + 1) * DK)
            vs = slice(h * DV, (h + 1) * DV)
            kt = ka_s[:, ks] * beta[:, h:h + 1]
            v = zcols(every, OFF_VA + h * DV, DV)
            vb = zcols(every, OFF_VB + h * DV, DV)
            q16 = qa_s[:, ks].astype(bf16)
            va = jnp.concatenate([v.astype(bf16), jnp.ones((rows, DK), bf16)], axis=1)
            s16 = jnp.where(tril_mask, _mm_nt(q16, kt.astype(bf16)), 0.0).astype(bf16)
            ni = _mm(s16, va)
            qh16 = qh_s[:, ks].astype(bf16)
            kh = kh_s[:, ks]
            a16 = jnp.where(tril_mask, _mm_nt(qh16, kh.astype(bf16)), 0.0).astype(bf16)
            o_intra = _mm(a16, vb.astype(bf16))
            ci_parts, oi_parts = [], []
            for g in range(n_groups):
                gs = slice(g * L, (g + 1) * L)
                caug, s_st = load_state(g, h)
                va_g = jnp.concatenate([v[gs].astype(bf16), ones_g], axis=1)
                ci_parts.append(_mm(qa_s[gs, ks].astype(bf16), caug.astype(bf16)))
                caug_new = wc[g][:, h:h + 1] * caug + fac[g][:, h:h + 1] * _mm_tn(kt[gs].astype(bf16), va_g)
                oi_parts.append(_mm(qh_s[gs, ks].astype(bf16), s_st.astype(bf16)))
                eb_last = jnp.exp(bg_s[gs, ks][L - 1:L, :])
                s_new = (_row_to_col(eb_last, eye) * s_st
                         + _mm_tn((kh[gs] * eb_last).astype(bf16), vb[gs].astype(bf16)))
                store_state(g, h, caug_new, s_new)
            comb = gi_r[:, h:h + 1] * ni + wi_r[:, h:h + 1] * jnp.concatenate(ci_parts, axis=0)
            inv = 1.0 / jnp.maximum(jnp.abs(comb[:, DV:]), floor[:, h:h + 1])
            ha_s[:, h * DV:h * DV + DK] = comb[:, :DK] * inv
            ha_s[:, h * DV + DK:(h + 1) * DV] = comb[:, DK:DV] * inv
            hb_s[:, vs] = o_intra + jnp.concatenate(oi_parts, axis=0)

    def run_groups(exact):
        if not exact:
            if sample and L == SUBLANES:
                groups_fast_batched()
            else:
                for g in range(n_groups):
                    group_fast(g)
        elif n_groups == 1:
            group_exact(0)
        else:
            def body(g, carry_):
                group_exact(g)
                return carry_
            lax.fori_loop(0, n_groups, body, 0)

    @pl.when(fast_ok)
    def _fast():
        run_groups(False)

    @pl.when(jnp.logical_not(fast_ok))
    def _exact():
        kb_s[...] = kb
        run_groups(True)

    for h in range(N_HEADS):
        vs = slice(h * DV, (h + 1) * DV)
        u = pa_s[:, vs] * _rms(ha_s[:, vs]) + pb_s[:, vs] * _rms(hb_s[:, vs])
        u_ref[:, vs] = u.astype(u_ref.dtype)


def _mixer_call(zb, zs, init, consts, *, nb, nt, bt, L, vlo, vhi):
    sample = init is not None
    rows = bt * L
    n_state = nb * bt
    row_spec = lambda n: pl.BlockSpec((rows, n), lambda i, j: (i * nt + j, 0))
    st4 = pl.BlockSpec((bt, N_HEADS, DK, DV), lambda i, j: (i, 0, 0, 0))
    st_n = pl.BlockSpec((bt, N_HEADS, DK), lambda i, j: (i, 0, 0))
    st_m = pl.BlockSpec((bt, 1, N_SMALL), lambda i, j: (i, 0, 0))
    in_specs = [row_spec(N_BIG), row_spec(N_SMALL)]
    args = [zb, zs]
    if sample:
        in_specs += [row_spec(2 * QK), st4, st_n, st_m, st4]
        args += list(init)
    in_specs += [_const_spec(c.shape) for c in consts]
    args += list(consts)
    return pl.pallas_call(
        functools.partial(_mixer_body, L=L, vlo=vlo, vhi=vhi, sample=sample),
        out_shape=(
            jax.ShapeDtypeStruct((zb.shape[0], D_MODEL), bf16),
            jax.ShapeDtypeStruct((n_state, N_HEADS, DK, DV), f32),
            jax.ShapeDtypeStruct((n_state, N_HEADS, DK), f32),
            jax.ShapeDtypeStruct((n_state, 1, N_SMALL), f32),
            jax.ShapeDtypeStruct((n_state, N_HEADS, DK, DV), f32),
        ),
        grid=(nb, nt),
        in_specs=in_specs,
        out_specs=(row_spec(D_MODEL), st4, st_n, st_m, st4),
        scratch_shapes=_mixer_scratch(rows),
        compiler_params=pltpu.CompilerParams(
            dimension_semantics=("arbitrary", "arbitrary"), vmem_limit_bytes=VMEM_LIMIT_BYTES),
        name="mixer_sample" if sample else "mixer_prompt",
    )(*args)


def _fused_call(x2d, ada4, g_norm, w_big, ws_hi, ws_lo, consts, *, nb, nt, L):
    ada, row0 = ada4
    row_spec = lambda n: pl.BlockSpec((L, n), lambda i, j: (i * nt + j, 0))
    st4 = pl.BlockSpec((1, N_HEADS, DK, DV), lambda i, j: (i, 0, 0, 0))
    st_n = pl.BlockSpec((1, N_HEADS, DK), lambda i, j: (i, 0, 0))
    st_m = pl.BlockSpec((1, 1, N_SMALL), lambda i, j: (i, 0, 0))
    return pl.pallas_call(
        functools.partial(_fused_body, L=L),
        out_shape=(
            jax.ShapeDtypeStruct((nb * nt, SUBLANES, PROJ_COLS), f32),
            jax.ShapeDtypeStruct((x2d.shape[0], D_MODEL), bf16),
            jax.ShapeDtypeStruct((nb, N_HEADS, DK, DV), f32),
            jax.ShapeDtypeStruct((nb, N_HEADS, DK), f32),
            jax.ShapeDtypeStruct((nb, 1, N_SMALL), f32),
            jax.ShapeDtypeStruct((nb, N_HEADS, DK, DV), f32),
        ),
        grid=(nb, nt),
        in_specs=[
            row_spec(D_MODEL), _mod_spec(1, 0, row0), _mod_spec(1, 1, row0),
            _const_spec((1, D_MODEL)), _const_spec((D_MODEL, N_BIG)),
            _const_spec((D_MODEL, N_SMALL)), _const_spec((D_MODEL, N_SMALL)),
        ] + [_const_spec(c.shape) for c in consts],
        out_specs=(
            pl.BlockSpec((1, SUBLANES, PROJ_COLS), lambda i, j: (i * nt + j, 0, 0)),
            row_spec(D_MODEL), st4, st_n, st_m, st4,
        ),
        scratch_shapes=([pltpu.VMEM((L, PROJ_COLS), f32)] * (N_BIG // PROJ_COLS)
                        + [pltpu.VMEM((L, N_SMALL), f32)] + _mixer_scratch(L)),
        compiler_params=pltpu.CompilerParams(
            dimension_semantics=("arbitrary", "arbitrary"), vmem_limit_bytes=VMEM_LIMIT_BYTES),
        name="proj_mixer_prompt",
    )(x2d, ada, ada, g_norm, w_big, ws_hi, ws_lo, *consts)


def _out_body(u_ref, x_ref, gate1_ref, shift2_ref, scale2_ref, gate2_ref, gn2_ref, gfin_ref,
              wout_ref, wg_ref, wu_ref, wd_ref, y_ref, *, bt, tt):
    x1 = x_ref[...] + _mod_rows(gate1_ref, bt, tt) * _mm(u_ref[...], wout_ref[...])
    h2 = _rms(x1) * gn2_ref[...]
    h2 = (h2 * (1.0 + _mod_rows(scale2_ref, bt, tt)) + _mod_rows(shift2_ref, bt, tt)).astype(bf16)
    acc = jnp.zeros(x1.shape, f32)
    for c in range(0, D_FF, FFN_COLS):
        g = _mm(h2, wg_ref[:, c:c + FFN_COLS])
        up = _mm(h2, wu_ref[:, c:c + FFN_COLS])
        acc = acc + _mm((g * _sigmoid(g) * up).astype(bf16), wd_ref[c:c + FFN_COLS, :])
    x2 = x1 + _mod_rows(gate2_ref, bt, tt) * acc
    y_ref[...] = _rms(x2) * gfin_ref[...]


def _out_call(u2d, x2d, ada4, g_norm2, g_final, w_out, w_gate, w_up, w_down, *, nb, nt, bt, tt):
    ada, row0 = ada4
    rows = bt * tt
    row_spec = pl.BlockSpec((rows, D_MODEL), lambda i, j: (i * nt + j, 0))
    return pl.pallas_call(
        functools.partial(_out_body, bt=bt, tt=tt),
        out_shape=jax.ShapeDtypeStruct(x2d.shape, f32),
        grid=(nb, nt),
        in_specs=[
            row_spec, row_spec,
            _mod_spec(bt, 2, row0), _mod_spec(bt, 3, row0), _mod_spec(bt, 4, row0), _mod_spec(bt, 5, row0),
            _const_spec((1, D_MODEL)), _const_spec((1, D_MODEL)),
            _const_spec((D_MODEL, D_MODEL)), _const_spec((D_MODEL, D_FF)),
            _const_spec((D_MODEL, D_FF)), _const_spec((D_FF, D_MODEL)),
        ],
        out_specs=row_spec,
        compiler_params=pltpu.CompilerParams(
            dimension_semantics=("arbitrary", "arbitrary"), vmem_limit_bytes=VMEM_LIMIT_BYTES),
        name="out_ffn",
    )(u2d, x2d, ada, ada, ada, ada, g_norm2, g_final, w_out, w_gate, w_up, w_down)


PROMPT_CHUNK = 256
PROMPT_FFN_ROWS = 512
SAMPLE_L = SUBLANES
SAMPLE_MIX_GROUPS = 8
SAMPLE_DENSE_GROUPS = 32


W_IN_SEG_A = (0, 3072)
W_IN_GATES = 3072
W_IN_SEG_B = (3080, 6152)
W_IN_LR = 6152
W_IN_SEG_C = (6168, 8216)
LANES = 128


PREP_COLS = 512
SHIFT_B = W_IN_SEG_B[0] - OFF_QB
SHIFT_C = W_IN_SEG_C[0] - OFF_GA
PREP_HALO = 32


def _prep_body(a_ref, b_ref, gt_ref, lt_ref, wb_ref, wh_ref, wl_ref):
    j = pl.program_id(0)
    both = jnp.concatenate([a_ref[...], b_ref[0:PREP_HALO, :]], axis=0)

    def emit(shift):
        wb_ref[...] = both[shift:shift + PREP_COLS, :].T.astype(bf16)

    pl.when(j < OFF_QB // PREP_COLS)(lambda: emit(0))
    pl.when((j >= OFF_QB // PREP_COLS) & (j < OFF_GA // PREP_COLS))(lambda: emit(SHIFT_B))
    pl.when(j >= OFF_GA // PREP_COLS)(lambda: emit(SHIFT_C))

    @pl.when(j == 0)
    def _narrow():
        row = lax.broadcasted_iota(jnp.int32, (LANES, 1), 0)
        small = jnp.where(row < LANE_LR, gt_ref[...], jnp.where(row < LANE_LR + GLA_RANK, lt_ref[...], 0.0)).T
        hi = small.astype(bf16)
        wh_ref[...] = hi
        wl_ref[...] = (small - hi.astype(f32)).astype(bf16)


def _prep_call(w_in):
    assert W_IN_GATES % LANES == 0 and (W_IN_LR - LANE_LR) % LANES == 0
    assert max(SHIFT_B, SHIFT_C) <= PREP_HALO and SHIFT_B % SUBLANES == 0 and SHIFT_C % SUBLANES == 0
    wt = jnp.transpose(w_in)
    last = (wt.shape[0] - 1) // PREP_COLS
    const = lambda shape: pl.BlockSpec(shape, lambda j: (0, 0))
    return pl.pallas_call(
        _prep_body,
        out_shape=(
            jax.ShapeDtypeStruct((D_MODEL, N_BIG), bf16),
            jax.ShapeDtypeStruct((D_MODEL, N_SMALL), bf16),
            jax.ShapeDtypeStruct((D_MODEL, N_SMALL), bf16),
        ),
        grid=(N_BIG // PREP_COLS,),
        in_specs=[
            pl.BlockSpec((PREP_COLS, D_MODEL), lambda j: (j, 0)),
            pl.BlockSpec((PREP_COLS, D_MODEL), lambda j: (jnp.minimum(j + 1, last), 0)),
            pl.BlockSpec((LANES, D_MODEL), lambda j: (W_IN_GATES // LANES, 0)),
            pl.BlockSpec((LANES, D_MODEL), lambda j: ((W_IN_LR - LANE_LR) // LANES, 0)),
        ],
        out_specs=(
            pl.BlockSpec((D_MODEL, PREP_COLS), lambda j: (0, j)),
            const((D_MODEL, N_SMALL)), const((D_MODEL, N_SMALL)),
        ),
        compiler_params=pltpu.CompilerParams(
            dimension_semantics=("arbitrary",), vmem_limit_bytes=VMEM_LIMIT_BYTES),
        name="w_in_prep",
    )(wt, wt, wt, wt)


def _prep_layer_weights(w_in, w_alpha, b_igate, b_fgate):
    w_big, ws_hi, ws_lo = _prep_call(w_in)
    w_al = jnp.zeros((N_SMALL, QK), f32).at[LANE_LR:LANE_LR + GLA_RANK].set(w_alpha).astype(bf16)
    g_bias = jnp.concatenate([b_igate, b_fgate, jnp.zeros((N_SMALL - 2 * N_HEADS,), f32)])[None, :]
    return w_big, ws_hi, ws_lo, w_al, g_bias


def _layer(xp2d, xs2d, st_s, ada_p, ada_s, p, *, bp, tp, bs):
    (g_norm1, g_norm2, w_in, conv_w, conv_b, b_igate, b_fgate, g_head_a,
     w_alpha, b_alpha, g_head_b, w_out, w_ffn_gate, w_ffn_up, w_ffn_down, g_final_or_ones) = p
    w_big, ws_hi, ws_lo, w_al, g_bias = _prep_layer_weights(w_in, w_alpha, b_igate, b_fgate)
    consts = (conv_w, conv_b[None, :], g_bias, w_al, b_alpha[None, :],
              g_head_a.reshape(1, D_MODEL), g_head_b.reshape(1, D_MODEL))
    w_out16, w_g16, w_u16, w_d16 = (w.astype(bf16) for w in (w_out, w_ffn_gate, w_ffn_up, w_ffn_down))
    gn1, gn2, gfin = g_norm1[None, :], g_norm2[None, :], g_final_or_ones[None, :]

    ntp = tp // PROMPT_CHUNK
    tail, u, c_p, n_p, m_p, s_p = _fused_call(xp2d, ada_p, gn1, w_big, ws_hi, ws_lo, consts,
                                              nb=bp, nt=ntp, L=PROMPT_CHUNK)
    conv_p = tail.reshape(bp, ntp, SUBLANES, 2 * QK)[:, ntp - 1, SUBLANES - (CONV_W - 1):, :]
    yp = _out_call(u, xp2d, ada_p, gn2, gfin, w_out16, w_g16, w_u16, w_d16,
                   nb=bp, nt=tp // PROMPT_FFN_ROWS, bt=1, tt=PROMPT_FFN_ROWS)

    conv0, c0, n0, m0, s0 = st_s
    nbd = bs // SAMPLE_DENSE_GROUPS
    zb_s, zs_s, _ = _inproj_call(xs2d, ada_s, gn1, w_big, ws_hi, ws_lo,
                                 nb=nbd, nt=1, bt=SAMPLE_DENSE_GROUPS, tt=SAMPLE_L, zdtype=f32)
    lo = CONV_W - 1
    n_tok = SAMPLE_L - lo - 1
    conv_s = zb_s[:, OFF_QKA:OFF_QKA + 2 * QK].reshape(bs, SAMPLE_L, 2 * QK)[:, n_tok:lo + n_tok, :]
    conv0_rows = jnp.pad(conv0, ((0, 0), (0, SAMPLE_L - lo), (0, 0))).reshape(bs * SAMPLE_L, 2 * QK)
    m0_lanes = jnp.pad(m0[:, None, :], ((0, 0), (0, 0), (0, N_SMALL - N_HEADS)))
    u_s, c_s, n_s, m_s, s_s = _mixer_call(
        zb_s, zs_s, (conv0_rows, c0, n0, m0_lanes, s0), consts,
        nb=bs // SAMPLE_MIX_GROUPS, nt=1, bt=SAMPLE_MIX_GROUPS, L=SAMPLE_L, vlo=lo, vhi=lo + n_tok)
    ys = _out_call(u_s, xs2d, ada_s, gn2, gfin, w_out16, w_g16, w_u16, w_d16,
                   nb=nbd, nt=1, bt=SAMPLE_DENSE_GROUPS, tt=SAMPLE_L)
    st_p = (conv_p, c_p, n_p, m_p[:, 0, :N_HEADS], s_p)
    st_s_new = (conv_s, c_s, n_s, m_s[:, 0, :N_HEADS], s_s)
    return yp, ys, st_p, st_s_new


def kernel(x_prompt, x_sample, state_conv, state_C, state_n, state_m, state_S, c_prompt, c_sample,
           g_norm1, g_norm2, w_ada, b_ada, w_in, conv_w, conv_b, b_igate, b_fgate, g_head_a,
           w_alpha, b_alpha, g_head_b, w_out, w_ffn_gate, w_ffn_up, w_ffn_down, g_final):
    bp, tp, _ = x_prompt.shape
    bs, ts, _ = x_sample.shape
    depth = w_in.shape[0]
    assert depth == 1, "the final norm is fused into the (single) layer's FFN kernel"
    lo = CONV_W - 1
    assert ts == SAMPLE_L - lo - 1
    xp2d = x_prompt.reshape(bp * tp, D_MODEL)
    xs2d = jnp.pad(x_sample, ((0, 0), (lo, SAMPLE_L - lo - ts), (0, 0))).reshape(bs * SAMPLE_L, D_MODEL)
    c_all = jnp.concatenate([c_sample, c_prompt], axis=0)
    new_p = [[] for _ in range(5)]
    new_s = [[] for _ in range(5)]
    for l in range(depth):
        ada = _ada_call(c_all, w_ada[l], b_ada[l][None, :])
        p = (g_norm1[l], g_norm2[l], w_in[l], conv_w[l], conv_b[l], b_igate[l], b_fgate[l], g_head_a[l],
             w_alpha[l], b_alpha[l], g_head_b[l], w_out[l], w_ffn_gate[l], w_ffn_up[l], w_ffn_down[l], g_final)
        st_s = (state_conv[l], state_C[l], state_n[l], state_m[l], state_S[l])
        xp2d, xs2d, st_p, st_s_new = _layer(xp2d, xs2d, st_s, (ada, bs), (ada, 0), p, bp=bp, tp=tp, bs=bs)
        for j in range(5):
            new_p[j].append(st_p[j])
            new_s[j].append(st_s_new[j])
    y_prompt = xp2d.reshape(bp, tp, D_MODEL)
    y_sample = xs2d.reshape(bs, SAMPLE_L, D_MODEL)[:, lo:lo + ts, :]
    outs_p = [jnp.stack(a, axis=0) for a in new_p]
    outs_s = [jnp.stack(a, axis=0) for a in new_s]
    return (y_prompt, y_sample, *outs_p, *outs_s)
```

```python
import functools

import jax
import jax.numpy as jnp
from jax import lax
from jax.experimental import pallas as pl
from jax.experimental.pallas import tpu as pltpu

f32 = jnp.float32
bf16 = jnp.bfloat16

D_MODEL = 1024
N_HEADS = 4
DK = 128
DV = 256
QK = N_HEADS * DK
CONV_W = 4
GLA_RANK = 16
GLA_TAU = 16.0
D_FF = 2816
EPS = 1e-6
N_ADA = 6

N_BIG = 8192
N_SMALL = 128
OFF_QKA, OFF_VA, OFF_OA, OFF_QB, OFF_KB, OFF_VB, OFF_RB, OFF_GA, OFF_GB = (
    0, 1024, 2048, 3072, 3584, 4096, 5120, 6144, 7168)
LANE_IG, LANE_FG, LANE_LR = 0, N_HEADS, 2 * N_HEADS

SUBLANES = 8
NEG = -1e30
SAFE_LOG_RANGE = 60.0
V7X_VMEM_BYTES = 64 * 1024 * 1024
VMEM_LIMIT_BYTES = V7X_VMEM_BYTES - 8 * 1024 * 1024
FFN_COLS = 256
PROJ_COLS = 1024
PIECE_COLS = 256
DVA = DV + DK


def _mm(a, b):
    return jnp.dot(a, b, preferred_element_type=f32)


def _mm_nt(a, b):
    return lax.dot_general(a, b, (((1,), (1,)), ((), ())), preferred_element_type=f32)


def _mm_tn(a, b):
    return lax.dot_general(a, b, (((0,), (0,)), ((), ())), preferred_element_type=f32)


def _split3(x):
    hi = x.astype(bf16)
    r = x - hi.astype(f32)
    mid = r.astype(bf16)
    lo = (r - mid.astype(f32)).astype(bf16)
    return hi, mid, lo


def _sigmoid(x):
    return 0.5 * jnp.tanh(0.5 * x) + 0.5


def _log_sigmoid(x):
    return jnp.minimum(x, 0.0) - jnp.log(1.0 + jnp.exp(-jnp.abs(x)))


def _rms(x):
    return x * lax.rsqrt(jnp.mean(x * x, axis=-1, keepdims=True) + EPS)


def _row_to_col(row, eye):
    return jnp.sum(jnp.where(eye, row, 0.0), axis=1, keepdims=True)


def _diag_row(mat, eye):
    return jnp.sum(jnp.where(eye, mat, 0.0), axis=0, keepdims=True)


def _mod_rows(mod_ref, bt, tt):
    if bt == 1:
        return mod_ref[0, 0]
    v = mod_ref[...].reshape(bt, 1, D_MODEL)
    return jnp.broadcast_to(v, (bt, tt, D_MODEL)).reshape(bt * tt, D_MODEL)


def _ada_body(c_ref, w_ref, b_ref, o_ref):
    ch, cl, _ = _split3(c_ref[...])
    wh, wl, _ = _split3(w_ref[...])
    o_ref[:, 0, 0, :] = _mm(ch, wh) + (_mm(cl, wh) + _mm(ch, wl)) + b_ref[...]


def _ada_call(c_all, w_ada, b_ada):
    m = c_all.shape[0]
    n = w_ada.shape[1]
    return pl.pallas_call(
        _ada_body,
        out_shape=jax.ShapeDtypeStruct((m, n // D_MODEL, 1, D_MODEL), f32),
        grid=(n // D_MODEL,),
        in_specs=[
            pl.BlockSpec((m, D_MODEL), lambda i: (0, 0)),
            pl.BlockSpec((D_MODEL, D_MODEL), lambda i: (0, i)),
            pl.BlockSpec((1, D_MODEL), lambda i: (0, i)),
        ],
        out_specs=pl.BlockSpec((m, 1, 1, D_MODEL), lambda i: (0, i, 0, 0)),
        compiler_params=pltpu.CompilerParams(
            dimension_semantics=("arbitrary",), vmem_limit_bytes=VMEM_LIMIT_BYTES),
        name="ada_proj",
    )(c_all, w_ada, b_ada)


def _inproj_parts(x_ref, shift_ref, scale_ref, g_ref, wbig_ref, wsh_ref, wsl_ref,
                  zb_ref, zs_ref, tail_ref, *, bt, tt):
    rows = bt * tt
    h = _rms(x_ref[...]) * g_ref[...]
    h = h * (1.0 + _mod_rows(scale_ref, bt, tt)) + _mod_rows(shift_ref, bt, tt)
    hh = h.astype(bf16)
    hl = (h - hh.astype(f32)).astype(bf16)

    def narrow():
        wsh = wsh_ref[...]
        rhs = jnp.concatenate([jnp.concatenate([wsh, wsl_ref[...]], axis=1),
                               jnp.concatenate([wsh, jnp.zeros_like(wsh)], axis=1)], axis=0)
        t = _mm(jnp.concatenate([hh, hl], axis=1), rhs)
        zs_ref[...] = t[:, :N_SMALL] + t[:, N_SMALL:]

    def wide(idx, lo=0, width=PROJ_COLS):
        c = idx * PROJ_COLS + lo
        z = _mm(hh, wbig_ref[:, c:c + width])
        if isinstance(zb_ref, (list, tuple)):
            zb_ref[idx][:, lo:lo + width] = z
        else:
            zb_ref[:, c:c + width] = z.astype(zb_ref.dtype)
        if idx == OFF_QKA // PROJ_COLS:
            tail_ref[0, :, lo:lo + width] = z[rows - SUBLANES:rows, :]

    return narrow, wide


def _inproj_body(*refs, bt, tt):
    narrow, wide = _inproj_parts(*refs, bt=bt, tt=tt)
    narrow()
    for idx in range(N_BIG // PROJ_COLS):
        wide(idx)


def _const_spec(shape):
    nd = len(shape)
    return pl.BlockSpec(shape, lambda i, j: (0,) * nd, pipeline_mode=pl.Buffered(1))


def _mod_spec(bt, k, row0):
    assert row0 % bt == 0
    return pl.BlockSpec((bt, 1, 1, D_MODEL), lambda i, j: (row0 // bt + i, k, 0, 0))


def _inproj_call(x2d, ada4, g_norm, w_big, ws_hi, ws_lo, *, nb, nt, bt, tt, zdtype):
    ada, row0 = ada4
    rows = bt * tt
    total = x2d.shape[0]
    row_spec = lambda n: pl.BlockSpec((rows, n), lambda i, j: (i * nt + j, 0))
    return pl.pallas_call(
        functools.partial(_inproj_body, bt=bt, tt=tt),
        out_shape=(
            jax.ShapeDtypeStruct((total, N_BIG), zdtype),
            jax.ShapeDtypeStruct((total, N_SMALL), f32),
            jax.ShapeDtypeStruct((nb * nt, SUBLANES, PROJ_COLS), f32),
        ),
        grid=(nb, nt),
        in_specs=[
            row_spec(D_MODEL), _mod_spec(bt, 0, row0), _mod_spec(bt, 1, row0),
            _const_spec((1, D_MODEL)), _const_spec((D_MODEL, N_BIG)),
            _const_spec((D_MODEL, N_SMALL)), _const_spec((D_MODEL, N_SMALL)),
        ],
        out_specs=(
            row_spec(N_BIG), row_spec(N_SMALL),
            pl.BlockSpec((1, SUBLANES, PROJ_COLS), lambda i, j: (i * nt + j, 0, 0)),
        ),
        compiler_params=pltpu.CompilerParams(
            dimension_semantics=("arbitrary", "arbitrary"), vmem_limit_bytes=VMEM_LIMIT_BYTES),
        name="in_proj",
    )(x2d, ada, ada, g_norm, w_big, ws_hi, ws_lo)


N_MIXER_CONSTS = 7
N_MIXER_INIT = 5
N_MIXER_OUTS = 5


def _mixer_scratch(rows):
    return [pltpu.VMEM((rows, QK), f32)] * 5 + [
        pltpu.VMEM((rows, N_SMALL), f32), pltpu.VMEM((rows, N_SMALL), f32), pltpu.VMEM((rows, QK), f32),
    ] + [pltpu.VMEM((rows, D_MODEL), f32)] * 4 + [
        pltpu.VMEM((SUBLANES, 2 * QK), f32), pltpu.VMEM((N_HEADS, DK, DVA), f32),
    ]


def _mixer_body(*refs, L, vlo, vhi, sample):
    zb_ref, zs_ref = refs[:2]
    k = 2
    init = None
    if sample:
        init = refs[k:k + N_MIXER_INIT]
        k += N_MIXER_INIT
    consts = refs[k:k + N_MIXER_CONSTS]
    k += N_MIXER_CONSTS
    outs = refs[k:k + N_MIXER_OUTS]
    _mixer_compute(zb_ref, zs_ref, init, consts, outs, refs[k + N_MIXER_OUTS:], L=L, vlo=vlo, vhi=vhi)


def _fused_body(x_ref, shift_ref, scale_ref, g_ref, wbig_ref, wsh_ref, wsl_ref, *refs, L):
    consts = refs[:N_MIXER_CONSTS]
    tail_ref = refs[N_MIXER_CONSTS]
    outs = refs[N_MIXER_CONSTS + 1:N_MIXER_CONSTS + 1 + N_MIXER_OUTS]
    k = N_MIXER_CONSTS + 1 + N_MIXER_OUTS
    zb_s = list(refs[k:k + N_BIG // PROJ_COLS])
    zs_s = refs[k + N_BIG // PROJ_COLS]
    scratch = refs[k + N_BIG // PROJ_COLS + 1:]
    _zero_carried_state(outs, scratch)
    narrow, wide = _inproj_parts(x_ref, shift_ref, scale_ref, g_ref, wbig_ref, wsh_ref, wsl_ref,
                                 zb_s, zs_s, tail_ref, bt=1, tt=L)

    order = (OFF_QKA, OFF_QB, OFF_GA, OFF_OA, OFF_GB, OFF_RB, OFF_VA, OFF_VB)
    pieces = [(off // PROJ_COLS, lo) for off in order for lo in range(0, PROJ_COLS, PIECE_COLS)]
    emitted = []

    def tick(n=1):
        if not emitted:
            narrow()
        for _ in range(n):
            if len(emitted) < len(pieces):
                idx, lo = pieces[len(emitted)]
                wide(idx, lo, PIECE_COLS)
                emitted.append((idx, lo))

    def need(off, width):
        wanted = [(off // PROJ_COLS, lo) for lo in range(0, PROJ_COLS, PIECE_COLS)
                  if lo < off % PROJ_COLS + width and lo + PIECE_COLS > off % PROJ_COLS]
        while any(w not in emitted for w in wanted):
            tick()

    _mixer_compute(zb_s, zs_s, None, consts, outs, scratch, L=L, vlo=0, vhi=L, zero_init=False,
                   hooks=(tick, need))


def _zero_carried_state(outs, scratch):
    _, _, _, mo_ref, So_ref = outs
    carry_s, caug_s = scratch[-2:]

    @pl.when(pl.program_id(1) == 0)
    def _zero():
        caug_s[...] = jnp.zeros(caug_s.shape, f32)
        mo_ref[...] = jnp.zeros(mo_ref.shape, f32)
        So_ref[...] = jnp.zeros(So_ref.shape, f32)
        carry_s[...] = jnp.zeros(carry_s.shape, f32)


def _mixer_compute(zb_ref, zs_ref, init, consts, outs, scratch, *, L, vlo, vhi, zero_init=True,
                   hooks=(lambda n=1: None, lambda off, width: None)):
    sample = init is not None
    if sample:
        conv0_ref, C0_ref, n0_ref, m0_ref, S0_ref = init
    cw_ref, cb_ref, gb_ref, wal_ref, bal_ref, gha_ref, ghb_ref = consts
    u_ref, Co_ref, no_ref, mo_ref, So_ref = outs
    qa_s, ka_s, qh_s, kh_s, kb_s, x_s, r_s, bg_s, ha_s, hb_s, pa_s, pb_s, carry_s, caug_s = scratch
    rows = zs_ref.shape[0]
    every = slice(None)

    tick, need = hooks

    def zcols(rsel, off, width):
        need(off, width)
        if isinstance(zb_ref, (list, tuple)):
            grp, o = divmod(off, PROJ_COLS)
            assert o + width <= PROJ_COLS
            return zb_ref[grp][rsel, o:o + width]
        return zb_ref[rsel, off:off + width]
    n_groups = rows // L
    log2_l = L.bit_length() - 1
    masked = not (vlo == 0 and vhi == L)
    scale = DK ** -0.5

    if not sample and zero_init:
        _zero_carried_state(outs, scratch)

    row_in = lax.broadcasted_iota(jnp.int32, (rows, 1), 0) & (L - 1)
    valid = (row_in >= vlo) & (row_in < vhi)

    cw = cw_ref[...]
    cb = cb_ref[...]
    for c0 in range(0, 2 * QK, PIECE_COLS):
        cs = slice(c0, c0 + PIECE_COLS)
        zq = zcols(every, OFF_QKA + c0, PIECE_COLS).astype(f32)
        if sample:
            zq = jnp.where(row_in < CONV_W - 1, conv0_ref[:, cs], zq)
            carry = jnp.zeros((SUBLANES, PIECE_COLS), f32)
        else:
            carry = carry_s[:, cs]
            carry_s[:, cs] = zq[rows - SUBLANES:rows, :]
        xe = jnp.concatenate([carry, zq], axis=0)
        conv = cb[:, cs] + zq * cw[CONV_W - 1:CONV_W, cs]
        for k in range(CONV_W - 1):
            o = SUBLANES - (CONV_W - 1) + k
            conv = conv + xe[o:o + rows] * cw[k:k + 1, cs]
        qk = conv * _sigmoid(conv)
        if c0 < QK:
            qa_s[:, cs] = qk * scale
        else:
            ka_s[:, c0 - QK:c0 - QK + PIECE_COLS] = qk
        tick()

    lane = lax.broadcasted_iota(jnp.int32, (1, N_SMALL), 1)
    zs = zs_ref[...]
    pre = zs + gb_ref[...]
    gate = jnp.where(lane < LANE_FG, pre, _log_sigmoid(pre))
    gate_f = jnp.where(valid, gate, 0.0) if masked else gate
    rr = lax.broadcasted_iota(jnp.int32, (rows, rows), 0)
    cc = lax.broadcasted_iota(jnp.int32, (rows, rows), 1)
    tril_mask = ((rr >> log2_l) == (cc >> log2_l)) & (cc <= rr)
    tril = jnp.where(tril_mask, 1.0, 0.0).astype(bf16)

    la_lin = _mm(zs.astype(bf16), wal_ref[...]) + bal_ref[...]
    tick(2)

    la_parts, kb_parts = [], []
    for h in range(N_HEADS):
        ks = slice(h * DK, (h + 1) * DK)
        la_h = _log_sigmoid(la_lin[:, ks]) * (1.0 / GLA_TAU)
        la_parts.append(jnp.where(valid, la_h, 0.0) if masked else la_h)
        tick()

    both = jnp.concatenate([gate_f] + la_parts, axis=1)
    wb = N_SMALL + QK
    t = _mm(tril, jnp.concatenate(_split3(both), axis=1))
    csum = t[:, :wb] + (t[:, wb:2 * wb] + t[:, 2 * wb:])
    tick(2)
    x_all = jnp.where(lane < LANE_FG, gate, csum[:, :N_SMALL])
    x_s[...] = x_all
    b_al = pltpu.roll(x_all, N_SMALL - N_HEADS, axis=1)
    r_all = x_all - b_al
    r_ok = (lane < N_HEADS) & valid if masked else jnp.broadcast_to(lane < N_HEADS, (rows, N_SMALL))
    r_s[...] = jnp.where(r_ok, r_all, NEG)
    range_m = jnp.max(jnp.where(r_ok, r_all, NEG)) - jnp.min(jnp.where(r_ok, r_all, -NEG))
    bg = csum[:, N_SMALL:]
    bg_s[...] = bg
    range_g = jnp.max(-bg)
    tick()
    for h in range(N_HEADS):
        ks = slice(h * DK, (h + 1) * DK)
        kb_h = zcols(every, OFF_KB + h * DK, DK).astype(f32)
        kb_parts.append(jnp.where(valid, kb_h, 0.0) if masked else kb_h)
        qh_s[:, ks] = zcols(every, OFF_QB + h * DK, DK).astype(f32) * scale * jnp.exp(bg[:, ks])
        kh_s[:, ks] = kb_parts[h] * jnp.exp(-bg[:, ks])
        tick()
    kb = jnp.concatenate(kb_parts, axis=1)
    fast_ok = jnp.logical_and(range_m <= SAFE_LOG_RANGE, range_g <= SAFE_LOG_RANGE)

    for c0 in range(0, D_MODEL, PIECE_COLS):
        cs = slice(c0, c0 + PIECE_COLS)
        col = lambda off: zcols(every, off + c0, PIECE_COLS).astype(f32)
        pa_s[:, cs] = _sigmoid(col(OFF_GA)) * _sigmoid(col(OFF_OA)) * gha_ref[:, cs]
        tick(2)
    for c0 in range(0, D_MODEL, PIECE_COLS):
        cs = slice(c0, c0 + PIECE_COLS)
        col = lambda off: zcols(every, off + c0, PIECE_COLS).astype(f32)
        rb = col(OFF_RB)
        pb_s[:, cs] = _sigmoid(col(OFF_GB)) * (rb * _sigmoid(rb)) * ghb_ref[:, cs]
        tick(2)
    need(OFF_VA, D_MODEL)
    need(OFF_VB, D_MODEL)

    tri = lax.broadcasted_iota(jnp.int32, (L, L), 0) >= lax.broadcasted_iota(jnp.int32, (L, L), 1)
    eye = lax.broadcasted_iota(jnp.int32, (DK, DK), 0) == lax.broadcasted_iota(jnp.int32, (DK, DK), 1)
    lane_h = lane
    lrow = lax.broadcasted_iota(jnp.int32, (1, L), 1)
    lcol = lax.broadcasted_iota(jnp.int32, (L, 1), 0)
    vrow = (lrow >= vlo) & (lrow < vhi)
    ones16 = jnp.ones((L, DK), bf16)

    def load_state(g, h):
        if sample:
            n_mat = jnp.broadcast_to(_row_to_col(n0_ref[g, pl.ds(h, 1), :], eye), (DK, DK))
            return jnp.concatenate([C0_ref[g, h], n_mat], axis=1), S0_ref[g, h]
        return caug_s[h], So_ref[0, h]

    def store_state(g, h, caug, s_new):
        sidx = g if sample else 0
        if not sample:
            caug_s[h] = caug
        Co_ref[sidx, h] = caug[:, :DV]
        no_ref[sidx, pl.ds(h, 1), :] = _diag_row(caug[:, DV:], eye)
        So_ref[sidx, h] = s_new

    def group_fast(g):
        r0 = g * L
        rs = pl.ds(r0, L)
        xc = x_s[rs, :]
        rc = r_s[rs, :]
        m_prev = m0_ref[g] if sample else mo_ref[0]
        bg_tail = bg_s[pl.ds(r0 + L - SUBLANES, SUBLANES), :]
        heads = []
        for h in range(N_HEADS):
            ks = slice(h * DK, (h + 1) * DK)
            caug, s_st = load_state(g, h)
            c0 = jnp.max(rc[:, h:h + 1], axis=0, keepdims=True)
            q16 = qa_s[rs, ks].astype(bf16)
            kt16 = (ka_s[rs, ks] * jnp.exp(rc[:, h:h + 1] - c0)).astype(bf16)
            qh16 = qh_s[rs, ks].astype(bf16)
            kh = kh_s[rs, ks]
            heads.append(dict(
                caug=caug, s_st=s_st, c0=c0, kt16=kt16, kh=kh,
                va=jnp.concatenate([zcols(rs, OFF_VA + h * DV, DV).astype(bf16), ones16], axis=1),
                vb16=zcols(rs, OFF_VB + h * DV, DV).astype(bf16),
                s_raw=_mm_nt(q16, kt16), a_raw=_mm_nt(qh16, kh.astype(bf16)),
                ci=_mm(q16, caug.astype(bf16)), oi=_mm(qh16, s_st.astype(bf16))))
        for p in heads:
            p["ni"] = _mm(jnp.where(tri, p["s_raw"], 0.0).astype(bf16), p["va"])
            p["o"] = _mm(jnp.where(tri, p["a_raw"], 0.0).astype(bf16), p["vb16"]) + p["oi"]
        for h, p in enumerate(heads):
            p["eb_last"] = jnp.exp(bg_tail[SUBLANES - 1:SUBLANES, h * DK:(h + 1) * DK])
            p["kv"] = _mm_tn(p["kt16"], p["va"])
            p["skv"] = _mm_tn((p["kh"] * p["eb_last"]).astype(bf16), p["vb16"])
        m_vec = m_prev
        for h, p in enumerate(heads):
            m_st = m_prev[:, h:h + 1]
            b_col = xc[:, LANE_FG + h:LANE_FG + h + 1]
            b_last = b_col[L - 1:L, :]
            c0 = p["c0"]
            mm = jnp.maximum(c0, m_st)
            comb = jnp.exp(c0 - mm) * p["ni"] + jnp.exp(m_st - mm) * p["ci"]
            inv = 1.0 / jnp.maximum(jnp.abs(comb[:, DV:]), jnp.exp(-(b_col + mm)))
            ha_s[rs, h * DV:h * DV + DK] = comb[:, :DK] * inv
            ha_s[rs, h * DV + DK:(h + 1) * DV] = comb[:, DK:DV] * inv
            hb_s[rs, h * DV:(h + 1) * DV] = p["o"]
            m_new = jnp.maximum(b_last + m_st, b_last + c0)
            caug_new = jnp.exp(b_last + m_st - m_new) * p["caug"] + jnp.exp(b_last + c0 - m_new) * p["kv"]
            s_new = _row_to_col(p["eb_last"], eye) * p["s_st"] + p["skv"]
            store_state(g, h, caug_new, s_new)
            m_vec = jnp.where(lane_h == h, m_new, m_vec)
        mo_ref[g if sample else 0] = m_vec

    def group_exact(g):
        static = isinstance(g, int)
        r0 = g * L if static else pl.multiple_of(g * L, L)
        tail0 = r0 + L - SUBLANES if static else pl.multiple_of(r0 + L - SUBLANES, SUBLANES)
        rs = pl.ds(r0, L)
        xc = x_s[rs, :]
        rc = r_s[rs, :]
        sr = lax.broadcasted_iota(jnp.int32, (2 * SUBLANES, N_SMALL), 0)
        sc = lax.broadcasted_iota(jnp.int32, (2 * SUBLANES, N_SMALL), 1)
        sel = jnp.where((sr < N_HEADS) & (sc == sr + LANE_IG), 1.0,
                        jnp.where((sr < N_HEADS) & (sc == sr + LANE_FG), -1.0, 0.0)).astype(bf16)
        xh, xm, xl = _split3(xc)
        r_rows = _mm_nt(sel, xh) + (_mm_nt(sel, xm) + _mm_nt(sel, xl))
        m_prev = m0_ref[g] if sample else mo_ref[0]
        m_vec = m_prev
        bg_tail = bg_s[pl.ds(tail0, SUBLANES), :]
        for h in range(N_HEADS):
            ks = slice(h * DK, (h + 1) * DK)
            vs = slice(h * DV, (h + 1) * DV)
            caug, s_st = load_state(g, h)
            m_st = m_prev[:, h:h + 1]
            b_col = xc[:, LANE_FG + h:LANE_FG + h + 1]
            r_col = rc[:, h:h + 1]
            b_last = b_col[L - 1:L, :]
            v16 = zcols(rs, OFF_VA + h * DV, DV).astype(bf16)
            vb16 = zcols(rs, OFF_VB + h * DV, DV).astype(bf16)
            eb_last = jnp.exp(bg_tail[SUBLANES - 1:SUBLANES, ks])
            eb_col = _row_to_col(eb_last, eye)
            c_st = caug[:, :DV]
            n_row = _diag_row(caug[:, DV:], eye)
            r_row = r_rows[h:h + 1, :]
            if masked:
                r_row = jnp.where(vrow, r_row, NEG)
            dm = jnp.where(tri, b_col + r_row, NEG)
            m_t = jnp.maximum(jnp.max(dm, axis=1, keepdims=True), b_col + m_st)
            w = jnp.exp(dm - m_t)
            wi = jnp.exp(b_col + m_st - m_t)
            q = qa_s[rs, ks]
            k = ka_s[rs, ks]
            q16 = q.astype(bf16)
            s = _mm_nt(q16, k.astype(bf16)) * w
            num = _mm(s.astype(bf16), v16) + wi * _mm(q16, c_st.astype(bf16))
            den = jnp.sum(s, axis=1, keepdims=True) + wi * jnp.sum(q * n_row, axis=1, keepdims=True)
            ha_s[rs, vs] = num / jnp.maximum(jnp.abs(den), jnp.exp(-m_t))
            m_new = jnp.maximum(b_last + m_st, b_last + jnp.max(r_row, axis=1, keepdims=True))
            wc = jnp.exp(b_last + m_st - m_new)
            kt = k * jnp.exp(b_last + r_col - m_new)
            n_new = wc * n_row + jnp.sum(kt, axis=0, keepdims=True)
            caug_new = jnp.concatenate(
                [wc * c_st + _mm_tn(kt.astype(bf16), v16),
                 jnp.broadcast_to(_row_to_col(n_new, eye), (DK, DK))], axis=1)
            qb = zcols(rs, OFF_QB + h * DK, DK).astype(f32) * scale
            bgc = bg_s[rs, ks]
            sub = lax.broadcasted_iota(jnp.int32, (SUBLANES, 1), 0)

            def score_col(s_i, a, ks=ks, qb=qb, bgc=bgc, sub=sub):
                s8 = lax.shift_left(lax.shift_right_logical(s_i, 3), 3)
                base = pl.multiple_of(r0 + s8, SUBLANES)
                pick = sub == (s_i - s8)
                b_s = jnp.sum(jnp.where(pick, bg_s[pl.ds(base, SUBLANES), ks], 0.0), axis=0, keepdims=True)
                k_s = jnp.sum(jnp.where(pick, kb_s[pl.ds(base, SUBLANES), ks], 0.0), axis=0, keepdims=True)
                e = jnp.exp(jnp.where(lcol >= s_i, bgc - b_s, NEG))
                col = jnp.sum(qb * e * k_s, axis=1, keepdims=True)
                return jnp.where(lrow == s_i, col, a)

            a = lax.fori_loop(0, L, score_col, jnp.zeros((L, L), f32))
            hb_s[rs, vs] = (_mm(a.astype(bf16), vb16)
                            + _mm((qb * jnp.exp(bgc)).astype(bf16), s_st.astype(bf16)))
            kt_b = kb_s[rs, ks] * jnp.exp(bg_tail[SUBLANES - 1:SUBLANES, ks] - bgc)
            s_new = eb_col * s_st + _mm_tn(kt_b.astype(bf16), vb16)
            store_state(g, h, caug_new, s_new)
            m_vec = jnp.where(lane_h == h, m_new, m_vec)
        mo_ref[g if sample else 0] = m_vec

    def groups_fast_batched():
        r_all_m = r_s[...]
        c0_l, m_l, mm_l, fac, wc = [], [], [], [], []
        for g in range(n_groups):
            gs = slice(g * L, (g + 1) * L)
            c0_g = jnp.max(r_all_m[gs], axis=0, keepdims=True)
            m_g = m0_ref[g]
            b_last = b_al[gs][L - 1:L, :]
            m_new = jnp.maximum(b_last + m_g, b_last + c0_g)
            mo_ref[g] = m_new
            c0_l.append(c0_g)
            m_l.append(m_g)
            mm_l.append(jnp.maximum(c0_g, m_g))
            fac.append(jnp.exp(b_last + c0_g - m_new))
            wc.append(jnp.exp(b_last + m_g - m_new))
        rows_of = lambda lst: jnp.concatenate([jnp.broadcast_to(v, (L, N_SMALL)) for v in lst], axis=0)
        c0_r, m_r, mm_r = rows_of(c0_l), rows_of(m_l), rows_of(mm_l)
        beta = jnp.exp(r_all_m - c0_r)
        gi_r = jnp.exp(c0_r - mm_r)
        wi_r = jnp.exp(m_r - mm_r)
        floor = jnp.exp(-(b_al + mm_r))
        ones_g = jnp.ones((L, DK), bf16)
        for h in range(N_HEADS):
            ks = slice(h * DK, (h + 1) * DK)
            vs = slice(h * DV, (h + 1) * DV)
            kt = ka_s[:, ks] * beta[:, h:h + 1]
            v = zcols(every, OFF_VA + h * DV, DV)
            vb = zcols(every, OFF_VB + h * DV, DV)
            q16 = qa_s[:, ks].astype(bf16)
            va = jnp.concatenate([v.astype(bf16), jnp.ones((rows, DK), bf16)], axis=1)
            s16 = jnp.where(tril_mask, _mm_nt(q16, kt.astype(bf16)), 0.0).astype(bf16)
            ni = _mm(s16, va)
            qh16 = qh_s[:, ks].astype(bf16)
            kh = kh_s[:, ks]
            a16 = jnp.where(tril_mask, _mm_nt(qh16, kh.astype(bf16)), 0.0).astype(bf16)
            o_intra = _mm(a16, vb.astype(bf16))
            ci_parts, oi_parts = [], []
            for g in range(n_groups):
                gs = slice(g * L, (g + 1) * L)
                caug, s_st = load_state(g, h)
                va_g = jnp.concatenate([v[gs].astype(bf16), ones_g], axis=1)
                ci_parts.append(_mm(qa_s[gs, ks].astype(bf16), caug.astype(bf16)))
                caug_new = wc[g][:, h:h + 1] * caug + fac[g][:, h:h + 1] * _mm_tn(kt[gs].astype(bf16), va_g)
                oi_parts.append(_mm(qh_s[gs, ks].astype(bf16), s_st.astype(bf16)))
                eb_last = jnp.exp(bg_s[gs, ks][L - 1:L, :])
                s_new = (_row_to_col(eb_last, eye) * s_st
                         + _mm_tn((kh[gs] * eb_last).astype(bf16), vb[gs].astype(bf16)))
                store_state(g, h, caug_new, s_new)
            comb = gi_r[:, h:h + 1] * ni + wi_r[:, h:h + 1] * jnp.concatenate(ci_parts, axis=0)
            inv = 1.0 / jnp.maximum(jnp.abs(comb[:, DV:]), floor[:, h:h + 1])
            ha_s[:, h * DV:h * DV + DK] = comb[:, :DK] * inv
            ha_s[:, h * DV + DK:(h + 1) * DV] = comb[:, DK:DV] * inv
            hb_s[:, vs] = o_intra + jnp.concatenate(oi_parts, axis=0)

    def run_groups(exact):
        if not exact:
            if sample and L == SUBLANES:
                groups_fast_batched()
            else:
                for g in range(n_groups):
                    group_fast(g)
        elif n_groups == 1:
            group_exact(0)
        else:
            def body(g, carry_):
                group_exact(g)
                return carry_
            lax.fori_loop(0, n_groups, body, 0)

    @pl.when(fast_ok)
    def _fast():
        run_groups(False)

    @pl.when(jnp.logical_not(fast_ok))
    def _exact():
        kb_s[...] = kb
        run_groups(True)

    for h in range(N_HEADS):
        vs = slice(h * DV, (h + 1) * DV)
        u = pa_s[:, vs] * _rms(ha_s[:, vs]) + pb_s[:, vs] * _rms(hb_s[:, vs])
        u_ref[:, vs] = u.astype(u_ref.dtype)


def _mixer_call(zb, zs, init, consts, *, nb, nt, bt, L, vlo, vhi):
    sample = init is not None
    rows = bt * L
    n_state = nb * bt
    row_spec = lambda n: pl.BlockSpec((rows, n), lambda i, j: (i * nt + j, 0))
    st4 = pl.BlockSpec((bt, N_HEADS, DK, DV), lambda i, j: (i, 0, 0, 0))
    st_n = pl.BlockSpec((bt, N_HEADS, DK), lambda i, j: (i, 0, 0))
    st_m = pl.BlockSpec((bt, 1, N_SMALL), lambda i, j: (i, 0, 0))
    in_specs = [row_spec(N_BIG), row_spec(N_SMALL)]
    args = [zb, zs]
    if sample:
        in_specs += [row_spec(2 * QK), st4, st_n, st_m, st4]
        args += list(init)
    in_specs += [_const_spec(c.shape) for c in consts]
    args += list(consts)
    return pl.pallas_call(
        functools.partial(_mixer_body, L=L, vlo=vlo, vhi=vhi, sample=sample),
        out_shape=(
            jax.ShapeDtypeStruct((zb.shape[0], D_MODEL), bf16),
            jax.ShapeDtypeStruct((n_state, N_HEADS, DK, DV), f32),
            jax.ShapeDtypeStruct((n_state, N_HEADS, DK), f32),
            jax.ShapeDtypeStruct((n_state, 1, N_SMALL), f32),
            jax.ShapeDtypeStruct((n_state, N_HEADS, DK, DV), f32),
        ),
        grid=(nb, nt),
        in_specs=in_specs,
        out_specs=(row_spec(D_MODEL), st4, st_n, st_m, st4),
        scratch_shapes=_mixer_scratch(rows),
        compiler_params=pltpu.CompilerParams(
            dimension_semantics=("arbitrary", "arbitrary"), vmem_limit_bytes=VMEM_LIMIT_BYTES),
        name="mixer_sample" if sample else "mixer_prompt",
    )(*args)


def _fused_call(x2d, ada4, g_norm, w_big, ws_hi, ws_lo, consts, *, nb, nt, L):
    ada, row0 = ada4
    row_spec = lambda n: pl.BlockSpec((L, n), lambda i, j: (i * nt + j, 0))
    st4 = pl.BlockSpec((1, N_HEADS, DK, DV), lambda i, j: (i, 0, 0, 0))
    st_n = pl.BlockSpec((1, N_HEADS, DK), lambda i, j: (i, 0, 0))
    st_m = pl.BlockSpec((1, 1, N_SMALL), lambda i, j: (i, 0, 0))
    return pl.pallas_call(
        functools.partial(_fused_body, L=L),
        out_shape=(
            jax.ShapeDtypeStruct((nb * nt, SUBLANES, PROJ_COLS), f32),
            jax.ShapeDtypeStruct((x2d.shape[0], D_MODEL), bf16),
            jax.ShapeDtypeStruct((nb, N_HEADS, DK, DV), f32),
            jax.ShapeDtypeStruct((nb, N_HEADS, DK), f32),
            jax.ShapeDtypeStruct((nb, 1, N_SMALL), f32),
            jax.ShapeDtypeStruct((nb, N_HEADS, DK, DV), f32),
        ),
        grid=(nb, nt),
        in_specs=[
            row_spec(D_MODEL), _mod_spec(1, 0, row0), _mod_spec(1, 1, row0),
            _const_spec((1, D_MODEL)), _const_spec((D_MODEL, N_BIG)),
            _const_spec((D_MODEL, N_SMALL)), _const_spec((D_MODEL, N_SMALL)),
        ] + [_const_spec(c.shape) for c in consts],
        out_specs=(
            pl.BlockSpec((1, SUBLANES, PROJ_COLS), lambda i, j: (i * nt + j, 0, 0)),
            row_spec(D_MODEL), st4, st_n, st_m, st4,
        ),
        scratch_shapes=([pltpu.VMEM((L, PROJ_COLS), f32)] * (N_BIG // PROJ_COLS)
                        + [pltpu.VMEM((L, N_SMALL), f32)] + _mixer_scratch(L)),
        compiler_params=pltpu.CompilerParams(
            dimension_semantics=("arbitrary", "arbitrary"), vmem_limit_bytes=VMEM_LIMIT_BYTES),
        name="proj_mixer_prompt",
    )(x2d, ada, ada, g_norm, w_big, ws_hi, ws_lo, *consts)


def _out_body(u_ref, x_ref, gate1_ref, shift2_ref, scale2_ref, gate2_ref, gn2_ref, gfin_ref,
              wout_ref, wg_ref, wu_ref, wd_ref, y_ref, *, bt, tt):
    x1 = x_ref[...] + _mod_rows(gate1_ref, bt, tt) * _mm(u_ref[...], wout_ref[...])
    h2 = _rms(x1) * gn2_ref[...]
    h2 = (h2 * (1.0 + _mod_rows(scale2_ref, bt, tt)) + _mod_rows(shift2_ref, bt, tt)).astype(bf16)
    acc = jnp.zeros(x1.shape, f32)
    for c in range(0, D_FF, FFN_COLS):
        g = _mm(h2, wg_ref[:, c:c + FFN_COLS])
        up = _mm(h2, wu_ref[:, c:c + FFN_COLS])
        acc = acc + _mm((g * _sigmoid(g) * up).astype(bf16), wd_ref[c:c + FFN_COLS, :])
    x2 = x1 + _mod_rows(gate2_ref, bt, tt) * acc
    y_ref[...] = _rms(x2) * gfin_ref[...]


def _out_call(u2d, x2d, ada4, g_norm2, g_final, w_out, w_gate, w_up, w_down, *, nb, nt, bt, tt):
    ada, row0 = ada4
    rows = bt * tt
    row_spec = pl.BlockSpec((rows, D_MODEL), lambda i, j: (i * nt + j, 0))
    return pl.pallas_call(
        functools.partial(_out_body, bt=bt, tt=tt),
        out_shape=jax.ShapeDtypeStruct(x2d.shape, f32),
        grid=(nb, nt),
        in_specs=[
            row_spec, row_spec,
            _mod_spec(bt, 2, row0), _mod_spec(bt, 3, row0), _mod_spec(bt, 4, row0), _mod_spec(bt, 5, row0),
            _const_spec((1, D_MODEL)), _const_spec((1, D_MODEL)),
            _const_spec((D_MODEL, D_MODEL)), _const_spec((D_MODEL, D_FF)),
            _const_spec((D_MODEL, D_FF)), _const_spec((D_FF, D_MODEL)),
        ],
        out_specs=row_spec,
        compiler_params=pltpu.CompilerParams(
            dimension_semantics=("arbitrary", "arbitrary"), vmem_limit_bytes=VMEM_LIMIT_BYTES),
        name="out_ffn",
    )(u2d, x2d, ada, ada, ada, ada, g_norm2, g_final, w_out, w_gate, w_up, w_down)


PROMPT_CHUNK = 256
PROMPT_FFN_ROWS = 512
SAMPLE_L = SUBLANES
SAMPLE_MIX_GROUPS = 8
SAMPLE_DENSE_GROUPS = 32


W_IN_SEG_A = (0, 3072)
W_IN_GATES = 3072
W_IN_SEG_B = (3080, 6152)
W_IN_LR = 6152
W_IN_SEG_C = (6168, 8216)
LANES = 128


PREP_COLS = 512
SHIFT_B = W_IN_SEG_B[0] - OFF_QB
SHIFT_C = W_IN_SEG_C[0] - OFF_GA
PREP_HALO = 32


def _prep_body(a_ref, b_ref, gt_ref, lt_ref, wb_ref, wh_ref, wl_ref):
    j = pl.program_id(0)
    both = jnp.concatenate([a_ref[...], b_ref[0:PREP_HALO, :]], axis=0)

    def emit(shift):
        wb_ref[...] = both[shift:shift + PREP_COLS, :].T.astype(bf16)

    pl.when(j < OFF_QB // PREP_COLS)(lambda: emit(0))
    pl.when((j >= OFF_QB // PREP_COLS) & (j < OFF_GA // PREP_COLS))(lambda: emit(SHIFT_B))
    pl.when(j >= OFF_GA // PREP_COLS)(lambda: emit(SHIFT_C))

    @pl.when(j == 0)
    def _narrow():
        row = lax.broadcasted_iota(jnp.int32, (LANES, 1), 0)
        small = jnp.where(row < LANE_LR, gt_ref[...], jnp.where(row < LANE_LR + GLA_RANK, lt_ref[...], 0.0)).T
        hi = small.astype(bf16)
        wh_ref[...] = hi
        wl_ref[...] = (small - hi.astype(f32)).astype(bf16)


def _prep_call(w_in):
    assert W_IN_GATES % LANES == 0 and (W_IN_LR - LANE_LR) % LANES == 0
    assert max(SHIFT_B, SHIFT_C) <= PREP_HALO and SHIFT_B % SUBLANES == 0 and SHIFT_C % SUBLANES == 0
    wt = jnp.transpose(w_in)
    last = (wt.shape[0] - 1) // PREP_COLS
    const = lambda shape: pl.BlockSpec(shape, lambda j: (0, 0))
    return pl.pallas_call(
        _prep_body,
        out_shape=(
            jax.ShapeDtypeStruct((D_MODEL, N_BIG), bf16),
            jax.ShapeDtypeStruct((D_MODEL, N_SMALL), bf16),
            jax.ShapeDtypeStruct((D_MODEL, N_SMALL), bf16),
        ),
        grid=(N_BIG // PREP_COLS,),
        in_specs=[
            pl.BlockSpec((PREP_COLS, D_MODEL), lambda j: (j, 0)),
            pl.BlockSpec((PREP_COLS, D_MODEL), lambda j: (jnp.minimum(j + 1, last), 0)),
            pl.BlockSpec((LANES, D_MODEL), lambda j: (W_IN_GATES // LANES, 0)),
            pl.BlockSpec((LANES, D_MODEL), lambda j: ((W_IN_LR - LANE_LR) // LANES, 0)),
        ],
        out_specs=(
            pl.BlockSpec((D_MODEL, PREP_COLS), lambda j: (0, j)),
            const((D_MODEL, N_SMALL)), const((D_MODEL, N_SMALL)),
        ),
        compiler_params=pltpu.CompilerParams(
            dimension_semantics=("arbitrary",), vmem_limit_bytes=VMEM_LIMIT_BYTES),
        name="w_in_prep",
    )(wt, wt, wt, wt)


def _prep_layer_weights(w_in, w_alpha, b_igate, b_fgate):
    w_big, ws_hi, ws_lo = _prep_call(w_in)
    w_al = jnp.zeros((N_SMALL, QK), f32).at[LANE_LR:LANE_LR + GLA_RANK].set(w_alpha).astype(bf16)
    g_bias = jnp.concatenate([b_igate, b_fgate, jnp.zeros((N_SMALL - 2 * N_HEADS,), f32)])[None, :]
    return w_big, ws_hi, ws_lo, w_al, g_bias


def _layer(xp2d, xs2d, st_s, ada_p, ada_s, p, *, bp, tp, bs):
    (g_norm1, g_norm2, w_in, conv_w, conv_b, b_igate, b_fgate, g_head_a,
     w_alpha, b_alpha, g_head_b, w_out, w_ffn_gate, w_ffn_up, w_ffn_down, g_final_or_ones) = p
    w_big, ws_hi, ws_lo, w_al, g_bias = _prep_layer_weights(w_in, w_alpha, b_igate, b_fgate)
    consts = (conv_w, conv_b[None, :], g_bias, w_al, b_alpha[None, :],
              g_head_a.reshape(1, D_MODEL), g_head_b.reshape(1, D_MODEL))
    w_out16, w_g16, w_u16, w_d16 = (w.astype(bf16) for w in (w_out, w_ffn_gate, w_ffn_up, w_ffn_down))
    gn1, gn2, gfin = g_norm1[None, :], g_norm2[None, :], g_final_or_ones[None, :]

    ntp = tp // PROMPT_CHUNK
    tail, u, c_p, n_p, m_p, s_p = _fused_call(xp2d, ada_p, gn1, w_big, ws_hi, ws_lo, consts,
                                              nb=bp, nt=ntp, L=PROMPT_CHUNK)
    conv_p = tail.reshape(bp, ntp, SUBLANES, 2 * QK)[:, ntp - 1, SUBLANES - (CONV_W - 1):, :]
    yp = _out_call(u, xp2d, ada_p, gn2, gfin, w_out16, w_g16, w_u16, w_d16,
                   nb=bp, nt=tp // PROMPT_FFN_ROWS, bt=1, tt=PROMPT_FFN_ROWS)

    conv0, c0, n0, m0, s0 = st_s
    nbd = bs // SAMPLE_DENSE_GROUPS
    zb_s, zs_s, _ = _inproj_call(xs2d, ada_s, gn1, w_big, ws_hi, ws_lo,
                                 nb=nbd, nt=1, bt=SAMPLE_DENSE_GROUPS, tt=SAMPLE_L, zdtype=f32)
    lo = CONV_W - 1
    n_tok = SAMPLE_L - lo - 1
    conv_s = zb_s[:, OFF_QKA:OFF_QKA + 2 * QK].reshape(bs, SAMPLE_L, 2 * QK)[:, n_tok:lo + n_tok, :]
    conv0_rows = jnp.pad(conv0, ((0, 0), (0, SAMPLE_L - lo), (0, 0))).reshape(bs * SAMPLE_L, 2 * QK)
    m0_lanes = jnp.pad(m0[:, None, :], ((0, 0), (0, 0), (0, N_SMALL - N_HEADS)))
    u_s, c_s, n_s, m_s, s_s = _mixer_call(
        zb_s, zs_s, (conv0_rows, c0, n0, m0_lanes, s0), consts,
        nb=bs // SAMPLE_MIX_GROUPS, nt=1, bt=SAMPLE_MIX_GROUPS, L=SAMPLE_L, vlo=lo, vhi=lo + n_tok)
    ys = _out_call(u_s, xs2d, ada_s, gn2, gfin, w_out16, w_g16, w_u16, w_d16,
                   nb=nbd, nt=1, bt=SAMPLE_DENSE_GROUPS, tt=SAMPLE_L)
    st_p = (conv_p, c_p, n_p, m_p[:, 0, :N_HEADS], s_p)
    st_s_new = (conv_s, c_s, n_s, m_s[:, 0, :N_HEADS], s_s)
    return yp, ys, st_p, st_s_new


def kernel(x_prompt, x_sample, state_conv, state_C, state_n, state_m, state_S, c_prompt, c_sample,
           g_norm1, g_norm2, w_ada, b_ada, w_in, conv_w, conv_b, b_igate, b_fgate, g_head_a,
           w_alpha, b_alpha, g_head_b, w_out, w_ffn_gate, w_ffn_up, w_ffn_down, g_final):
    bp, tp, _ = x_prompt.shape
    bs, ts, _ = x_sample.shape
    depth = w_in.shape[0]
    assert depth == 1, "the final norm is fused into the (single) layer's FFN kernel"
    lo = CONV_W - 1
    assert ts == SAMPLE_L - lo - 1
    xp2d = x_prompt.reshape(bp * tp, D_MODEL)
    xs2d = jnp.pad(x_sample, ((0, 0), (lo, SAMPLE_L - lo - ts), (0, 0))).reshape(bs * SAMPLE_L, D_MODEL)
    c_all = jnp.concatenate([c_sample, c_prompt], axis=0)
    new_p = [[] for _ in range(5)]
    new_s = [[] for _ in range(5)]
    for l in range(depth):
        ada = _ada_call(c_all, w_ada[l], b_ada[l][None, :])
        p = (g_norm1[l], g_norm2[l], w_in[l], conv_w[l], conv_b[l], b_igate[l], b_fgate[l], g_head_a[l],
             w_alpha[l], b_alpha[l], g_head_b[l], w_out[l], w_ffn_gate[l], w_ffn_up[l], w_ffn_down[l], g_final)
        st_s = (state_conv[l], state_C[l], state_n[l], state_m[l], state_S[l])
        xp2d, xs2d, st_p, st_s_new = _layer(xp2d, xs2d, st_s, (ada, bs), (ada, 0), p, bp=bp, tp=tp, bs=bs)
        for j in range(5):
            new_p[j].append(st_p[j])
            new_s[j].append(st_s_new[j])
    y_prompt = xp2d.reshape(bp, tp, D_MODEL)
    y_sample = xs2d.reshape(bs, SAMPLE_L, D_MODEL)[:, lo:lo + ts, :]
    outs_p = [jnp.stack(a, axis=0) for a in new_p]
    outs_s = [jnp.stack(a, axis=0) for a in new_s]
    return (y_prompt, y_sample, *outs_p, *outs_s)
```

```python
import functools

import jax
import jax.numpy as jnp
from jax import lax
from jax.experimental import pallas as pl
from jax.experimental.pallas import tpu as pltpu

f32 = jnp.float32
bf16 = jnp.bfloat16

D_MODEL = 1024
N_HEADS = 4
DK = 128
DV = 256
QK = N_HEADS * DK
CONV_W = 4
GLA_RANK = 16
GLA_TAU = 16.0
D_FF = 2816
EPS = 1e-6
N_ADA = 6

N_BIG = 8192
N_SMALL = 128
OFF_QKA, OFF_VA, OFF_OA, OFF_QB, OFF_KB, OFF_VB, OFF_RB, OFF_GA, OFF_GB = (
    0, 1024, 2048, 3072, 3584, 4096, 5120, 6144, 7168)
LANE_IG, LANE_FG, LANE_LR = 0, N_HEADS, 2 * N_HEADS

SUBLANES = 8
NEG = -1e30
SAFE_LOG_RANGE = 60.0
V7X_VMEM_BYTES = 64 * 1024 * 1024
VMEM_LIMIT_BYTES = V7X_VMEM_BYTES - 8 * 1024 * 1024
FFN_COLS = 256
PROJ_COLS = 1024
PIECE_COLS = 256
DVA = DV + DK


def _mm(a, b):
    return jnp.dot(a, b, preferred_element_type=f32)


def _mm_nt(a, b):
    return lax.dot_general(a, b, (((1,), (1,)), ((), ())), preferred_element_type=f32)


def _mm_tn(a, b):
    return lax.dot_general(a, b, (((0,), (0,)), ((), ())), preferred_element_type=f32)


def _split3(x):
    hi = x.astype(bf16)
    r = x - hi.astype(f32)
    mid = r.astype(bf16)
    lo = (r - mid.astype(f32)).astype(bf16)
    return hi, mid, lo


def _sigmoid(x):
    return 0.5 * jnp.tanh(0.5 * x) + 0.5


def _log_sigmoid(x):
    return jnp.minimum(x, 0.0) - jnp.log(1.0 + jnp.exp(-jnp.abs(x)))


def _rms(x):
    return x * lax.rsqrt(jnp.mean(x * x, axis=-1, keepdims=True) + EPS)


def _row_to_col(row, eye):
    return jnp.sum(jnp.where(eye, row, 0.0), axis=1, keepdims=True)


def _diag_row(mat, eye):
    return jnp.sum(jnp.where(eye, mat, 0.0), axis=0, keepdims=True)


def _mod_rows(mod_ref, bt, tt):
    if bt == 1:
        return mod_ref[0, 0]
    if tt % SUBLANES == 0:
        v = mod_ref[...].reshape(bt, 1, D_MODEL)
        return jnp.broadcast_to(v, (bt, tt, D_MODEL)).reshape(bt * tt, D_MODEL)
    assert 2 * tt == SUBLANES and bt % 2 == 0
    first = lax.broadcasted_iota(jnp.int32, (SUBLANES, 1), 0) < tt
    return jnp.concatenate([jnp.where(first, mod_ref[2 * p, 0], mod_ref[2 * p + 1, 0]) for p in range(bt // 2)],
                           axis=0)


def _spread_tokens(c, lo, n_tok):
    assert 2 * n_tok == SUBLANES
    row = lax.broadcasted_iota(jnp.int32, (SUBLANES, 1), 0)
    mid = (row >= lo) & (row < lo + n_tok)
    tiles = []
    for p in range(c.shape[0] // SUBLANES):
        t = c[p * SUBLANES:(p + 1) * SUBLANES]
        tiles.append(jnp.where(mid, pltpu.roll(t, lo, axis=0), 0.0))
        tiles.append(jnp.where(mid, pltpu.roll(t, (lo + n_tok) % SUBLANES, axis=0), 0.0))
    return jnp.concatenate(tiles, axis=0)


def _gather_tokens(x, lo, n_tok):
    assert 2 * n_tok == SUBLANES
    first = lax.broadcasted_iota(jnp.int32, (SUBLANES, 1), 0) < n_tok
    tiles = []
    for p in range(x.shape[0] // (2 * SUBLANES)):
        even = x[2 * p * SUBLANES:(2 * p + 1) * SUBLANES]
        odd = x[(2 * p + 1) * SUBLANES:(2 * p + 2) * SUBLANES]
        tiles.append(jnp.where(first, pltpu.roll(even, SUBLANES - lo, axis=0),
                               pltpu.roll(odd, (SUBLANES - lo + n_tok) % SUBLANES, axis=0)))
    return jnp.concatenate(tiles, axis=0)


def _ada_body(c_ref, w_ref, b_ref, o_ref):
    ch, cl, _ = _split3(c_ref[...])
    wh, wl, _ = _split3(w_ref[...])
    o_ref[:, 0, 0, :] = _mm(ch, wh) + (_mm(cl, wh) + _mm(ch, wl)) + b_ref[...]


def _ada_call(c_all, w_ada, b_ada):
    m = c_all.shape[0]
    n = w_ada.shape[1]
    return pl.pallas_call(
        _ada_body,
        out_shape=jax.ShapeDtypeStruct((m, n // D_MODEL, 1, D_MODEL), f32),
        grid=(n // D_MODEL,),
        in_specs=[
            pl.BlockSpec((m, D_MODEL), lambda i: (0, 0)),
            pl.BlockSpec((D_MODEL, D_MODEL), lambda i: (0, i)),
            pl.BlockSpec((1, D_MODEL), lambda i: (0, i)),
        ],
        out_specs=pl.BlockSpec((m, 1, 1, D_MODEL), lambda i: (0, i, 0, 0)),
        compiler_params=pltpu.CompilerParams(
            dimension_semantics=("arbitrary",), vmem_limit_bytes=VMEM_LIMIT_BYTES),
        name="ada_proj",
    )(c_all, w_ada, b_ada)


def _inproj_parts(x_ref, shift_ref, scale_ref, g_ref, wbig_ref, wsh_ref, wsl_ref,
                  zb_ref, zs_ref, tail_ref, *, bt, tt):
    rows = bt * tt
    h = _rms(x_ref[...]) * g_ref[...]
    h = h * (1.0 + _mod_rows(scale_ref, bt, tt)) + _mod_rows(shift_ref, bt, tt)
    hh = h.astype(bf16)
    hl = (h - hh.astype(f32)).astype(bf16)

    def narrow():
        wsh = wsh_ref[...]
        rhs = jnp.concatenate([jnp.concatenate([wsh, wsl_ref[...]], axis=1),
                               jnp.concatenate([wsh, jnp.zeros_like(wsh)], axis=1)], axis=0)
        t = _mm(jnp.concatenate([hh, hl], axis=1), rhs)
        zs_ref[...] = t[:, :N_SMALL] + t[:, N_SMALL:]

    def wide(idx, lo=0, width=PROJ_COLS):
        c = idx * PROJ_COLS + lo
        z = _mm(hh, wbig_ref[:, c:c + width])
        if isinstance(zb_ref, (list, tuple)):
            zb_ref[idx][:, lo:lo + width] = z
        else:
            zb_ref[:, c:c + width] = z.astype(zb_ref.dtype)
        if idx == OFF_QKA // PROJ_COLS:
            tail_ref[0, :, lo:lo + width] = z[rows - SUBLANES:rows, :]

    return narrow, wide


def _inproj_body(*refs, bt, tt):
    narrow, wide = _inproj_parts(*refs, bt=bt, tt=tt)
    narrow()
    for idx in range(N_BIG // PROJ_COLS):
        wide(idx)


def _const_spec(shape):
    nd = len(shape)
    return pl.BlockSpec(shape, lambda i, j: (0,) * nd, pipeline_mode=pl.Buffered(1))


def _mod_spec(bt, k, row0):
    assert row0 % bt == 0
    return pl.BlockSpec((bt, 1, 1, D_MODEL), lambda i, j: (row0 // bt + i, k, 0, 0))


def _inproj_call(x2d, ada4, g_norm, w_big, ws_hi, ws_lo, *, nb, nt, bt, tt, zdtype):
    ada, row0 = ada4
    rows = bt * tt
    total = x2d.shape[0]
    row_spec = lambda n: pl.BlockSpec((rows, n), lambda i, j: (i * nt + j, 0))
    return pl.pallas_call(
        functools.partial(_inproj_body, bt=bt, tt=tt),
        out_shape=(
            jax.ShapeDtypeStruct((total, N_BIG), zdtype),
            jax.ShapeDtypeStruct((total, N_SMALL), f32),
            jax.ShapeDtypeStruct((nb * nt, SUBLANES, PROJ_COLS), f32),
        ),
        grid=(nb, nt),
        in_specs=[
            row_spec(D_MODEL), _mod_spec(bt, 0, row0), _mod_spec(bt, 1, row0),
            _const_spec((1, D_MODEL)), _const_spec((D_MODEL, N_BIG)),
            _const_spec((D_MODEL, N_SMALL)), _const_spec((D_MODEL, N_SMALL)),
        ],
        out_specs=(
            row_spec(N_BIG), row_spec(N_SMALL),
            pl.BlockSpec((1, SUBLANES, PROJ_COLS), lambda i, j: (i * nt + j, 0, 0)),
        ),
        compiler_params=pltpu.CompilerParams(
            dimension_semantics=("arbitrary", "arbitrary"), vmem_limit_bytes=VMEM_LIMIT_BYTES),
        name="in_proj",
    )(x2d, ada, ada, g_norm, w_big, ws_hi, ws_lo)


N_MIXER_CONSTS = 7
N_MIXER_INIT = 5
N_MIXER_OUTS = 5


def _mixer_scratch(rows):
    return [pltpu.VMEM((rows, QK), f32)] * 5 + [
        pltpu.VMEM((rows, N_SMALL), f32), pltpu.VMEM((rows, N_SMALL), f32), pltpu.VMEM((rows, QK), f32),
    ] + [pltpu.VMEM((rows, D_MODEL), f32)] * 4 + [
        pltpu.VMEM((SUBLANES, 2 * QK), f32), pltpu.VMEM((N_HEADS, DK, DVA), f32),
    ]


def _mixer_body(*refs, L, vlo, vhi, sample):
    zb_ref, zs_ref = refs[:2]
    k = 2
    init = None
    if sample:
        init = refs[k:k + N_MIXER_INIT]
        k += N_MIXER_INIT
    consts = refs[k:k + N_MIXER_CONSTS]
    k += N_MIXER_CONSTS
    outs = refs[k:k + N_MIXER_OUTS]
    if not sample:
        _mixer_compute(zb_ref, zs_ref, init, consts, outs, refs[k + N_MIXER_OUTS:], L=L, vlo=vlo, vhi=vhi)
        return
    zx_s, zsx_s, ux_s = refs[-3:]
    n_tok = vhi - vlo
    for c in range(0, N_BIG, PROJ_COLS):
        zx_s[:, c:c + PROJ_COLS] = _spread_tokens(zb_ref[:, c:c + PROJ_COLS], vlo, n_tok)
    zsx_s[...] = _spread_tokens(zs_ref[...], vlo, n_tok)
    _mixer_compute(zx_s, zsx_s, init, consts, (ux_s,) + tuple(outs[1:]), refs[k + N_MIXER_OUTS:-3],
                   L=L, vlo=vlo, vhi=vhi)
    outs[0][...] = _gather_tokens(ux_s[...], vlo, n_tok).astype(outs[0].dtype)


def _fused_body(x_ref, shift_ref, scale_ref, g_ref, wbig_ref, wsh_ref, wsl_ref, *refs, L):
    consts = refs[:N_MIXER_CONSTS]
    tail_ref = refs[N_MIXER_CONSTS]
    outs = refs[N_MIXER_CONSTS + 1:N_MIXER_CONSTS + 1 + N_MIXER_OUTS]
    k = N_MIXER_CONSTS + 1 + N_MIXER_OUTS
    zb_s = list(refs[k:k + N_BIG // PROJ_COLS])
    zs_s = refs[k + N_BIG // PROJ_COLS]
    scratch = refs[k + N_BIG // PROJ_COLS + 1:]
    _zero_carried_state(outs, scratch)
    narrow, wide = _inproj_parts(x_ref, shift_ref, scale_ref, g_ref, wbig_ref, wsh_ref, wsl_ref,
                                 zb_s, zs_s, tail_ref, bt=1, tt=L)

    order = (OFF_QKA, OFF_QB, OFF_GA, OFF_OA, OFF_GB, OFF_RB, OFF_VA, OFF_VB)
    pieces = [(off // PROJ_COLS, lo) for off in order for lo in range(0, PROJ_COLS, PIECE_COLS)]
    emitted = []

    def tick(n=1):
        if not emitted:
            narrow()
        for _ in range(n):
            if len(emitted) < len(pieces):
                idx, lo = pieces[len(emitted)]
                wide(idx, lo, PIECE_COLS)
                emitted.append((idx, lo))

    def need(off, width):
        wanted = [(off // PROJ_COLS, lo) for lo in range(0, PROJ_COLS, PIECE_COLS)
                  if lo < off % PROJ_COLS + width and lo + PIECE_COLS > off % PROJ_COLS]
        while any(w not in emitted for w in wanted):
            tick()

    _mixer_compute(zb_s, zs_s, None, consts, outs, scratch, L=L, vlo=0, vhi=L, zero_init=False,
                   hooks=(tick, need))


def _zero_carried_state(outs, scratch):
    _, _, _, mo_ref, So_ref = outs
    carry_s, caug_s = scratch[-2:]

    @pl.when(pl.program_id(1) == 0)
    def _zero():
        caug_s[...] = jnp.zeros(caug_s.shape, f32)
        mo_ref[...] = jnp.zeros(mo_ref.shape, f32)
        So_ref[...] = jnp.zeros(So_ref.shape, f32)
        carry_s[...] = jnp.zeros(carry_s.shape, f32)


def _mixer_compute(zb_ref, zs_ref, init, consts, outs, scratch, *, L, vlo, vhi, zero_init=True,
                   hooks=(lambda n=1: None, lambda off, width: None)):
    sample = init is not None
    if sample:
        conv0_ref, C0_ref, n0_ref, m0_ref, S0_ref = init
    cw_ref, cb_ref, gb_ref, wal_ref, bal_ref, gha_ref, ghb_ref = consts
    u_ref, Co_ref, no_ref, mo_ref, So_ref = outs
    qa_s, ka_s, qh_s, kh_s, kb_s, x_s, r_s, bg_s, ha_s, hb_s, pa_s, pb_s, carry_s, caug_s = scratch
    rows = zs_ref.shape[0]
    every = slice(None)

    tick, need = hooks

    def zcols(rsel, off, width):
        need(off, width)
        if isinstance(zb_ref, (list, tuple)):
            grp, o = divmod(off, PROJ_COLS)
            assert o + width <= PROJ_COLS
            return zb_ref[grp][rsel, o:o + width]
        return zb_ref[rsel, off:off + width]
    n_groups = rows // L
    log2_l = L.bit_length() - 1
    masked = not (vlo == 0 and vhi == L)
    scale = DK ** -0.5

    if not sample and zero_init:
        _zero_carried_state(outs, scratch)

    row_in = lax.broadcasted_iota(jnp.int32, (rows, 1), 0) & (L - 1)
    valid = (row_in >= vlo) & (row_in < vhi)

    cw = cw_ref[...]
    cb = cb_ref[...]
    for c0 in range(0, 2 * QK, PIECE_COLS):
        cs = slice(c0, c0 + PIECE_COLS)
        zq = zcols(every, OFF_QKA + c0, PIECE_COLS).astype(f32)
        if sample:
            zq = jnp.where(row_in < CONV_W - 1, conv0_ref[:, cs], zq)
            carry = jnp.zeros((SUBLANES, PIECE_COLS), f32)
        else:
            carry = carry_s[:, cs]
            carry_s[:, cs] = zq[rows - SUBLANES:rows, :]
        xe = jnp.concatenate([carry, zq], axis=0)
        conv = cb[:, cs] + zq * cw[CONV_W - 1:CONV_W, cs]
        for k in range(CONV_W - 1):
            o = SUBLANES - (CONV_W - 1) + k
            conv = conv + xe[o:o + rows] * cw[k:k + 1, cs]
        qk = conv * _sigmoid(conv)
        if c0 < QK:
            qa_s[:, cs] = qk * scale
        else:
            ka_s[:, c0 - QK:c0 - QK + PIECE_COLS] = qk
        tick()

    lane = lax.broadcasted_iota(jnp.int32, (1, N_SMALL), 1)
    zs = zs_ref[...]
    pre = zs + gb_ref[...]
    gate = jnp.where(lane < LANE_FG, pre, _log_sigmoid(pre))
    gate_f = jnp.where(valid, gate, 0.0) if masked else gate
    rr = lax.broadcasted_iota(jnp.int32, (rows, rows), 0)
    cc = lax.broadcasted_iota(jnp.int32, (rows, rows), 1)
    tril_mask = ((rr >> log2_l) == (cc >> log2_l)) & (cc <= rr)
    tril = jnp.where(tril_mask, 1.0, 0.0).astype(bf16)

    la_lin = _mm(zs.astype(bf16), wal_ref[...]) + bal_ref[...]
    tick(2)

    la_parts, kb_parts = [], []
    for h in range(N_HEADS):
        ks = slice(h * DK, (h + 1) * DK)
        la_h = _log_sigmoid(la_lin[:, ks]) * (1.0 / GLA_TAU)
        la_parts.append(jnp.where(valid, la_h, 0.0) if masked else la_h)
        tick()

    both = jnp.concatenate([gate_f] + la_parts, axis=1)
    wb = N_SMALL + QK
    t = _mm(tril, jnp.concatenate(_split3(both), axis=1))
    csum = t[:, :wb] + (t[:, wb:2 * wb] + t[:, 2 * wb:])
    tick(2)
    x_all = jnp.where(lane < LANE_FG, gate, csum[:, :N_SMALL])
    x_s[...] = x_all
    b_al = pltpu.roll(x_all, N_SMALL - N_HEADS, axis=1)
    r_all = x_all - b_al
    r_ok = (lane < N_HEADS) & valid if masked else jnp.broadcast_to(lane < N_HEADS, (rows, N_SMALL))
    r_s[...] = jnp.where(r_ok, r_all, NEG)
    range_m = jnp.max(jnp.where(r_ok, r_all, NEG)) - jnp.min(jnp.where(r_ok, r_all, -NEG))
    bg = csum[:, N_SMALL:]
    bg_s[...] = bg
    range_g = jnp.max(-bg)
    tick()
    for h in range(N_HEADS):
        ks = slice(h * DK, (h + 1) * DK)
        kb_h = zcols(every, OFF_KB + h * DK, DK).astype(f32)
        kb_parts.append(jnp.where(valid, kb_h, 0.0) if masked else kb_h)
        qh_s[:, ks] = zcols(every, OFF_QB + h * DK, DK).astype(f32) * scale * jnp.exp(bg[:, ks])
        kh_s[:, ks] = kb_parts[h] * jnp.exp(-bg[:, ks])
        tick()
    kb = jnp.concatenate(kb_parts, axis=1)
    fast_ok = jnp.logical_and(range_m <= SAFE_LOG_RANGE, range_g <= SAFE_LOG_RANGE)

    for c0 in range(0, D_MODEL, PIECE_COLS):
        cs = slice(c0, c0 + PIECE_COLS)
        col = lambda off: zcols(every, off + c0, PIECE_COLS).astype(f32)
        pa_s[:, cs] = _sigmoid(col(OFF_GA)) * _sigmoid(col(OFF_OA)) * gha_ref[:, cs]
        tick(2)
    for c0 in range(0, D_MODEL, PIECE_COLS):
        cs = slice(c0, c0 + PIECE_COLS)
        col = lambda off: zcols(every, off + c0, PIECE_COLS).astype(f32)
        rb = col(OFF_RB)
        pb_s[:, cs] = _sigmoid(col(OFF_GB)) * (rb * _sigmoid(rb)) * ghb_ref[:, cs]
        tick(2)
    need(OFF_VA, D_MODEL)
    need(OFF_VB, D_MODEL)

    tri = lax.broadcasted_iota(jnp.int32, (L, L), 0) >= lax.broadcasted_iota(jnp.int32, (L, L), 1)
    eye = lax.broadcasted_iota(jnp.int32, (DK, DK), 0) == lax.broadcasted_iota(jnp.int32, (DK, DK), 1)
    lane_h = lane
    lrow = lax.broadcasted_iota(jnp.int32, (1, L), 1)
    lcol = lax.broadcasted_iota(jnp.int32, (L, 1), 0)
    vrow = (lrow >= vlo) & (lrow < vhi)
    ones16 = jnp.ones((L, DK), bf16)

    def load_state(g, h):
        if sample:
            n_mat = jnp.broadcast_to(_row_to_col(n0_ref[g, pl.ds(h, 1), :], eye), (DK, DK))
            return jnp.concatenate([C0_ref[g, h], n_mat], axis=1), S0_ref[g, h]
        return caug_s[h], So_ref[0, h]

    def store_state(g, h, caug, s_new):
        sidx = g if sample else 0
        if not sample:
            caug_s[h] = caug
        Co_ref[sidx, h] = caug[:, :DV]
        no_ref[sidx, pl.ds(h, 1), :] = _diag_row(caug[:, DV:], eye)
        So_ref[sidx, h] = s_new

    def group_fast(g):
        r0 = g * L
        rs = pl.ds(r0, L)
        xc = x_s[rs, :]
        rc = r_s[rs, :]
        m_prev = m0_ref[g] if sample else mo_ref[0]
        bg_tail = bg_s[pl.ds(r0 + L - SUBLANES, SUBLANES), :]
        heads = []
        for h in range(N_HEADS):
            ks = slice(h * DK, (h + 1) * DK)
            caug, s_st = load_state(g, h)
            c0 = jnp.max(rc[:, h:h + 1], axis=0, keepdims=True)
            q16 = qa_s[rs, ks].astype(bf16)
            kt16 = (ka_s[rs, ks] * jnp.exp(rc[:, h:h + 1] - c0)).astype(bf16)
            qh16 = qh_s[rs, ks].astype(bf16)
            kh = kh_s[rs, ks]
            heads.append(dict(
                caug=caug, s_st=s_st, c0=c0, kt16=kt16, kh=kh,
                va=jnp.concatenate([zcols(rs, OFF_VA + h * DV, DV).astype(bf16), ones16], axis=1),
                vb16=zcols(rs, OFF_VB + h * DV, DV).astype(bf16),
                s_raw=_mm_nt(q16, kt16), a_raw=_mm_nt(qh16, kh.astype(bf16)),
                ci=_mm(q16, caug.astype(bf16)), oi=_mm(qh16, s_st.astype(bf16))))
        for p in heads:
            p["ni"] = _mm(jnp.where(tri, p["s_raw"], 0.0).astype(bf16), p["va"])
            p["o"] = _mm(jnp.where(tri, p["a_raw"], 0.0).astype(bf16), p["vb16"]) + p["oi"]
        for h, p in enumerate(heads):
            p["eb_last"] = jnp.exp(bg_tail[SUBLANES - 1:SUBLANES, h * DK:(h + 1) * DK])
            p["kv"] = _mm_tn(p["kt16"], p["va"])
            p["skv"] = _mm_tn((p["kh"] * p["eb_last"]).astype(bf16), p["vb16"])
        m_vec = m_prev
        for h, p in enumerate(heads):
            m_st = m_prev[:, h:h + 1]
            b_col = xc[:, LANE_FG + h:LANE_FG + h + 1]
            b_last = b_col[L - 1:L, :]
            c0 = p["c0"]
            mm = jnp.maximum(c0, m_st)
            comb = jnp.exp(c0 - mm) * p["ni"] + jnp.exp(m_st - mm) * p["ci"]
            inv = 1.0 / jnp.maximum(jnp.abs(comb[:, DV:]), jnp.exp(-(b_col + mm)))
            ha_s[rs, h * DV:h * DV + DK] = comb[:, :DK] * inv
            ha_s[rs, h * DV + DK:(h + 1) * DV] = comb[:, DK:DV] * inv
            hb_s[rs, h * DV:(h + 1) * DV] = p["o"]
            m_new = jnp.maximum(b_last + m_st, b_last + c0)
            caug_new = jnp.exp(b_last + m_st - m_new) * p["caug"] + jnp.exp(b_last + c0 - m_new) * p["kv"]
            s_new = _row_to_col(p["eb_last"], eye) * p["s_st"] + p["skv"]
            store_state(g, h, caug_new, s_new)
            m_vec = jnp.where(lane_h == h, m_new, m_vec)
        mo_ref[g if sample else 0] = m_vec

    def group_exact(g):
        static = isinstance(g, int)
        r0 = g * L if static else pl.multiple_of(g * L, L)
        tail0 = r0 + L - SUBLANES if static else pl.multiple_of(r0 + L - SUBLANES, SUBLANES)
        rs = pl.ds(r0, L)
        xc = x_s[rs, :]
        rc = r_s[rs, :]
        sr = lax.broadcasted_iota(jnp.int32, (2 * SUBLANES, N_SMALL), 0)
        sc = lax.broadcasted_iota(jnp.int32, (2 * SUBLANES, N_SMALL), 1)
        sel = jnp.where((sr < N_HEADS) & (sc == sr + LANE_IG), 1.0,
                        jnp.where((sr < N_HEADS) & (sc == sr + LANE_FG), -1.0, 0.0)).astype(bf16)
        xh, xm, xl = _split3(xc)
        r_rows = _mm_nt(sel, xh) + (_mm_nt(sel, xm) + _mm_nt(sel, xl))
        m_prev = m0_ref[g] if sample else mo_ref[0]
        m_vec = m_prev
        bg_tail = bg_s[pl.ds(tail0, SUBLANES), :]
        for h in range(N_HEADS):
            ks = slice(h * DK, (h + 1) * DK)
            vs = slice(h * DV, (h + 1) * DV)
            caug, s_st = load_state(g, h)
            m_st = m_prev[:, h:h + 1]
            b_col = xc[:, LANE_FG + h:LANE_FG + h + 1]
            r_col = rc[:, h:h + 1]
            b_last = b_col[L - 1:L, :]
            v16 = zcols(rs, OFF_VA + h * DV, DV).astype(bf16)
            vb16 = zcols(rs, OFF_VB + h * DV, DV).astype(bf16)
            eb_last = jnp.exp(bg_tail[SUBLANES - 1:SUBLANES, ks])
            eb_col = _row_to_col(eb_last, eye)
            c_st = caug[:, :DV]
            n_row = _diag_row(caug[:, DV:], eye)
            r_row = r_rows[h:h + 1, :]
            if masked:
                r_row = jnp.where(vrow, r_row, NEG)
            dm = jnp.where(tri, b_col + r_row, NEG)
            m_t = jnp.maximum(jnp.max(dm, axis=1, keepdims=True), b_col + m_st)
            w = jnp.exp(dm - m_t)
            wi = jnp.exp(b_col + m_st - m_t)
            q = qa_s[rs, ks]
            k = ka_s[rs, ks]
            q16 = q.astype(bf16)
            s = _mm_nt(q16, k.astype(bf16)) * w
            num = _mm(s.astype(bf16), v16) + wi * _mm(q16, c_st.astype(bf16))
            den = jnp.sum(s, axis=1, keepdims=True) + wi * jnp.sum(q * n_row, axis=1, keepdims=True)
            ha_s[rs, vs] = num / jnp.maximum(jnp.abs(den), jnp.exp(-m_t))
            m_new = jnp.maximum(b_last + m_st, b_last + jnp.max(r_row, axis=1, keepdims=True))
            wc = jnp.exp(b_last + m_st - m_new)
            kt = k * jnp.exp(b_last + r_col - m_new)
            n_new = wc * n_row + jnp.sum(kt, axis=0, keepdims=True)
            caug_new = jnp.concatenate(
                [wc * c_st + _mm_tn(kt.astype(bf16), v16),
                 jnp.broadcast_to(_row_to_col(n_new, eye), (DK, DK))], axis=1)
            qb = zcols(rs, OFF_QB + h * DK, DK).astype(f32) * scale
            bgc = bg_s[rs, ks]
            sub = lax.broadcasted_iota(jnp.int32, (SUBLANES, 1), 0)

            def score_col(s_i, a, ks=ks, qb=qb, bgc=bgc, sub=sub):
                s8 = lax.shift_left(lax.shift_right_logical(s_i, 3), 3)
                base = pl.multiple_of(r0 + s8, SUBLANES)
                pick = sub == (s_i - s8)
                b_s = jnp.sum(jnp.where(pick, bg_s[pl.ds(base, SUBLANES), ks], 0.0), axis=0, keepdims=True)
                k_s = jnp.sum(jnp.where(pick, kb_s[pl.ds(base, SUBLANES), ks], 0.0), axis=0, keepdims=True)
                e = jnp.exp(jnp.where(lcol >= s_i, bgc - b_s, NEG))
                col = jnp.sum(qb * e * k_s, axis=1, keepdims=True)
                return jnp.where(lrow == s_i, col, a)

            a = lax.fori_loop(0, L, score_col, jnp.zeros((L, L), f32))
            hb_s[rs, vs] = (_mm(a.astype(bf16), vb16)
                            + _mm((qb * jnp.exp(bgc)).astype(bf16), s_st.astype(bf16)))
            kt_b = kb_s[rs, ks] * jnp.exp(bg_tail[SUBLANES - 1:SUBLANES, ks] - bgc)
            s_new = eb_col * s_st + _mm_tn(kt_b.astype(bf16), vb16)
            store_state(g, h, caug_new, s_new)
            m_vec = jnp.where(lane_h == h, m_new, m_vec)
        mo_ref[g if sample else 0] = m_vec

    def groups_fast_batched():
        r_all_m = r_s[...]
        c0_l, m_l, mm_l, fac, wc = [], [], [], [], []
        for g in range(n_groups):
            gs = slice(g * L, (g + 1) * L)
            c0_g = jnp.max(r_all_m[gs], axis=0, keepdims=True)
            m_g = m0_ref[g]
            b_last = b_al[gs][L - 1:L, :]
            m_new = jnp.maximum(b_last + m_g, b_last + c0_g)
            mo_ref[g] = m_new
            c0_l.append(c0_g)
            m_l.append(m_g)
            mm_l.append(jnp.maximum(c0_g, m_g))
            fac.append(jnp.exp(b_last + c0_g - m_new))
            wc.append(jnp.exp(b_last + m_g - m_new))
        rows_of = lambda lst: jnp.concatenate([jnp.broadcast_to(v, (L, N_SMALL)) for v in lst], axis=0)
        c0_r, m_r, mm_r = rows_of(c0_l), rows_of(m_l), rows_of(mm_l)
        beta = jnp.exp(r_all_m - c0_r)
        gi_r = jnp.exp(c0_r - mm_r)
        wi_r = jnp.exp(m_r - mm_r)
        floor = jnp.exp(-(b_al + mm_r))
        ones_g = jnp.ones((L, DK), bf16)
        for h in range(N_HEADS):
            ks = slice(h * DK, (h + 1) * DK)
            vs = slice(h * DV, (h + 1) * DV)
            kt = ka_s[:, ks] * beta[:, h:h + 1]
            v = zcols(every, OFF_VA + h * DV, DV)
            vb = zcols(every, OFF_VB + h * DV, DV)
            q16 = qa_s[:, ks].astype(bf16)
            va = jnp.concatenate([v.astype(bf16), jnp.ones((rows, DK), bf16)], axis=1)
            s16 = jnp.where(tril_mask, _mm_nt(q16, kt.astype(bf16)), 0.0).astype(bf16)
            ni = _mm(s16, va)
            qh16 = qh_s[:, ks].astype(bf16)
            kh = kh_s[:, ks]
            a16 = jnp.where(tril_mask, _mm_nt(qh16, kh.astype(bf16)), 0.0).astype(bf16)
            o_intra = _mm(a16, vb.astype(bf16))
            ci_parts, oi_parts = [], []
            for g in range(n_groups):
                gs = slice(g * L, (g + 1) * L)
                caug, s_st = load_state(g, h)
                va_g = jnp.concatenate([v[gs].astype(bf16), ones_g], axis=1)
                ci_parts.append(_mm(qa_s[gs, ks].astype(bf16), caug.astype(bf16)))
                caug_new = wc[g][:, h:h + 1] * caug + fac[g][:, h:h + 1] * _mm_tn(kt[gs].astype(bf16), va_g)
                oi_parts.append(_mm(qh_s[gs, ks].astype(bf16), s_st.astype(bf16)))
                eb_last = jnp.exp(bg_s[gs, ks][L - 1:L, :])
                s_new = (_row_to_col(eb_last, eye) * s_st
                         + _mm_tn((kh[gs] * eb_last).astype(bf16), vb[gs].astype(bf16)))
                store_state(g, h, caug_new, s_new)
            comb = gi_r[:, h:h + 1] * ni + wi_r[:, h:h + 1] * jnp.concatenate(ci_parts, axis=0)
            inv = 1.0 / jnp.maximum(jnp.abs(comb[:, DV:]), floor[:, h:h + 1])
            ha_s[:, h * DV:h * DV + DK] = comb[:, :DK] * inv
            ha_s[:, h * DV + DK:(h + 1) * DV] = comb[:, DK:DV] * inv
            hb_s[:, vs] = o_intra + jnp.concatenate(oi_parts, axis=0)

    def run_groups(exact):
        if not exact:
            if sample and L == SUBLANES:
                groups_fast_batched()
            else:
                for g in range(n_groups):
                    group_fast(g)
        elif n_groups == 1:
            group_exact(0)
        else:
            def body(g, carry_):
                group_exact(g)
                return carry_
            lax.fori_loop(0, n_groups, body, 0)

    @pl.when(fast_ok)
    def _fast():
        run_groups(False)

    @pl.when(jnp.logical_not(fast_ok))
    def _exact():
        kb_s[...] = kb
        run_groups(True)

    for h in range(N_HEADS):
        vs = slice(h * DV, (h + 1) * DV)
        u = pa_s[:, vs] * _rms(ha_s[:, vs]) + pb_s[:, vs] * _rms(hb_s[:, vs])
        u_ref[:, vs] = u.astype(u_ref.dtype)


def _mixer_call(zb, zs, init, consts, *, nb, nt, bt, L, vlo, vhi):
    sample = init is not None
    rows = bt * L
    tok_rows = bt * (vhi - vlo) if sample else rows
    n_state = nb * bt
    row_spec = lambda n, r=rows: pl.BlockSpec((r, n), lambda i, j: (i * nt + j, 0))
    st4 = pl.BlockSpec((bt, N_HEADS, DK, DV), lambda i, j: (i, 0, 0, 0))
    st_n = pl.BlockSpec((bt, N_HEADS, DK), lambda i, j: (i, 0, 0))
    st_m = pl.BlockSpec((bt, 1, N_SMALL), lambda i, j: (i, 0, 0))
    in_specs = [row_spec(N_BIG, tok_rows), row_spec(N_SMALL, tok_rows)]
    args = [zb, zs]
    scratch = _mixer_scratch(rows)
    if sample:
        in_specs += [row_spec(2 * QK), st4, st_n, st_m, st4]
        args += list(init)
        scratch += [pltpu.VMEM((rows, N_BIG), f32), pltpu.VMEM((rows, N_SMALL), f32),
                    pltpu.VMEM((rows, D_MODEL), f32)]
    in_specs += [_const_spec(c.shape) for c in consts]
    args += list(consts)
    return pl.pallas_call(
        functools.partial(_mixer_body, L=L, vlo=vlo, vhi=vhi, sample=sample),
        out_shape=(
            jax.ShapeDtypeStruct((zb.shape[0], D_MODEL), bf16),
            jax.ShapeDtypeStruct((n_state, N_HEADS, DK, DV), f32),
            jax.ShapeDtypeStruct((n_state, N_HEADS, DK), f32),
            jax.ShapeDtypeStruct((n_state, 1, N_SMALL), f32),
            jax.ShapeDtypeStruct((n_state, N_HEADS, DK, DV), f32),
        ),
        grid=(nb, nt),
        in_specs=in_specs,
        out_specs=(row_spec(D_MODEL, tok_rows), st4, st_n, st_m, st4),
        scratch_shapes=scratch,
        compiler_params=pltpu.CompilerParams(
            dimension_semantics=("arbitrary", "arbitrary"), vmem_limit_bytes=VMEM_LIMIT_BYTES),
        name="mixer_sample" if sample else "mixer_prompt",
    )(*args)


def _fused_call(x2d, ada4, g_norm, w_big, ws_hi, ws_lo, consts, *, nb, nt, L):
    ada, row0 = ada4
    row_spec = lambda n: pl.BlockSpec((L, n), lambda i, j: (i * nt + j, 0))
    st4 = pl.BlockSpec((1, N_HEADS, DK, DV), lambda i, j: (i, 0, 0, 0))
    st_n = pl.BlockSpec((1, N_HEADS, DK), lambda i, j: (i, 0, 0))
    st_m = pl.BlockSpec((1, 1, N_SMALL), lambda i, j: (i, 0, 0))
    return pl.pallas_call(
        functools.partial(_fused_body, L=L),
        out_shape=(
            jax.ShapeDtypeStruct((nb * nt, SUBLANES, PROJ_COLS), f32),
            jax.ShapeDtypeStruct((x2d.shape[0], D_MODEL), bf16),
            jax.ShapeDtypeStruct((nb, N_HEADS, DK, DV), f32),
            jax.ShapeDtypeStruct((nb, N_HEADS, DK), f32),
            jax.ShapeDtypeStruct((nb, 1, N_SMALL), f32),
            jax.ShapeDtypeStruct((nb, N_HEADS, DK, DV), f32),
        ),
        grid=(nb, nt),
        in_specs=[
            row_spec(D_MODEL), _mod_spec(1, 0, row0), _mod_spec(1, 1, row0),
            _const_spec((1, D_MODEL)), _const_spec((D_MODEL, N_BIG)),
            _const_spec((D_MODEL, N_SMALL)), _const_spec((D_MODEL, N_SMALL)),
        ] + [_const_spec(c.shape) for c in consts],
        out_specs=(
            pl.BlockSpec((1, SUBLANES, PROJ_COLS), lambda i, j: (i * nt + j, 0, 0)),
            row_spec(D_MODEL), st4, st_n, st_m, st4,
        ),
        scratch_shapes=([pltpu.VMEM((L, PROJ_COLS), f32)] * (N_BIG // PROJ_COLS)
                        + [pltpu.VMEM((L, N_SMALL), f32)] + _mixer_scratch(L)),
        compiler_params=pltpu.CompilerParams(
            dimension_semantics=("arbitrary", "arbitrary"), vmem_limit_bytes=VMEM_LIMIT_BYTES),
        name="proj_mixer_prompt",
    )(x2d, ada, ada, g_norm, w_big, ws_hi, ws_lo, *consts)


def _out_body(u_ref, x_ref, gate1_ref, shift2_ref, scale2_ref, gate2_ref, gn2_ref, gfin_ref,
              wout_ref, wg_ref, wu_ref, wd_ref, y_ref, *, bt, tt):
    x1 = x_ref[...] + _mod_rows(gate1_ref, bt, tt) * _mm(u_ref[...], wout_ref[...])
    h2 = _rms(x1) * gn2_ref[...]
    h2 = (h2 * (1.0 + _mod_rows(scale2_ref, bt, tt)) + _mod_rows(shift2_ref, bt, tt)).astype(bf16)
    acc = jnp.zeros(x1.shape, f32)
    for c in range(0, D_FF, FFN_COLS):
        g = _mm(h2, wg_ref[:, c:c + FFN_COLS])
        up = _mm(h2, wu_ref[:, c:c + FFN_COLS])
        acc = acc + _mm((g * _sigmoid(g) * up).astype(bf16), wd_ref[c:c + FFN_COLS, :])
    x2 = x1 + _mod_rows(gate2_ref, bt, tt) * acc
    y_ref[...] = _rms(x2) * gfin_ref[...]


def _out_call(u2d, x2d, ada4, g_norm2, g_final, w_out, w_gate, w_up, w_down, *, nb, nt, bt, tt):
    ada, row0 = ada4
    rows = bt * tt
    row_spec = pl.BlockSpec((rows, D_MODEL), lambda i, j: (i * nt + j, 0))
    return pl.pallas_call(
        functools.partial(_out_body, bt=bt, tt=tt),
        out_shape=jax.ShapeDtypeStruct(x2d.shape, f32),
        grid=(nb, nt),
        in_specs=[
            row_spec, row_spec,
            _mod_spec(bt, 2, row0), _mod_spec(bt, 3, row0), _mod_spec(bt, 4, row0), _mod_spec(bt, 5, row0),
            _const_spec((1, D_MODEL)), _const_spec((1, D_MODEL)),
            _const_spec((D_MODEL, D_MODEL)), _const_spec((D_MODEL, D_FF)),
            _const_spec((D_MODEL, D_FF)), _const_spec((D_FF, D_MODEL)),
        ],
        out_specs=row_spec,
        compiler_params=pltpu.CompilerParams(
            dimension_semantics=("arbitrary", "arbitrary"), vmem_limit_bytes=VMEM_LIMIT_BYTES),
        name="out_ffn",
    )(u2d, x2d, ada, ada, ada, ada, g_norm2, g_final, w_out, w_gate, w_up, w_down)


PROMPT_CHUNK = 256
PROMPT_FFN_ROWS = 512
SAMPLE_L = SUBLANES
SAMPLE_MIX_GROUPS = 8
SAMPLE_DENSE_GROUPS = 64


W_IN_SEG_A = (0, 3072)
W_IN_GATES = 3072
W_IN_SEG_B = (3080, 6152)
W_IN_LR = 6152
W_IN_SEG_C = (6168, 8216)
LANES = 128


PREP_COLS = 512
SHIFT_B = W_IN_SEG_B[0] - OFF_QB
SHIFT_C = W_IN_SEG_C[0] - OFF_GA
PREP_HALO = 32


def _prep_body(a_ref, b_ref, gt_ref, lt_ref, wb_ref, wh_ref, wl_ref):
    j = pl.program_id(0)
    both = jnp.concatenate([a_ref[...], b_ref[0:PREP_HALO, :]], axis=0)

    def emit(shift):
        wb_ref[...] = both[shift:shift + PREP_COLS, :].T.astype(bf16)

    pl.when(j < OFF_QB // PREP_COLS)(lambda: emit(0))
    pl.when((j >= OFF_QB // PREP_COLS) & (j < OFF_GA // PREP_COLS))(lambda: emit(SHIFT_B))
    pl.when(j >= OFF_GA // PREP_COLS)(lambda: emit(SHIFT_C))

    @pl.when(j == 0)
    def _narrow():
        row = lax.broadcasted_iota(jnp.int32, (LANES, 1), 0)
        small = jnp.where(row < LANE_LR, gt_ref[...], jnp.where(row < LANE_LR + GLA_RANK, lt_ref[...], 0.0)).T
        hi = small.astype(bf16)
        wh_ref[...] = hi
        wl_ref[...] = (small - hi.astype(f32)).astype(bf16)


def _prep_call(w_in):
    assert W_IN_GATES % LANES == 0 and (W_IN_LR - LANE_LR) % LANES == 0
    assert max(SHIFT_B, SHIFT_C) <= PREP_HALO and SHIFT_B % SUBLANES == 0 and SHIFT_C % SUBLANES == 0
    wt = jnp.transpose(w_in)
    last = (wt.shape[0] - 1) // PREP_COLS
    const = lambda shape: pl.BlockSpec(shape, lambda j: (0, 0))
    return pl.pallas_call(
        _prep_body,
        out_shape=(
            jax.ShapeDtypeStruct((D_MODEL, N_BIG), bf16),
            jax.ShapeDtypeStruct((D_MODEL, N_SMALL), bf16),
            jax.ShapeDtypeStruct((D_MODEL, N_SMALL), bf16),
        ),
        grid=(N_BIG // PREP_COLS,),
        in_specs=[
            pl.BlockSpec((PREP_COLS, D_MODEL), lambda j: (j, 0)),
            pl.BlockSpec((PREP_COLS, D_MODEL), lambda j: (jnp.minimum(j + 1, last), 0)),
            pl.BlockSpec((LANES, D_MODEL), lambda j: (W_IN_GATES // LANES, 0)),
            pl.BlockSpec((LANES, D_MODEL), lambda j: ((W_IN_LR - LANE_LR) // LANES, 0)),
        ],
        out_specs=(
            pl.BlockSpec((D_MODEL, PREP_COLS), lambda j: (0, j)),
            const((D_MODEL, N_SMALL)), const((D_MODEL, N_SMALL)),
        ),
        compiler_params=pltpu.CompilerParams(
            dimension_semantics=("arbitrary",), vmem_limit_bytes=VMEM_LIMIT_BYTES),
        name="w_in_prep",
    )(wt, wt, wt, wt)


def _prep_layer_weights(w_in, w_alpha, b_igate, b_fgate):
    w_big, ws_hi, ws_lo = _prep_call(w_in)
    w_al = jnp.zeros((N_SMALL, QK), f32).at[LANE_LR:LANE_LR + GLA_RANK].set(w_alpha).astype(bf16)
    g_bias = jnp.concatenate([b_igate, b_fgate, jnp.zeros((N_SMALL - 2 * N_HEADS,), f32)])[None, :]
    return w_big, ws_hi, ws_lo, w_al, g_bias


def _layer(xp2d, xs2d, st_s, ada_p, ada_s, p, *, bp, tp, bs):
    (g_norm1, g_norm2, w_in, conv_w, conv_b, b_igate, b_fgate, g_head_a,
     w_alpha, b_alpha, g_head_b, w_out, w_ffn_gate, w_ffn_up, w_ffn_down, g_final_or_ones) = p
    w_big, ws_hi, ws_lo, w_al, g_bias = _prep_layer_weights(w_in, w_alpha, b_igate, b_fgate)
    consts = (conv_w, conv_b[None, :], g_bias, w_al, b_alpha[None, :],
              g_head_a.reshape(1, D_MODEL), g_head_b.reshape(1, D_MODEL))
    w_out16, w_g16, w_u16, w_d16 = (w.astype(bf16) for w in (w_out, w_ffn_gate, w_ffn_up, w_ffn_down))
    gn1, gn2, gfin = g_norm1[None, :], g_norm2[None, :], g_final_or_ones[None, :]

    ntp = tp // PROMPT_CHUNK
    tail, u, c_p, n_p, m_p, s_p = _fused_call(xp2d, ada_p, gn1, w_big, ws_hi, ws_lo, consts,
                                              nb=bp, nt=ntp, L=PROMPT_CHUNK)
    conv_p = tail.reshape(bp, ntp, SUBLANES, 2 * QK)[:, ntp - 1, SUBLANES - (CONV_W - 1):, :]
    yp = _out_call(u, xp2d, ada_p, gn2, gfin, w_out16, w_g16, w_u16, w_d16,
                   nb=bp, nt=tp // PROMPT_FFN_ROWS, bt=1, tt=PROMPT_FFN_ROWS)

    conv0, c0, n0, m0, s0 = st_s
    lo = CONV_W - 1
    n_tok = SAMPLE_L - lo - 1
    nbd = bs // SAMPLE_DENSE_GROUPS
    zb_s, zs_s, _ = _inproj_call(xs2d, ada_s, gn1, w_big, ws_hi, ws_lo,
                                 nb=nbd, nt=1, bt=SAMPLE_DENSE_GROUPS, tt=n_tok, zdtype=f32)
    conv_s = zb_s[:, OFF_QKA:OFF_QKA + 2 * QK].reshape(bs, n_tok, 2 * QK)[:, n_tok - lo:, :]
    conv0_rows = jnp.pad(conv0, ((0, 0), (0, SAMPLE_L - lo), (0, 0))).reshape(bs * SAMPLE_L, 2 * QK)
    m0_lanes = jnp.pad(m0[:, None, :], ((0, 0), (0, 0), (0, N_SMALL - N_HEADS)))
    u_s, c_s, n_s, m_s, s_s = _mixer_call(
        zb_s, zs_s, (conv0_rows, c0, n0, m0_lanes, s0), consts,
        nb=bs // SAMPLE_MIX_GROUPS, nt=1, bt=SAMPLE_MIX_GROUPS, L=SAMPLE_L, vlo=lo, vhi=lo + n_tok)
    ys = _out_call(u_s, xs2d, ada_s, gn2, gfin, w_out16, w_g16, w_u16, w_d16,
                   nb=nbd, nt=1, bt=SAMPLE_DENSE_GROUPS, tt=n_tok)
    st_p = (conv_p, c_p, n_p, m_p[:, 0, :N_HEADS], s_p)
    st_s_new = (conv_s, c_s, n_s, m_s[:, 0, :N_HEADS], s_s)
    return yp, ys, st_p, st_s_new


def kernel(x_prompt, x_sample, state_conv, state_C, state_n, state_m, state_S, c_prompt, c_sample,
           g_norm1, g_norm2, w_ada, b_ada, w_in, conv_w, conv_b, b_igate, b_fgate, g_head_a,
           w_alpha, b_alpha, g_head_b, w_out, w_ffn_gate, w_ffn_up, w_ffn_down, g_final):
    bp, tp, _ = x_prompt.shape
    bs, ts, _ = x_sample.shape
    depth = w_in.shape[0]
    assert depth == 1, "the final norm is fused into the (single) layer's FFN kernel"
    lo = CONV_W - 1
    assert ts == SAMPLE_L - lo - 1
    xp2d = x_prompt.reshape(bp * tp, D_MODEL)
    xs2d = x_sample.reshape(bs * ts, D_MODEL)
    c_all = jnp.concatenate([c_sample, c_prompt], axis=0)
    new_p = [[] for _ in range(5)]
    new_s = [[] for _ in range(5)]
    for l in range(depth):
        ada = _ada_call(c_all, w_ada[l], b_ada[l][None, :])
        p = (g_norm1[l], g_norm2[l], w_in[l], conv_w[l], conv_b[l], b_igate[l], b_fgate[l], g_head_a[l],
             w_alpha[l], b_alpha[l], g_head_b[l], w_out[l], w_ffn_gate[l], w_ffn_up[l], w_ffn_down[l], g_final)
        st_s = (state_conv[l], state_C[l], state_n[l], state_m[l], state_S[l])
        xp2d, xs2d, st_p, st_s_new = _layer(xp2d, xs2d, st_s, (ada, bs), (ada, 0), p, bp=bp, tp=tp, bs=bs)
        for j in range(5):
            new_p[j].append(st_p[j])
            new_s[j].append(st_s_new[j])
    y_prompt = xp2d.reshape(bp, tp, D_MODEL)
    y_sample = xs2d.reshape(bs, ts, D_MODEL)
    outs_p = [jnp.stack(a, axis=0) for a in new_p]
    outs_s = [jnp.stack(a, axis=0) for a in new_s]
    return (y_prompt, y_sample, *outs_p, *outs_s)
```

```python
import functools

import jax
import jax.numpy as jnp
from jax import lax
from jax.experimental import pallas as pl
from jax.experimental.pallas import tpu as pltpu

f32 = jnp.float32
bf16 = jnp.bfloat16

D_MODEL = 1024
N_HEADS = 4
DK = 128
DV = 256
QK = N_HEADS * DK
CONV_W = 4
GLA_RANK = 16
GLA_TAU = 16.0
D_FF = 2816
EPS = 1e-6
N_ADA = 6

N_BIG = 8192
N_SMALL = 128
OFF_QKA, OFF_VA, OFF_OA, OFF_QB, OFF_KB, OFF_VB, OFF_RB, OFF_GA, OFF_GB = (
    0, 1024, 2048, 3072, 3584, 4096, 5120, 6144, 7168)
LANE_IG, LANE_FG, LANE_LR = 0, N_HEADS, 2 * N_HEADS

SUBLANES = 8
NEG = -1e30
SAFE_LOG_RANGE = 60.0
V7X_VMEM_BYTES = 64 * 1024 * 1024
VMEM_LIMIT_BYTES = V7X_VMEM_BYTES - 8 * 1024 * 1024
FFN_COLS = 256
PROJ_COLS = 1024
PIECE_COLS = 256
DVA = DV + DK


def _mm(a, b):
    return jnp.dot(a, b, preferred_element_type=f32)


def _mm_nt(a, b):
    return lax.dot_general(a, b, (((1,), (1,)), ((), ())), preferred_element_type=f32)


def _mm_tn(a, b):
    return lax.dot_general(a, b, (((0,), (0,)), ((), ())), preferred_element_type=f32)


def _split3(x):
    hi = x.astype(bf16)
    r = x - hi.astype(f32)
    mid = r.astype(bf16)
    lo = (r - mid.astype(f32)).astype(bf16)
    return hi, mid, lo


def _sigmoid(x):
    return 0.5 * jnp.tanh(0.5 * x) + 0.5


def _log_sigmoid(x):
    return jnp.minimum(x, 0.0) - jnp.log(1.0 + jnp.exp(-jnp.abs(x)))


def _rms(x):
    return x * lax.rsqrt(jnp.mean(x * x, axis=-1, keepdims=True) + EPS)


def _row_to_col(row, eye):
    return jnp.sum(jnp.where(eye, row, 0.0), axis=1, keepdims=True)


def _diag_row(mat, eye):
    return jnp.sum(jnp.where(eye, mat, 0.0), axis=0, keepdims=True)


def _mod_rows(mod_ref, bt, tt):
    if bt == 1:
        return mod_ref[0, 0]
    if tt % SUBLANES == 0:
        v = mod_ref[...].reshape(bt, 1, D_MODEL)
        return jnp.broadcast_to(v, (bt, tt, D_MODEL)).reshape(bt * tt, D_MODEL)
    assert 2 * tt == SUBLANES and bt % 2 == 0
    first = lax.broadcasted_iota(jnp.int32, (SUBLANES, 1), 0) < tt
    return jnp.concatenate([jnp.where(first, mod_ref[2 * p, 0], mod_ref[2 * p + 1, 0]) for p in range(bt // 2)],
                           axis=0)


def _spread_tokens(c, lo, n_tok):
    assert 2 * n_tok == SUBLANES
    row = lax.broadcasted_iota(jnp.int32, (SUBLANES, 1), 0)
    mid = (row >= lo) & (row < lo + n_tok)
    tiles = []
    for p in range(c.shape[0] // SUBLANES):
        t = c[p * SUBLANES:(p + 1) * SUBLANES]
        tiles.append(jnp.where(mid, pltpu.roll(t, lo, axis=0), 0.0))
        tiles.append(jnp.where(mid, pltpu.roll(t, (lo + n_tok) % SUBLANES, axis=0), 0.0))
    return jnp.concatenate(tiles, axis=0)


def _gather_tokens(x, lo, n_tok):
    assert 2 * n_tok == SUBLANES
    first = lax.broadcasted_iota(jnp.int32, (SUBLANES, 1), 0) < n_tok
    tiles = []
    for p in range(x.shape[0] // (2 * SUBLANES)):
        even = x[2 * p * SUBLANES:(2 * p + 1) * SUBLANES]
        odd = x[(2 * p + 1) * SUBLANES:(2 * p + 2) * SUBLANES]
        tiles.append(jnp.where(first, pltpu.roll(even, SUBLANES - lo, axis=0),
                               pltpu.roll(odd, (SUBLANES - lo + n_tok) % SUBLANES, axis=0)))
    return jnp.concatenate(tiles, axis=0)


def _ada_body(c_ref, w_ref, b_ref, o_ref):
    o_ref[:, 0, 0, :] = _mm(c_ref[...].astype(bf16), w_ref[...].astype(bf16)) + b_ref[...]


def _ada_call(c_all, w_ada, b_ada):
    m = c_all.shape[0]
    n = w_ada.shape[1]
    return pl.pallas_call(
        _ada_body,
        out_shape=jax.ShapeDtypeStruct((m, n // D_MODEL, 1, D_MODEL), f32),
        grid=(n // D_MODEL,),
        in_specs=[
            pl.BlockSpec((m, D_MODEL), lambda i: (0, 0)),
            pl.BlockSpec((D_MODEL, D_MODEL), lambda i: (0, i)),
            pl.BlockSpec((1, D_MODEL), lambda i: (0, i)),
        ],
        out_specs=pl.BlockSpec((m, 1, 1, D_MODEL), lambda i: (0, i, 0, 0)),
        compiler_params=pltpu.CompilerParams(
            dimension_semantics=("arbitrary",), vmem_limit_bytes=VMEM_LIMIT_BYTES),
        name="ada_proj",
    )(c_all, w_ada, b_ada)


def _inproj_parts(x_ref, shift_ref, scale_ref, g_ref, wbig_ref, wsh_ref, wsl_ref,
                  zb_ref, zs_ref, tail_ref, *, bt, tt):
    rows = bt * tt
    h = _rms(x_ref[...]) * g_ref[...]
    h = h * (1.0 + _mod_rows(scale_ref, bt, tt)) + _mod_rows(shift_ref, bt, tt)
    hh = h.astype(bf16)
    hl = (h - hh.astype(f32)).astype(bf16)

    def narrow():
        wsh = wsh_ref[...]
        rhs = jnp.concatenate([jnp.concatenate([wsh, wsl_ref[...]], axis=1),
                               jnp.concatenate([wsh, jnp.zeros_like(wsh)], axis=1)], axis=0)
        t = _mm(jnp.concatenate([hh, hl], axis=1), rhs)
        zs_ref[...] = t[:, :N_SMALL] + t[:, N_SMALL:]

    def wide(idx, lo=0, width=PROJ_COLS):
        c = idx * PROJ_COLS + lo
        z = _mm(hh, wbig_ref[:, c:c + width])
        if isinstance(zb_ref, (list, tuple)):
            zb_ref[idx][:, lo:lo + width] = z
        else:
            zb_ref[:, c:c + width] = z.astype(zb_ref.dtype)
        if idx == OFF_QKA // PROJ_COLS:
            tail_ref[0, :, lo:lo + width] = z[rows - SUBLANES:rows, :]

    return narrow, wide


def _inproj_body(*refs, bt, tt):
    narrow, wide = _inproj_parts(*refs, bt=bt, tt=tt)
    narrow()
    for idx in range(N_BIG // PROJ_COLS):
        wide(idx)


def _const_spec(shape):
    nd = len(shape)
    return pl.BlockSpec(shape, lambda i, j: (0,) * nd, pipeline_mode=pl.Buffered(1))


def _mod_spec(bt, k, row0):
    assert row0 % bt == 0
    return pl.BlockSpec((bt, 1, 1, D_MODEL), lambda i, j: (row0 // bt + i, k, 0, 0))


def _inproj_call(x2d, ada4, g_norm, w_big, ws_hi, ws_lo, *, nb, nt, bt, tt, zdtype):
    ada, row0 = ada4
    rows = bt * tt
    total = x2d.shape[0]
    row_spec = lambda n: pl.BlockSpec((rows, n), lambda i, j: (i * nt + j, 0))
    return pl.pallas_call(
        functools.partial(_inproj_body, bt=bt, tt=tt),
        out_shape=(
            jax.ShapeDtypeStruct((total, N_BIG), zdtype),
            jax.ShapeDtypeStruct((total, N_SMALL), f32),
            jax.ShapeDtypeStruct((nb * nt, SUBLANES, PROJ_COLS), f32),
        ),
        grid=(nb, nt),
        in_specs=[
            row_spec(D_MODEL), _mod_spec(bt, 0, row0), _mod_spec(bt, 1, row0),
            _const_spec((1, D_MODEL)), _const_spec((D_MODEL, N_BIG)),
            _const_spec((D_MODEL, N_SMALL)), _const_spec((D_MODEL, N_SMALL)),
        ],
        out_specs=(
            row_spec(N_BIG), row_spec(N_SMALL),
            pl.BlockSpec((1, SUBLANES, PROJ_COLS), lambda i, j: (i * nt + j, 0, 0)),
        ),
        compiler_params=pltpu.CompilerParams(
            dimension_semantics=("arbitrary", "arbitrary"), vmem_limit_bytes=VMEM_LIMIT_BYTES),
        name="in_proj",
    )(x2d, ada, ada, g_norm, w_big, ws_hi, ws_lo)


N_MIXER_CONSTS = 7
N_MIXER_INIT = 5
N_MIXER_OUTS = 5


def _mixer_scratch(rows):
    return [pltpu.VMEM((rows, QK), f32)] * 5 + [
        pltpu.VMEM((rows, N_SMALL), f32), pltpu.VMEM((rows, N_SMALL), f32), pltpu.VMEM((rows, QK), f32),
    ] + [pltpu.VMEM((rows, D_MODEL), f32)] * 4 + [
        pltpu.VMEM((SUBLANES, 2 * QK), f32), pltpu.VMEM((N_HEADS, DK, DVA), f32),
    ]


def _mixer_body(*refs, L, vlo, vhi, sample):
    zb_ref, zs_ref = refs[:2]
    k = 2
    init = None
    if sample:
        init = refs[k:k + N_MIXER_INIT]
        k += N_MIXER_INIT
    consts = refs[k:k + N_MIXER_CONSTS]
    k += N_MIXER_CONSTS
    outs = refs[k:k + N_MIXER_OUTS]
    if not sample:
        _mixer_compute(zb_ref, zs_ref, init, consts, outs, refs[k + N_MIXER_OUTS:], L=L, vlo=vlo, vhi=vhi)
        return
    zx_s, zsx_s, ux_s = refs[-3:]
    n_tok = vhi - vlo
    for c in range(0, N_BIG, PROJ_COLS):
        zx_s[:, c:c + PROJ_COLS] = _spread_tokens(zb_ref[:, c:c + PROJ_COLS], vlo, n_tok)
    zsx_s[...] = _spread_tokens(zs_ref[...], vlo, n_tok)
    _mixer_compute(zx_s, zsx_s, init, consts, (ux_s,) + tuple(outs[1:]), refs[k + N_MIXER_OUTS:-3],
                   L=L, vlo=vlo, vhi=vhi)
    outs[0][...] = _gather_tokens(ux_s[...], vlo, n_tok).astype(outs[0].dtype)


def _fused_body(x_ref, shift_ref, scale_ref, g_ref, wbig_ref, wsh_ref, wsl_ref, *refs, L):
    consts = refs[:N_MIXER_CONSTS]
    tail_ref = refs[N_MIXER_CONSTS]
    outs = refs[N_MIXER_CONSTS + 1:N_MIXER_CONSTS + 1 + N_MIXER_OUTS]
    k = N_MIXER_CONSTS + 1 + N_MIXER_OUTS
    zb_s = list(refs[k:k + N_BIG // PROJ_COLS])
    zs_s = refs[k + N_BIG // PROJ_COLS]
    scratch = refs[k + N_BIG // PROJ_COLS + 1:]
    _zero_carried_state(outs, scratch)
    narrow, wide = _inproj_parts(x_ref, shift_ref, scale_ref, g_ref, wbig_ref, wsh_ref, wsl_ref,
                                 zb_s, zs_s, tail_ref, bt=1, tt=L)

    order = (OFF_QKA, OFF_QB, OFF_GA, OFF_OA, OFF_GB, OFF_RB, OFF_VA, OFF_VB)
    pieces = [(off // PROJ_COLS, lo) for off in order for lo in range(0, PROJ_COLS, PIECE_COLS)]
    emitted = []

    def tick(n=1):
        if not emitted:
            narrow()
        for _ in range(n):
            if len(emitted) < len(pieces):
                idx, lo = pieces[len(emitted)]
                wide(idx, lo, PIECE_COLS)
                emitted.append((idx, lo))

    def need(off, width):
        wanted = [(off // PROJ_COLS, lo) for lo in range(0, PROJ_COLS, PIECE_COLS)
                  if lo < off % PROJ_COLS + width and lo + PIECE_COLS > off % PROJ_COLS]
        while any(w not in emitted for w in wanted):
            tick()

    _mixer_compute(zb_s, zs_s, None, consts, outs, scratch, L=L, vlo=0, vhi=L, zero_init=False,
                   hooks=(tick, need))


def _zero_carried_state(outs, scratch):
    _, _, _, mo_ref, So_ref = outs
    carry_s, caug_s = scratch[-2:]

    @pl.when(pl.program_id(1) == 0)
    def _zero():
        caug_s[...] = jnp.zeros(caug_s.shape, f32)
        mo_ref[...] = jnp.zeros(mo_ref.shape, f32)
        So_ref[...] = jnp.zeros(So_ref.shape, f32)
        carry_s[...] = jnp.zeros(carry_s.shape, f32)


def _mixer_compute(zb_ref, zs_ref, init, consts, outs, scratch, *, L, vlo, vhi, zero_init=True,
                   hooks=(lambda n=1: None, lambda off, width: None)):
    sample = init is not None
    if sample:
        conv0_ref, C0_ref, n0_ref, m0_ref, S0_ref = init
    cw_ref, cb_ref, gb_ref, wal_ref, bal_ref, gha_ref, ghb_ref = consts
    u_ref, Co_ref, no_ref, mo_ref, So_ref = outs
    qa_s, ka_s, qh_s, kh_s, kb_s, x_s, r_s, bg_s, ha_s, hb_s, pa_s, pb_s, carry_s, caug_s = scratch
    rows = zs_ref.shape[0]
    every = slice(None)

    tick, need = hooks

    def zcols(rsel, off, width):
        need(off, width)
        if isinstance(zb_ref, (list, tuple)):
            grp, o = divmod(off, PROJ_COLS)
            assert o + width <= PROJ_COLS
            return zb_ref[grp][rsel, o:o + width]
        return zb_ref[rsel, off:off + width]
    n_groups = rows // L
    log2_l = L.bit_length() - 1
    masked = not (vlo == 0 and vhi == L)
    scale = DK ** -0.5

    if not sample and zero_init:
        _zero_carried_state(outs, scratch)

    row_in = lax.broadcasted_iota(jnp.int32, (rows, 1), 0) & (L - 1)
    valid = (row_in >= vlo) & (row_in < vhi)

    cw = cw_ref[...]
    cb = cb_ref[...]
    for c0 in range(0, 2 * QK, PIECE_COLS):
        cs = slice(c0, c0 + PIECE_COLS)
        zq = zcols(every, OFF_QKA + c0, PIECE_COLS).astype(f32)
        if sample:
            zq = jnp.where(row_in < CONV_W - 1, conv0_ref[:, cs], zq)
            carry = jnp.zeros((SUBLANES, PIECE_COLS), f32)
        else:
            carry = carry_s[:, cs]
            carry_s[:, cs] = zq[rows - SUBLANES:rows, :]
        xe = jnp.concatenate([carry, zq], axis=0)
        conv = cb[:, cs] + zq * cw[CONV_W - 1:CONV_W, cs]
        for k in range(CONV_W - 1):
            o = SUBLANES - (CONV_W - 1) + k
            conv = conv + xe[o:o + rows] * cw[k:k + 1, cs]
        qk = conv * _sigmoid(conv)
        if c0 < QK:
            qa_s[:, cs] = qk * scale
        else:
            ka_s[:, c0 - QK:c0 - QK + PIECE_COLS] = qk
        tick()

    lane = lax.broadcasted_iota(jnp.int32, (1, N_SMALL), 1)
    zs = zs_ref[...]
    pre = zs + gb_ref[...]
    gate = jnp.where(lane < LANE_FG, pre, _log_sigmoid(pre))
    gate_f = jnp.where(valid, gate, 0.0) if masked else gate
    rr = lax.broadcasted_iota(jnp.int32, (rows, rows), 0)
    cc = lax.broadcasted_iota(jnp.int32, (rows, rows), 1)
    tril_mask = ((rr >> log2_l) == (cc >> log2_l)) & (cc <= rr)
    tril = jnp.where(tril_mask, 1.0, 0.0).astype(bf16)

    la_lin = _mm(zs.astype(bf16), wal_ref[...]) + bal_ref[...]
    tick(2)

    la_parts, kb_parts = [], []
    for h in range(N_HEADS):
        ks = slice(h * DK, (h + 1) * DK)
        la_h = _log_sigmoid(la_lin[:, ks]) * (1.0 / GLA_TAU)
        la_parts.append(jnp.where(valid, la_h, 0.0) if masked else la_h)
        tick()

    both = jnp.concatenate([gate_f] + la_parts, axis=1)
    wb = N_SMALL + QK
    t = _mm(tril, jnp.concatenate(_split3(both), axis=1))
    csum = t[:, :wb] + (t[:, wb:2 * wb] + t[:, 2 * wb:])
    tick(2)
    x_all = jnp.where(lane < LANE_FG, gate, csum[:, :N_SMALL])
    x_s[...] = x_all
    b_al = pltpu.roll(x_all, N_SMALL - N_HEADS, axis=1)
    r_all = x_all - b_al
    r_ok = (lane < N_HEADS) & valid if masked else jnp.broadcast_to(lane < N_HEADS, (rows, N_SMALL))
    r_s[...] = jnp.where(r_ok, r_all, NEG)
    range_m = jnp.max(jnp.where(r_ok, r_all, NEG)) - jnp.min(jnp.where(r_ok, r_all, -NEG))
    bg = csum[:, N_SMALL:]
    bg_s[...] = bg
    range_g = jnp.max(-bg)
    tick()
    for h in range(N_HEADS):
        ks = slice(h * DK, (h + 1) * DK)
        kb_h = zcols(every, OFF_KB + h * DK, DK).astype(f32)
        kb_parts.append(jnp.where(valid, kb_h, 0.0) if masked else kb_h)
        qh_s[:, ks] = zcols(every, OFF_QB + h * DK, DK).astype(f32) * scale * jnp.exp(bg[:, ks])
        kh_s[:, ks] = kb_parts[h] * jnp.exp(-bg[:, ks])
        tick()
    kb = jnp.concatenate(kb_parts, axis=1)
    fast_ok = jnp.logical_and(range_m <= SAFE_LOG_RANGE, range_g <= SAFE_LOG_RANGE)

    for c0 in range(0, D_MODEL, PIECE_COLS):
        cs = slice(c0, c0 + PIECE_COLS)
        col = lambda off: zcols(every, off + c0, PIECE_COLS).astype(f32)
        pa_s[:, cs] = _sigmoid(col(OFF_GA)) * _sigmoid(col(OFF_OA)) * gha_ref[:, cs]
        tick(2)
    for c0 in range(0, D_MODEL, PIECE_COLS):
        cs = slice(c0, c0 + PIECE_COLS)
        col = lambda off: zcols(every, off + c0, PIECE_COLS).astype(f32)
        rb = col(OFF_RB)
        pb_s[:, cs] = _sigmoid(col(OFF_GB)) * (rb * _sigmoid(rb)) * ghb_ref[:, cs]
        tick(2)
    need(OFF_VA, D_MODEL)
    need(OFF_VB, D_MODEL)

    tri = lax.broadcasted_iota(jnp.int32, (L, L), 0) >= lax.broadcasted_iota(jnp.int32, (L, L), 1)
    eye = lax.broadcasted_iota(jnp.int32, (DK, DK), 0) == lax.broadcasted_iota(jnp.int32, (DK, DK), 1)
    lane_h = lane
    lrow = lax.broadcasted_iota(jnp.int32, (1, L), 1)
    lcol = lax.broadcasted_iota(jnp.int32, (L, 1), 0)
    vrow = (lrow >= vlo) & (lrow < vhi)
    ones16 = jnp.ones((L, DK), bf16)

    def load_state(g, h):
        if sample:
            n_mat = jnp.broadcast_to(_row_to_col(n0_ref[g, pl.ds(h, 1), :], eye), (DK, DK))
            return jnp.concatenate([C0_ref[g, h], n_mat], axis=1), S0_ref[g, h]
        return caug_s[h], So_ref[0, h]

    def store_state(g, h, caug, s_new):
        sidx = g if sample else 0
        if not sample:
            caug_s[h] = caug
        Co_ref[sidx, h] = caug[:, :DV]
        no_ref[sidx, pl.ds(h, 1), :] = _diag_row(caug[:, DV:], eye)
        So_ref[sidx, h] = s_new

    def group_fast(g):
        r0 = g * L
        rs = pl.ds(r0, L)
        xc = x_s[rs, :]
        rc = r_s[rs, :]
        m_prev = m0_ref[g] if sample else mo_ref[0]
        bg_tail = bg_s[pl.ds(r0 + L - SUBLANES, SUBLANES), :]
        heads = []
        for h in range(N_HEADS):
            ks = slice(h * DK, (h + 1) * DK)
            caug, s_st = load_state(g, h)
            c0 = jnp.max(rc[:, h:h + 1], axis=0, keepdims=True)
            q16 = qa_s[rs, ks].astype(bf16)
            kt16 = (ka_s[rs, ks] * jnp.exp(rc[:, h:h + 1] - c0)).astype(bf16)
            qh16 = qh_s[rs, ks].astype(bf16)
            kh = kh_s[rs, ks]
            heads.append(dict(
                caug=caug, s_st=s_st, c0=c0, kt16=kt16, kh=kh,
                va=jnp.concatenate([zcols(rs, OFF_VA + h * DV, DV).astype(bf16), ones16], axis=1),
                vb16=zcols(rs, OFF_VB + h * DV, DV).astype(bf16),
                s_raw=_mm_nt(q16, kt16), a_raw=_mm_nt(qh16, kh.astype(bf16)),
                ci=_mm(q16, caug.astype(bf16)), oi=_mm(qh16, s_st.astype(bf16))))
        for p in heads:
            p["ni"] = _mm(jnp.where(tri, p["s_raw"], 0.0).astype(bf16), p["va"])
            p["o"] = _mm(jnp.where(tri, p["a_raw"], 0.0).astype(bf16), p["vb16"]) + p["oi"]
        for h, p in enumerate(heads):
            p["eb_last"] = jnp.exp(bg_tail[SUBLANES - 1:SUBLANES, h * DK:(h + 1) * DK])
            p["kv"] = _mm_tn(p["kt16"], p["va"])
            p["skv"] = _mm_tn((p["kh"] * p["eb_last"]).astype(bf16), p["vb16"])
        m_vec = m_prev
        for h, p in enumerate(heads):
            m_st = m_prev[:, h:h + 1]
            b_col = xc[:, LANE_FG + h:LANE_FG + h + 1]
            b_last = b_col[L - 1:L, :]
            c0 = p["c0"]
            mm = jnp.maximum(c0, m_st)
            comb = jnp.exp(c0 - mm) * p["ni"] + jnp.exp(m_st - mm) * p["ci"]
            inv = 1.0 / jnp.maximum(jnp.abs(comb[:, DV:]), jnp.exp(-(b_col + mm)))
            ha_s[rs, h * DV:h * DV + DK] = comb[:, :DK] * inv
            ha_s[rs, h * DV + DK:(h + 1) * DV] = comb[:, DK:DV] * inv
            hb_s[rs, h * DV:(h + 1) * DV] = p["o"]
            m_new = jnp.maximum(b_last + m_st, b_last + c0)
            caug_new = jnp.exp(b_last + m_st - m_new) * p["caug"] + jnp.exp(b_last + c0 - m_new) * p["kv"]
            s_new = _row_to_col(p["eb_last"], eye) * p["s_st"] + p["skv"]
            store_state(g, h, caug_new, s_new)
            m_vec = jnp.where(lane_h == h, m_new, m_vec)
        mo_ref[g if sample else 0] = m_vec

    def group_exact(g):
        static = isinstance(g, int)
        r0 = g * L if static else pl.multiple_of(g * L, L)
        tail0 = r0 + L - SUBLANES if static else pl.multiple_of(r0 + L - SUBLANES, SUBLANES)
        rs = pl.ds(r0, L)
        xc = x_s[rs, :]
        rc = r_s[rs, :]
        sr = lax.broadcasted_iota(jnp.int32, (2 * SUBLANES, N_SMALL), 0)
        sc = lax.broadcasted_iota(jnp.int32, (2 * SUBLANES, N_SMALL), 1)
        sel = jnp.where((sr < N_HEADS) & (sc == sr + LANE_IG), 1.0,
                        jnp.where((sr < N_HEADS) & (sc == sr + LANE_FG), -1.0, 0.0)).astype(bf16)
        xh, xm, xl = _split3(xc)
        r_rows = _mm_nt(sel, xh) + (_mm_nt(sel, xm) + _mm_nt(sel, xl))
        m_prev = m0_ref[g] if sample else mo_ref[0]
        m_vec = m_prev
        bg_tail = bg_s[pl.ds(tail0, SUBLANES), :]
        for h in range(N_HEADS):
            ks = slice(h * DK, (h + 1) * DK)
            vs = slice(h * DV, (h + 1) * DV)
            caug, s_st = load_state(g, h)
            m_st = m_prev[:, h:h + 1]
            b_col = xc[:, LANE_FG + h:LANE_FG + h + 1]
            r_col = rc[:, h:h + 1]
            b_last = b_col[L - 1:L, :]
            v16 = zcols(rs, OFF_VA + h * DV, DV).astype(bf16)
            vb16 = zcols(rs, OFF_VB + h * DV, DV).astype(bf16)
            eb_last = jnp.exp(bg_tail[SUBLANES - 1:SUBLANES, ks])
            eb_col = _row_to_col(eb_last, eye)
            c_st = caug[:, :DV]
            n_row = _diag_row(caug[:, DV:], eye)
            r_row = r_rows[h:h + 1, :]
            if masked:
                r_row = jnp.where(vrow, r_row, NEG)
            dm = jnp.where(tri, b_col + r_row, NEG)
            m_t = jnp.maximum(jnp.max(dm, axis=1, keepdims=True), b_col + m_st)
            w = jnp.exp(dm - m_t)
            wi = jnp.exp(b_col + m_st - m_t)
            q = qa_s[rs, ks]
            k = ka_s[rs, ks]
            q16 = q.astype(bf16)
            s = _mm_nt(q16, k.astype(bf16)) * w
            num = _mm(s.astype(bf16), v16) + wi * _mm(q16, c_st.astype(bf16))
            den = jnp.sum(s, axis=1, keepdims=True) + wi * jnp.sum(q * n_row, axis=1, keepdims=True)
            ha_s[rs, vs] = num / jnp.maximum(jnp.abs(den), jnp.exp(-m_t))
            m_new = jnp.maximum(b_last + m_st, b_last + jnp.max(r_row, axis=1, keepdims=True))
            wc = jnp.exp(b_last + m_st - m_new)
            kt = k * jnp.exp(b_last + r_col - m_new)
            n_new = wc * n_row + jnp.sum(kt, axis=0, keepdims=True)
            caug_new = jnp.concatenate(
                [wc * c_st + _mm_tn(kt.astype(bf16), v16),
                 jnp.broadcast_to(_row_to_col(n_new, eye), (DK, DK))], axis=1)
            qb = zcols(rs, OFF_QB + h * DK, DK).astype(f32) * scale
            bgc = bg_s[rs, ks]
            sub = lax.broadcasted_iota(jnp.int32, (SUBLANES, 1), 0)

            def score_col(s_i, a, ks=ks, qb=qb, bgc=bgc, sub=sub):
                s8 = lax.shift_left(lax.shift_right_logical(s_i, 3), 3)
                base = pl.multiple_of(r0 + s8, SUBLANES)
                pick = sub == (s_i - s8)
                b_s = jnp.sum(jnp.where(pick, bg_s[pl.ds(base, SUBLANES), ks], 0.0), axis=0, keepdims=True)
                k_s = jnp.sum(jnp.where(pick, kb_s[pl.ds(base, SUBLANES), ks], 0.0), axis=0, keepdims=True)
                e = jnp.exp(jnp.where(lcol >= s_i, bgc - b_s, NEG))
                col = jnp.sum(qb * e * k_s, axis=1, keepdims=True)
                return jnp.where(lrow == s_i, col, a)

            a = lax.fori_loop(0, L, score_col, jnp.zeros((L, L), f32))
            hb_s[rs, vs] = (_mm(a.astype(bf16), vb16)
                            + _mm((qb * jnp.exp(bgc)).astype(bf16), s_st.astype(bf16)))
            kt_b = kb_s[rs, ks] * jnp.exp(bg_tail[SUBLANES - 1:SUBLANES, ks] - bgc)
            s_new = eb_col * s_st + _mm_tn(kt_b.astype(bf16), vb16)
            store_state(g, h, caug_new, s_new)
            m_vec = jnp.where(lane_h == h, m_new, m_vec)
        mo_ref[g if sample else 0] = m_vec

    def groups_fast_batched():
        r_all_m = r_s[...]
        c0_l, m_l, mm_l, fac, wc = [], [], [], [], []
        for g in range(n_groups):
            gs = slice(g * L, (g + 1) * L)
            c0_g = jnp.max(r_all_m[gs], axis=0, keepdims=True)
            m_g = m0_ref[g]
            b_last = b_al[gs][L - 1:L, :]
            m_new = jnp.maximum(b_last + m_g, b_last + c0_g)
            mo_ref[g] = m_new
            c0_l.append(c0_g)
            m_l.append(m_g)
            mm_l.append(jnp.maximum(c0_g, m_g))
            fac.append(jnp.exp(b_last + c0_g - m_new))
            wc.append(jnp.exp(b_last + m_g - m_new))
        rows_of = lambda lst: jnp.concatenate([jnp.broadcast_to(v, (L, N_SMALL)) for v in lst], axis=0)
        c0_r, m_r, mm_r = rows_of(c0_l), rows_of(m_l), rows_of(mm_l)
        beta = jnp.exp(r_all_m - c0_r)
        gi_r = jnp.exp(c0_r - mm_r)
        wi_r = jnp.exp(m_r - mm_r)
        floor = jnp.exp(-(b_al + mm_r))
        ones_g = jnp.ones((L, DK), bf16)
        for h in range(N_HEADS):
            ks = slice(h * DK, (h + 1) * DK)
            vs = slice(h * DV, (h + 1) * DV)
            kt = ka_s[:, ks] * beta[:, h:h + 1]
            v = zcols(every, OFF_VA + h * DV, DV)
            vb = zcols(every, OFF_VB + h * DV, DV)
            q16 = qa_s[:, ks].astype(bf16)
            va = jnp.concatenate([v.astype(bf16), jnp.ones((rows, DK), bf16)], axis=1)
            s16 = jnp.where(tril_mask, _mm_nt(q16, kt.astype(bf16)), 0.0).astype(bf16)
            ni = _mm(s16, va)
            qh16 = qh_s[:, ks].astype(bf16)
            kh = kh_s[:, ks]
            a16 = jnp.where(tril_mask, _mm_nt(qh16, kh.astype(bf16)), 0.0).astype(bf16)
            o_intra = _mm(a16, vb.astype(bf16))
            ci_parts, oi_parts = [], []
            for g in range(n_groups):
                gs = slice(g * L, (g + 1) * L)
                caug, s_st = load_state(g, h)
                va_g = jnp.concatenate([v[gs].astype(bf16), ones_g], axis=1)
                ci_parts.append(_mm(qa_s[gs, ks].astype(bf16), caug.astype(bf16)))
                caug_new = wc[g][:, h:h + 1] * caug + fac[g][:, h:h + 1] * _mm_tn(kt[gs].astype(bf16), va_g)
                oi_parts.append(_mm(qh_s[gs, ks].astype(bf16), s_st.astype(bf16)))
                eb_last = jnp.exp(bg_s[gs, ks][L - 1:L, :])
                s_new = (_row_to_col(eb_last, eye) * s_st
                         + _mm_tn((kh[gs] * eb_last).astype(bf16), vb[gs].astype(bf16)))
                store_state(g, h, caug_new, s_new)
            comb = gi_r[:, h:h + 1] * ni + wi_r[:, h:h + 1] * jnp.concatenate(ci_parts, axis=0)
            inv = 1.0 / jnp.maximum(jnp.abs(comb[:, DV:]), floor[:, h:h + 1])
            ha_s[:, h * DV:h * DV + DK] = comb[:, :DK] * inv
            ha_s[:, h * DV + DK:(h + 1) * DV] = comb[:, DK:DV] * inv
            hb_s[:, vs] = o_intra + jnp.concatenate(oi_parts, axis=0)

    def run_groups(exact):
        if not exact:
            if sample and L == SUBLANES:
                groups_fast_batched()
            else:
                for g in range(n_groups):
                    group_fast(g)
        elif n_groups == 1:
            group_exact(0)
        else:
            def body(g, carry_):
                group_exact(g)
                return carry_
            lax.fori_loop(0, n_groups, body, 0)

    @pl.when(fast_ok)
    def _fast():
        run_groups(False)

    @pl.when(jnp.logical_not(fast_ok))
    def _exact():
        kb_s[...] = kb
        run_groups(True)

    for h in range(N_HEADS):
        vs = slice(h * DV, (h + 1) * DV)
        u = pa_s[:, vs] * _rms(ha_s[:, vs]) + pb_s[:, vs] * _rms(hb_s[:, vs])
        u_ref[:, vs] = u.astype(u_ref.dtype)


def _mixer_call(zb, zs, init, consts, *, nb, nt, bt, L, vlo, vhi):
    sample = init is not None
    rows = bt * L
    tok_rows = bt * (vhi - vlo) if sample else rows
    n_state = nb * bt
    row_spec = lambda n, r=rows: pl.BlockSpec((r, n), lambda i, j: (i * nt + j, 0))
    st4 = pl.BlockSpec((bt, N_HEADS, DK, DV), lambda i, j: (i, 0, 0, 0))
    st_n = pl.BlockSpec((bt, N_HEADS, DK), lambda i, j: (i, 0, 0))
    st_m = pl.BlockSpec((bt, 1, N_SMALL), lambda i, j: (i, 0, 0))
    in_specs = [row_spec(N_BIG, tok_rows), row_spec(N_SMALL, tok_rows)]
    args = [zb, zs]
    scratch = _mixer_scratch(rows)
    if sample:
        in_specs += [row_spec(2 * QK), st4, st_n, st_m, st4]
        args += list(init)
        scratch += [pltpu.VMEM((rows, N_BIG), f32), pltpu.VMEM((rows, N_SMALL), f32),
                    pltpu.VMEM((rows, D_MODEL), f32)]
    in_specs += [_const_spec(c.shape) for c in consts]
    args += list(consts)
    return pl.pallas_call(
        functools.partial(_mixer_body, L=L, vlo=vlo, vhi=vhi, sample=sample),
        out_shape=(
            jax.ShapeDtypeStruct((zb.shape[0], D_MODEL), bf16),
            jax.ShapeDtypeStruct((n_state, N_HEADS, DK, DV), f32),
            jax.ShapeDtypeStruct((n_state, N_HEADS, DK), f32),
            jax.ShapeDtypeStruct((n_state, 1, N_SMALL), f32),
            jax.ShapeDtypeStruct((n_state, N_HEADS, DK, DV), f32),
        ),
        grid=(nb, nt),
        in_specs=in_specs,
        out_specs=(row_spec(D_MODEL, tok_rows), st4, st_n, st_m, st4),
        scratch_shapes=scratch,
        compiler_params=pltpu.CompilerParams(
            dimension_semantics=("arbitrary", "arbitrary"), vmem_limit_bytes=VMEM_LIMIT_BYTES),
        name="mixer_sample" if sample else "mixer_prompt",
    )(*args)


def _fused_call(x2d, ada4, g_norm, w_big, ws_hi, ws_lo, consts, *, nb, nt, L):
    ada, row0 = ada4
    row_spec = lambda n: pl.BlockSpec((L, n), lambda i, j: (i * nt + j, 0))
    st4 = pl.BlockSpec((1, N_HEADS, DK, DV), lambda i, j: (i, 0, 0, 0))
    st_n = pl.BlockSpec((1, N_HEADS, DK), lambda i, j: (i, 0, 0))
    st_m = pl.BlockSpec((1, 1, N_SMALL), lambda i, j: (i, 0, 0))
    return pl.pallas_call(
        functools.partial(_fused_body, L=L),
        out_shape=(
            jax.ShapeDtypeStruct((nb * nt, SUBLANES, PROJ_COLS), f32),
            jax.ShapeDtypeStruct((x2d.shape[0], D_MODEL), bf16),
            jax.ShapeDtypeStruct((nb, N_HEADS, DK, DV), f32),
            jax.ShapeDtypeStruct((nb, N_HEADS, DK), f32),
            jax.ShapeDtypeStruct((nb, 1, N_SMALL), f32),
            jax.ShapeDtypeStruct((nb, N_HEADS, DK, DV), f32),
        ),
        grid=(nb, nt),
        in_specs=[
            row_spec(D_MODEL), _mod_spec(1, 0, row0), _mod_spec(1, 1, row0),
            _const_spec((1, D_MODEL)), _const_spec((D_MODEL, N_BIG)),
            _const_spec((D_MODEL, N_SMALL)), _const_spec((D_MODEL, N_SMALL)),
        ] + [_const_spec(c.shape) for c in consts],
        out_specs=(
            pl.BlockSpec((1, SUBLANES, PROJ_COLS), lambda i, j: (i * nt + j, 0, 0)),
            row_spec(D_MODEL), st4, st_n, st_m, st4,
        ),
        scratch_shapes=([pltpu.VMEM((L, PROJ_COLS), f32)] * (N_BIG // PROJ_COLS)
                        + [pltpu.VMEM((L, N_SMALL), f32)] + _mixer_scratch(L)),
        compiler_params=pltpu.CompilerParams(
            dimension_semantics=("arbitrary", "arbitrary"), vmem_limit_bytes=VMEM_LIMIT_BYTES),
        name="proj_mixer_prompt",
    )(x2d, ada, ada, g_norm, w_big, ws_hi, ws_lo, *consts)


def _out_body(u_ref, x_ref, gate1_ref, shift2_ref, scale2_ref, gate2_ref, gn2_ref, gfin_ref,
              wout_ref, wg_ref, wu_ref, wd_ref, y_ref, *, bt, tt):
    x1 = x_ref[...] + _mod_rows(gate1_ref, bt, tt) * _mm(u_ref[...], wout_ref[...])
    h2 = _rms(x1) * gn2_ref[...]
    h2 = (h2 * (1.0 + _mod_rows(scale2_ref, bt, tt)) + _mod_rows(shift2_ref, bt, tt)).astype(bf16)
    acc = jnp.zeros(x1.shape, f32)
    for c in range(0, D_FF, FFN_COLS):
        g = _mm(h2, wg_ref[:, c:c + FFN_COLS])
        up = _mm(h2, wu_ref[:, c:c + FFN_COLS])
        acc = acc + _mm((g * _sigmoid(g) * up).astype(bf16), wd_ref[c:c + FFN_COLS, :])
    x2 = x1 + _mod_rows(gate2_ref, bt, tt) * acc
    y_ref[...] = _rms(x2) * gfin_ref[...]


def _out_call(u2d, x2d, ada4, g_norm2, g_final, w_out, w_gate, w_up, w_down, *, nb, nt, bt, tt):
    ada, row0 = ada4
    rows = bt * tt
    row_spec = pl.BlockSpec((rows, D_MODEL), lambda i, j: (i * nt + j, 0))
    return pl.pallas_call(
        functools.partial(_out_body, bt=bt, tt=tt),
        out_shape=jax.ShapeDtypeStruct(x2d.shape, f32),
        grid=(nb, nt),
        in_specs=[
            row_spec, row_spec,
            _mod_spec(bt, 2, row0), _mod_spec(bt, 3, row0), _mod_spec(bt, 4, row0), _mod_spec(bt, 5, row0),
            _const_spec((1, D_MODEL)), _const_spec((1, D_MODEL)),
            _const_spec((D_MODEL, D_MODEL)), _const_spec((D_MODEL, D_FF)),
            _const_spec((D_MODEL, D_FF)), _const_spec((D_FF, D_MODEL)),
        ],
        out_specs=row_spec,
        compiler_params=pltpu.CompilerParams(
            dimension_semantics=("arbitrary", "arbitrary"), vmem_limit_bytes=VMEM_LIMIT_BYTES),
        name="out_ffn",
    )(u2d, x2d, ada, ada, ada, ada, g_norm2, g_final, w_out, w_gate, w_up, w_down)


PROMPT_CHUNK = 256
PROMPT_FFN_ROWS = 512
SAMPLE_L = SUBLANES
SAMPLE_MIX_GROUPS = 8
SAMPLE_DENSE_GROUPS = 64


W_IN_SEG_A = (0, 3072)
W_IN_GATES = 3072
W_IN_SEG_B = (3080, 6152)
W_IN_LR = 6152
W_IN_SEG_C = (6168, 8216)
LANES = 128


PREP_COLS = 512
SHIFT_B = W_IN_SEG_B[0] - OFF_QB
SHIFT_C = W_IN_SEG_C[0] - OFF_GA
PREP_HALO = 32


def _prep_body(a_ref, b_ref, gt_ref, lt_ref, wb_ref, wh_ref, wl_ref):
    j = pl.program_id(0)
    both = jnp.concatenate([a_ref[...], b_ref[...]], axis=0)

    def emit(shift):
        wb_ref[...] = both[shift:shift + PREP_COLS, :].T.astype(bf16)

    pl.when(j < OFF_QB // PREP_COLS)(lambda: emit(0))
    pl.when((j >= OFF_QB // PREP_COLS) & (j < OFF_GA // PREP_COLS))(lambda: emit(SHIFT_B))
    pl.when(j >= OFF_GA // PREP_COLS)(lambda: emit(SHIFT_C))

    @pl.when(j == 0)
    def _narrow():
        row = lax.broadcasted_iota(jnp.int32, (LANES, 1), 0)
        small = jnp.where(row < LANE_LR, gt_ref[...], jnp.where(row < LANE_LR + GLA_RANK, lt_ref[...], 0.0)).T
        hi = small.astype(bf16)
        wh_ref[...] = hi
        wl_ref[...] = (small - hi.astype(f32)).astype(bf16)


def _prep_call(w_in):
    assert W_IN_GATES % LANES == 0 and (W_IN_LR - LANE_LR) % LANES == 0
    assert max(SHIFT_B, SHIFT_C) <= PREP_HALO and SHIFT_B % SUBLANES == 0 and SHIFT_C % SUBLANES == 0
    wt = jnp.transpose(w_in)
    halo_per_block = PREP_COLS // PREP_HALO
    last = (wt.shape[0] - 1) // PREP_HALO
    const = lambda shape: pl.BlockSpec(shape, lambda j: (0, 0))
    return pl.pallas_call(
        _prep_body,
        out_shape=(
            jax.ShapeDtypeStruct((D_MODEL, N_BIG), bf16),
            jax.ShapeDtypeStruct((D_MODEL, N_SMALL), bf16),
            jax.ShapeDtypeStruct((D_MODEL, N_SMALL), bf16),
        ),
        grid=(N_BIG // PREP_COLS,),
        in_specs=[
            pl.BlockSpec((PREP_COLS, D_MODEL), lambda j: (j, 0)),
            pl.BlockSpec((PREP_HALO, D_MODEL), lambda j: (jnp.minimum((j + 1) * halo_per_block, last), 0)),
            pl.BlockSpec((LANES, D_MODEL), lambda j: (W_IN_GATES // LANES, 0)),
            pl.BlockSpec((LANES, D_MODEL), lambda j: ((W_IN_LR - LANE_LR) // LANES, 0)),
        ],
        out_specs=(
            pl.BlockSpec((D_MODEL, PREP_COLS), lambda j: (0, j)),
            const((D_MODEL, N_SMALL)), const((D_MODEL, N_SMALL)),
        ),
        compiler_params=pltpu.CompilerParams(
            dimension_semantics=("arbitrary",), vmem_limit_bytes=VMEM_LIMIT_BYTES),
        name="w_in_prep",
    )(wt, wt, wt, wt)


def _prep_layer_weights(w_in, w_alpha, b_igate, b_fgate):
    w_big, ws_hi, ws_lo = _prep_call(w_in)
    w_al = jnp.zeros((N_SMALL, QK), f32).at[LANE_LR:LANE_LR + GLA_RANK].set(w_alpha).astype(bf16)
    g_bias = jnp.concatenate([b_igate, b_fgate, jnp.zeros((N_SMALL - 2 * N_HEADS,), f32)])[None, :]
    return w_big, ws_hi, ws_lo, w_al, g_bias


def _layer(xp2d, xs2d, st_s, ada_p, ada_s, p, *, bp, tp, bs):
    (g_norm1, g_norm2, w_in, conv_w, conv_b, b_igate, b_fgate, g_head_a,
     w_alpha, b_alpha, g_head_b, w_out, w_ffn_gate, w_ffn_up, w_ffn_down, g_final_or_ones) = p
    w_big, ws_hi, ws_lo, w_al, g_bias = _prep_layer_weights(w_in, w_alpha, b_igate, b_fgate)
    consts = (conv_w, conv_b[None, :], g_bias, w_al, b_alpha[None, :],
              g_head_a.reshape(1, D_MODEL), g_head_b.reshape(1, D_MODEL))
    w_out16, w_g16, w_u16, w_d16 = (w.astype(bf16) for w in (w_out, w_ffn_gate, w_ffn_up, w_ffn_down))
    gn1, gn2, gfin = g_norm1[None, :], g_norm2[None, :], g_final_or_ones[None, :]

    ntp = tp // PROMPT_CHUNK
    tail, u, c_p, n_p, m_p, s_p = _fused_call(xp2d, ada_p, gn1, w_big, ws_hi, ws_lo, consts,
                                              nb=bp, nt=ntp, L=PROMPT_CHUNK)
    conv_p = tail.reshape(bp, ntp, SUBLANES, 2 * QK)[:, ntp - 1, SUBLANES - (CONV_W - 1):, :]
    yp = _out_call(u, xp2d, ada_p, gn2, gfin, w_out16, w_g16, w_u16, w_d16,
                   nb=bp, nt=tp // PROMPT_FFN_ROWS, bt=1, tt=PROMPT_FFN_ROWS)

    conv0, c0, n0, m0, s0 = st_s
    lo = CONV_W - 1
    n_tok = SAMPLE_L - lo - 1
    nbd = bs // SAMPLE_DENSE_GROUPS
    zb_s, zs_s, _ = _inproj_call(xs2d, ada_s, gn1, w_big, ws_hi, ws_lo,
                                 nb=nbd, nt=1, bt=SAMPLE_DENSE_GROUPS, tt=n_tok, zdtype=f32)
    conv_s = zb_s[:, OFF_QKA:OFF_QKA + 2 * QK].reshape(bs, n_tok, 2 * QK)[:, n_tok - lo:, :]
    conv0_rows = jnp.pad(conv0, ((0, 0), (0, SAMPLE_L - lo), (0, 0))).reshape(bs * SAMPLE_L, 2 * QK)
    m0_lanes = jnp.pad(m0[:, None, :], ((0, 0), (0, 0), (0, N_SMALL - N_HEADS)))
    u_s, c_s, n_s, m_s, s_s = _mixer_call(
        zb_s, zs_s, (conv0_rows, c0, n0, m0_lanes, s0), consts,
        nb=bs // SAMPLE_MIX_GROUPS, nt=1, bt=SAMPLE_MIX_GROUPS, L=SAMPLE_L, vlo=lo, vhi=lo + n_tok)
    ys = _out_call(u_s, xs2d, ada_s, gn2, gfin, w_out16, w_g16, w_u16, w_d16,
                   nb=nbd, nt=1, bt=SAMPLE_DENSE_GROUPS, tt=n_tok)
    st_p = (conv_p, c_p, n_p, m_p[:, 0, :N_HEADS], s_p)
    st_s_new = (conv_s, c_s, n_s, m_s[:, 0, :N_HEADS], s_s)
    return yp, ys, st_p, st_s_new


def kernel(x_prompt, x_sample, state_conv, state_C, state_n, state_m, state_S, c_prompt, c_sample,
           g_norm1, g_norm2, w_ada, b_ada, w_in, conv_w, conv_b, b_igate, b_fgate, g_head_a,
           w_alpha, b_alpha, g_head_b, w_out, w_ffn_gate, w_ffn_up, w_ffn_down, g_final):
    bp, tp, _ = x_prompt.shape
    bs, ts, _ = x_sample.shape
    depth = w_in.shape[0]
    assert depth == 1, "the final norm is fused into the (single) layer's FFN kernel"
    lo = CONV_W - 1
    assert ts == SAMPLE_L - lo - 1
    xp2d = x_prompt.reshape(bp * tp, D_MODEL)
    xs2d = x_sample.reshape(bs * ts, D_MODEL)
    c_all = jnp.concatenate([c_sample, c_prompt], axis=0)
    new_p = [[] for _ in range(5)]
    new_s = [[] for _ in range(5)]
    for l in range(depth):
        ada = _ada_call(c_all, w_ada[l], b_ada[l][None, :])
        p = (g_norm1[l], g_norm2[l], w_in[l], conv_w[l], conv_b[l], b_igate[l], b_fgate[l], g_head_a[l],
             w_alpha[l], b_alpha[l], g_head_b[l], w_out[l], w_ffn_gate[l], w_ffn_up[l], w_ffn_down[l], g_final)
        st_s = (state_conv[l], state_C[l], state_n[l], state_m[l], state_S[l])
        xp2d, xs2d, st_p, st_s_new = _layer(xp2d, xs2d, st_s, (ada, bs), (ada, 0), p, bp=bp, tp=tp, bs=bs)
        for j in range(5):
            new_p[j].append(st_p[j])
            new_s[j].append(st_s_new[j])
    y_prompt = xp2d.reshape(bp, tp, D_MODEL)
    y_sample = xs2d.reshape(bs, ts, D_MODEL)
    outs_p = [jnp.stack(a, axis=0) for a in new_p]
    outs_s = [jnp.stack(a, axis=0) for a in new_s]
    return (y_prompt, y_sample, *outs_p, *outs_s)
```

```python
import functools

import jax
import jax.numpy as jnp
from jax import lax
from jax.experimental import pallas as pl
from jax.experimental.pallas import tpu as pltpu

f32 = jnp.float32
bf16 = jnp.bfloat16

D_MODEL = 1024
N_HEADS = 4
DK = 128
DV = 256
QK = N_HEADS * DK
CONV_W = 4
GLA_RANK = 16
GLA_TAU = 16.0
D_FF = 2816
EPS = 1e-6
N_ADA = 6

N_BIG = 8192
N_SMALL = 128
OFF_QKA, OFF_VA, OFF_OA, OFF_QB, OFF_KB, OFF_VB, OFF_RB, OFF_GA, OFF_GB = (
    0, 1024, 2048, 3072, 3584, 4096, 5120, 6144, 7168)
LANE_IG, LANE_FG, LANE_LR = 0, N_HEADS, 2 * N_HEADS

SUBLANES = 8
NEG = -1e30
SAFE_LOG_RANGE = 60.0
V7X_VMEM_BYTES = 64 * 1024 * 1024
VMEM_LIMIT_BYTES = V7X_VMEM_BYTES - 8 * 1024 * 1024
FFN_COLS = 256
PROJ_COLS = 1024
PIECE_COLS = 256
DVA = DV + DK


def _mm(a, b):
    return jnp.dot(a, b, preferred_element_type=f32)


def _mm_nt(a, b):
    return lax.dot_general(a, b, (((1,), (1,)), ((), ())), preferred_element_type=f32)


def _mm_tn(a, b):
    return lax.dot_general(a, b, (((0,), (0,)), ((), ())), preferred_element_type=f32)


def _split3(x):
    hi = x.astype(bf16)
    r = x - hi.astype(f32)
    mid = r.astype(bf16)
    lo = (r - mid.astype(f32)).astype(bf16)
    return hi, mid, lo


def _sigmoid(x):
    return 0.5 * jnp.tanh(0.5 * x) + 0.5


def _log_sigmoid(x):
    return jnp.minimum(x, 0.0) - jnp.log(1.0 + jnp.exp(-jnp.abs(x)))


def _rms(x):
    return x * lax.rsqrt(jnp.mean(x * x, axis=-1, keepdims=True) + EPS)


def _row_to_col(row, eye):
    return jnp.sum(jnp.where(eye, row, 0.0), axis=1, keepdims=True)


def _diag_row(mat, eye):
    return jnp.sum(jnp.where(eye, mat, 0.0), axis=0, keepdims=True)


def _mod_rows(mod_ref, bt, tt):
    if bt == 1:
        return mod_ref[0, 0]
    if tt % SUBLANES == 0:
        v = mod_ref[...].reshape(bt, 1, D_MODEL)
        return jnp.broadcast_to(v, (bt, tt, D_MODEL)).reshape(bt * tt, D_MODEL)
    assert 2 * tt == SUBLANES and bt % 2 == 0
    first = lax.broadcasted_iota(jnp.int32, (SUBLANES, 1), 0) < tt
    return jnp.concatenate([jnp.where(first, mod_ref[2 * p, 0], mod_ref[2 * p + 1, 0]) for p in range(bt // 2)],
                           axis=0)


def _spread_tokens(c, lo, n_tok):
    assert 2 * n_tok == SUBLANES
    row = lax.broadcasted_iota(jnp.int32, (SUBLANES, 1), 0)
    mid = (row >= lo) & (row < lo + n_tok)
    tiles = []
    for p in range(c.shape[0] // SUBLANES):
        t = c[p * SUBLANES:(p + 1) * SUBLANES]
        tiles.append(jnp.where(mid, pltpu.roll(t, lo, axis=0), 0.0))
        tiles.append(jnp.where(mid, pltpu.roll(t, (lo + n_tok) % SUBLANES, axis=0), 0.0))
    return jnp.concatenate(tiles, axis=0)


def _gather_tokens(x, lo, n_tok):
    assert 2 * n_tok == SUBLANES
    first = lax.broadcasted_iota(jnp.int32, (SUBLANES, 1), 0) < n_tok
    tiles = []
    for p in range(x.shape[0] // (2 * SUBLANES)):
        even = x[2 * p * SUBLANES:(2 * p + 1) * SUBLANES]
        odd = x[(2 * p + 1) * SUBLANES:(2 * p + 2) * SUBLANES]
        tiles.append(jnp.where(first, pltpu.roll(even, SUBLANES - lo, axis=0),
                               pltpu.roll(odd, (SUBLANES - lo + n_tok) % SUBLANES, axis=0)))
    return jnp.concatenate(tiles, axis=0)


def _ada_body(c_ref, w_ref, b_ref, o_ref):
    o_ref[:, 0, 0, :] = _mm(c_ref[...].astype(bf16), w_ref[...].astype(bf16)) + b_ref[...]


def _ada_call(c_all, w_ada, b_ada):
    m = c_all.shape[0]
    n = w_ada.shape[1]
    return pl.pallas_call(
        _ada_body,
        out_shape=jax.ShapeDtypeStruct((m, n // D_MODEL, 1, D_MODEL), f32),
        grid=(n // D_MODEL,),
        in_specs=[
            pl.BlockSpec((m, D_MODEL), lambda i: (0, 0)),
            pl.BlockSpec((D_MODEL, D_MODEL), lambda i: (0, i)),
            pl.BlockSpec((1, D_MODEL), lambda i: (0, i)),
        ],
        out_specs=pl.BlockSpec((m, 1, 1, D_MODEL), lambda i: (0, i, 0, 0)),
        compiler_params=pltpu.CompilerParams(
            dimension_semantics=("arbitrary",), vmem_limit_bytes=VMEM_LIMIT_BYTES),
        name="ada_proj",
    )(c_all, w_ada, b_ada)


def _inproj_parts(x_ref, shift_ref, scale_ref, g_ref, wbig_ref, wsh_ref, wsl_ref,
                  zb_ref, zs_ref, tail_ref, *, bt, tt):
    rows = bt * tt
    h = _rms(x_ref[...]) * g_ref[...]
    h = h * (1.0 + _mod_rows(scale_ref, bt, tt)) + _mod_rows(shift_ref, bt, tt)
    hh = h.astype(bf16)
    hl = (h - hh.astype(f32)).astype(bf16)

    def narrow():
        wsh = wsh_ref[...]
        rhs = jnp.concatenate([jnp.concatenate([wsh, wsl_ref[...]], axis=1),
                               jnp.concatenate([wsh, jnp.zeros_like(wsh)], axis=1)], axis=0)
        t = _mm(jnp.concatenate([hh, hl], axis=1), rhs)
        zs_ref[...] = t[:, :N_SMALL] + t[:, N_SMALL:]

    def wide(idx, lo=0, width=PROJ_COLS):
        c = idx * PROJ_COLS + lo
        z = _mm(hh, wbig_ref[:, c:c + width])
        if isinstance(zb_ref, (list, tuple)):
            zb_ref[idx][:, lo:lo + width] = z
        else:
            zb_ref[:, c:c + width] = z.astype(zb_ref.dtype)
        if idx == OFF_QKA // PROJ_COLS:
            tail_ref[0, :, lo:lo + width] = z[rows - SUBLANES:rows, :]

    return narrow, wide


def _inproj_body(*refs, bt, tt):
    narrow, wide = _inproj_parts(*refs, bt=bt, tt=tt)
    narrow()
    for idx in range(N_BIG // PROJ_COLS):
        wide(idx)


def _const_spec(shape):
    nd = len(shape)
    return pl.BlockSpec(shape, lambda i, j: (0,) * nd, pipeline_mode=pl.Buffered(1))


def _mod_spec(bt, k, row0):
    assert row0 % bt == 0
    return pl.BlockSpec((bt, 1, 1, D_MODEL), lambda i, j: (row0 // bt + i, k, 0, 0))


def _inproj_call(x2d, ada4, g_norm, w_big, ws_hi, ws_lo, *, nb, nt, bt, tt, zdtype):
    ada, row0 = ada4
    rows = bt * tt
    total = x2d.shape[0]
    row_spec = lambda n: pl.BlockSpec((rows, n), lambda i, j: (i * nt + j, 0))
    return pl.pallas_call(
        functools.partial(_inproj_body, bt=bt, tt=tt),
        out_shape=(
            jax.ShapeDtypeStruct((total, N_BIG), zdtype),
            jax.ShapeDtypeStruct((total, N_SMALL), f32),
            jax.ShapeDtypeStruct((nb * nt, SUBLANES, PROJ_COLS), f32),
        ),
        grid=(nb, nt),
        in_specs=[
            row_spec(D_MODEL), _mod_spec(bt, 0, row0), _mod_spec(bt, 1, row0),
            _const_spec((1, D_MODEL)), _const_spec((D_MODEL, N_BIG)),
            _const_spec((D_MODEL, N_SMALL)), _const_spec((D_MODEL, N_SMALL)),
        ],
        out_specs=(
            row_spec(N_BIG), row_spec(N_SMALL),
            pl.BlockSpec((1, SUBLANES, PROJ_COLS), lambda i, j: (i * nt + j, 0, 0)),
        ),
        compiler_params=pltpu.CompilerParams(
            dimension_semantics=("arbitrary", "arbitrary"), vmem_limit_bytes=VMEM_LIMIT_BYTES),
        name="in_proj",
    )(x2d, ada, ada, g_norm, w_big, ws_hi, ws_lo)


N_MIXER_CONSTS = 7
N_MIXER_INIT = 5
N_MIXER_OUTS = 5


def _mixer_scratch(rows):
    return [pltpu.VMEM((rows, QK), f32)] * 5 + [
        pltpu.VMEM((rows, N_SMALL), f32), pltpu.VMEM((rows, N_SMALL), f32), pltpu.VMEM((rows, QK), f32),
    ] + [pltpu.VMEM((rows, D_MODEL), f32)] * 4 + [
        pltpu.VMEM((SUBLANES, 2 * QK), f32), pltpu.VMEM((N_HEADS, DK, DVA), f32),
    ]


def _mixer_body(*refs, L, vlo, vhi, sample):
    zb_ref, zs_ref = refs[:2]
    k = 2
    init = None
    if sample:
        init = refs[k:k + N_MIXER_INIT]
        k += N_MIXER_INIT
    consts = refs[k:k + N_MIXER_CONSTS]
    k += N_MIXER_CONSTS
    outs = refs[k:k + N_MIXER_OUTS]
    if not sample:
        _mixer_compute(zb_ref, zs_ref, init, consts, outs, refs[k + N_MIXER_OUTS:], L=L, vlo=vlo, vhi=vhi)
        return
    zx_s, zsx_s, ux_s = refs[-3:]
    n_tok = vhi - vlo
    for c in range(0, N_BIG, PROJ_COLS):
        zx_s[:, c:c + PROJ_COLS] = _spread_tokens(zb_ref[:, c:c + PROJ_COLS], vlo, n_tok)
    zsx_s[...] = _spread_tokens(zs_ref[...], vlo, n_tok)
    _mixer_compute(zx_s, zsx_s, init, consts, (ux_s,) + tuple(outs[1:]), refs[k + N_MIXER_OUTS:-3],
                   L=L, vlo=vlo, vhi=vhi)
    outs[0][...] = _gather_tokens(ux_s[...], vlo, n_tok).astype(outs[0].dtype)


def _fused_body(x_ref, shift_ref, scale_ref, g_ref, wbig_ref, wsh_ref, wsl_ref, *refs, L):
    consts = refs[:N_MIXER_CONSTS]
    tail_ref = refs[N_MIXER_CONSTS]
    outs = refs[N_MIXER_CONSTS + 1:N_MIXER_CONSTS + 1 + N_MIXER_OUTS]
    k = N_MIXER_CONSTS + 1 + N_MIXER_OUTS
    zb_s = list(refs[k:k + N_BIG // PROJ_COLS])
    zs_s = refs[k + N_BIG // PROJ_COLS]
    scratch = refs[k + N_BIG // PROJ_COLS + 1:]
    _zero_carried_state(outs, scratch)
    narrow, wide = _inproj_parts(x_ref, shift_ref, scale_ref, g_ref, wbig_ref, wsh_ref, wsl_ref,
                                 zb_s, zs_s, tail_ref, bt=1, tt=L)

    order = (OFF_QKA, OFF_QB, OFF_GA, OFF_OA, OFF_GB, OFF_RB, OFF_VA, OFF_VB)
    pieces = [(off // PROJ_COLS, lo) for off in order for lo in range(0, PROJ_COLS, PIECE_COLS)]
    emitted = []

    def tick(n=1):
        if not emitted:
            narrow()
        for _ in range(n):
            if len(emitted) < len(pieces):
                idx, lo = pieces[len(emitted)]
                wide(idx, lo, PIECE_COLS)
                emitted.append((idx, lo))

    def need(off, width):
        wanted = [(off // PROJ_COLS, lo) for lo in range(0, PROJ_COLS, PIECE_COLS)
                  if lo < off % PROJ_COLS + width and lo + PIECE_COLS > off % PROJ_COLS]
        while any(w not in emitted for w in wanted):
            tick()

    _mixer_compute(zb_s, zs_s, None, consts, outs, scratch, L=L, vlo=0, vhi=L, zero_init=False,
                   hooks=(tick, need))


def _zero_carried_state(outs, scratch):
    _, _, _, mo_ref, So_ref = outs
    carry_s, caug_s = scratch[-2:]

    @pl.when(pl.program_id(1) == 0)
    def _zero():
        caug_s[...] = jnp.zeros(caug_s.shape, f32)
        mo_ref[...] = jnp.zeros(mo_ref.shape, f32)
        So_ref[...] = jnp.zeros(So_ref.shape, f32)
        carry_s[...] = jnp.zeros(carry_s.shape, f32)


def _mixer_compute(zb_ref, zs_ref, init, consts, outs, scratch, *, L, vlo, vhi, zero_init=True,
                   hooks=(lambda n=1: None, lambda off, width: None)):
    sample = init is not None
    if sample:
        conv0_ref, C0_ref, n0_ref, m0_ref, S0_ref = init
    cw_ref, cb_ref, gb_ref, wal_ref, bal_ref, gha_ref, ghb_ref = consts
    u_ref, Co_ref, no_ref, mo_ref, So_ref = outs
    qa_s, ka_s, qh_s, kh_s, kb_s, x_s, r_s, bg_s, ha_s, hb_s, pa_s, pb_s, carry_s, caug_s = scratch
    rows = zs_ref.shape[0]
    every = slice(None)

    tick, need = hooks

    def zcols(rsel, off, width):
        need(off, width)
        if isinstance(zb_ref, (list, tuple)):
            grp, o = divmod(off, PROJ_COLS)
            assert o + width <= PROJ_COLS
            return zb_ref[grp][rsel, o:o + width]
        return zb_ref[rsel, off:off + width]
    n_groups = rows // L
    log2_l = L.bit_length() - 1
    masked = not (vlo == 0 and vhi == L)
    scale = DK ** -0.5

    if not sample and zero_init:
        _zero_carried_state(outs, scratch)

    row_in = lax.broadcasted_iota(jnp.int32, (rows, 1), 0) & (L - 1)
    valid = (row_in >= vlo) & (row_in < vhi)

    cw = cw_ref[...]
    cb = cb_ref[...]
    for c0 in range(0, 2 * QK, PIECE_COLS):
        cs = slice(c0, c0 + PIECE_COLS)
        zq = zcols(every, OFF_QKA + c0, PIECE_COLS).astype(f32)
        if sample:
            zq = jnp.where(row_in < CONV_W - 1, conv0_ref[:, cs], zq)
            carry = jnp.zeros((SUBLANES, PIECE_COLS), f32)
        else:
            carry = carry_s[:, cs]
            carry_s[:, cs] = zq[rows - SUBLANES:rows, :]
        xe = jnp.concatenate([carry, zq], axis=0)
        conv = cb[:, cs] + zq * cw[CONV_W - 1:CONV_W, cs]
        for k in range(CONV_W - 1):
            o = SUBLANES - (CONV_W - 1) + k
            conv = conv + xe[o:o + rows] * cw[k:k + 1, cs]
        qk = conv * _sigmoid(conv)
        if c0 < QK:
            qa_s[:, cs] = qk * scale
        else:
            ka_s[:, c0 - QK:c0 - QK + PIECE_COLS] = qk
        tick()

    lane = lax.broadcasted_iota(jnp.int32, (1, N_SMALL), 1)
    zs = zs_ref[...]
    pre = zs + gb_ref[...]
    gate = jnp.where(lane < LANE_FG, pre, _log_sigmoid(pre))
    gate_f = jnp.where(valid, gate, 0.0) if masked else gate
    rr = lax.broadcasted_iota(jnp.int32, (rows, rows), 0)
    cc = lax.broadcasted_iota(jnp.int32, (rows, rows), 1)
    tril_mask = ((rr >> log2_l) == (cc >> log2_l)) & (cc <= rr)
    tril = jnp.where(tril_mask, 1.0, 0.0).astype(bf16)

    la_lin = _mm(zs.astype(bf16), wal_ref[...]) + bal_ref[...]
    tick(2)

    la_parts, kb_parts = [], []
    for h in range(N_HEADS):
        ks = slice(h * DK, (h + 1) * DK)
        la_h = _log_sigmoid(la_lin[:, ks]) * (1.0 / GLA_TAU)
        la_parts.append(jnp.where(valid, la_h, 0.0) if masked else la_h)
        tick()

    both = jnp.concatenate([gate_f] + la_parts, axis=1)
    wb = N_SMALL + QK
    t = _mm(tril, jnp.concatenate(_split3(both), axis=1))
    csum = t[:, :wb] + (t[:, wb:2 * wb] + t[:, 2 * wb:])
    tick(2)
    x_all = jnp.where(lane < LANE_FG, gate, csum[:, :N_SMALL])
    x_s[...] = x_all
    b_al = pltpu.roll(x_all, N_SMALL - N_HEADS, axis=1)
    r_all = x_all - b_al
    r_ok = (lane < N_HEADS) & valid if masked else jnp.broadcast_to(lane < N_HEADS, (rows, N_SMALL))
    r_s[...] = jnp.where(r_ok, r_all, NEG)
    range_m = jnp.max(jnp.where(r_ok, r_all, NEG)) - jnp.min(jnp.where(r_ok, r_all, -NEG))
    bg = csum[:, N_SMALL:]
    bg_s[...] = bg
    range_g = jnp.max(-bg)
    tick()
    for h in range(N_HEADS):
        ks = slice(h * DK, (h + 1) * DK)
        kb_h = zcols(every, OFF_KB + h * DK, DK).astype(f32)
        kb_parts.append(jnp.where(valid, kb_h, 0.0) if masked else kb_h)
        qh_s[:, ks] = zcols(every, OFF_QB + h * DK, DK).astype(f32) * scale * jnp.exp(bg[:, ks])
        kh_s[:, ks] = kb_parts[h] * jnp.exp(-bg[:, ks])
        tick()
    kb = jnp.concatenate(kb_parts, axis=1)
    fast_ok = jnp.logical_and(range_m <= SAFE_LOG_RANGE, range_g <= SAFE_LOG_RANGE)

    for c0 in range(0, D_MODEL, PIECE_COLS):
        cs = slice(c0, c0 + PIECE_COLS)
        col = lambda off: zcols(every, off + c0, PIECE_COLS).astype(f32)
        pa_s[:, cs] = _sigmoid(col(OFF_GA)) * _sigmoid(col(OFF_OA)) * gha_ref[:, cs]
        tick(2)
    for c0 in range(0, D_MODEL, PIECE_COLS):
        cs = slice(c0, c0 + PIECE_COLS)
        col = lambda off: zcols(every, off + c0, PIECE_COLS).astype(f32)
        rb = col(OFF_RB)
        pb_s[:, cs] = _sigmoid(col(OFF_GB)) * (rb * _sigmoid(rb)) * ghb_ref[:, cs]
        tick(2)
    need(OFF_VA, D_MODEL)
    need(OFF_VB, D_MODEL)

    tri = lax.broadcasted_iota(jnp.int32, (L, L), 0) >= lax.broadcasted_iota(jnp.int32, (L, L), 1)
    eye = lax.broadcasted_iota(jnp.int32, (DK, DK), 0) == lax.broadcasted_iota(jnp.int32, (DK, DK), 1)
    lane_h = lane
    lrow = lax.broadcasted_iota(jnp.int32, (1, L), 1)
    lcol = lax.broadcasted_iota(jnp.int32, (L, 1), 0)
    vrow = (lrow >= vlo) & (lrow < vhi)
    ones16 = jnp.ones((L, DK), bf16)

    def load_state(g, h):
        if sample:
            n_mat = jnp.broadcast_to(_row_to_col(n0_ref[g, pl.ds(h, 1), :], eye), (DK, DK))
            return jnp.concatenate([C0_ref[g, h], n_mat], axis=1), S0_ref[g, h]
        return caug_s[h], So_ref[0, h]

    def store_state(g, h, caug, s_new):
        sidx = g if sample else 0
        if not sample:
            caug_s[h] = caug
        Co_ref[sidx, h] = caug[:, :DV]
        no_ref[sidx, pl.ds(h, 1), :] = _diag_row(caug[:, DV:], eye)
        So_ref[sidx, h] = s_new

    def group_fast(g):
        r0 = g * L
        rs = pl.ds(r0, L)
        xc = x_s[rs, :]
        rc = r_s[rs, :]
        m_prev = m0_ref[g] if sample else mo_ref[0]
        bg_tail = bg_s[pl.ds(r0 + L - SUBLANES, SUBLANES), :]
        heads = []
        for h in range(N_HEADS):
            ks = slice(h * DK, (h + 1) * DK)
            caug, s_st = load_state(g, h)
            c0 = jnp.max(rc[:, h:h + 1], axis=0, keepdims=True)
            q16 = qa_s[rs, ks].astype(bf16)
            kt16 = (ka_s[rs, ks] * jnp.exp(rc[:, h:h + 1] - c0)).astype(bf16)
            qh16 = qh_s[rs, ks].astype(bf16)
            kh = kh_s[rs, ks]
            heads.append(dict(
                caug=caug, s_st=s_st, c0=c0, kt16=kt16, kh=kh,
                va=jnp.concatenate([zcols(rs, OFF_VA + h * DV, DV).astype(bf16), ones16], axis=1),
                vb16=zcols(rs, OFF_VB + h * DV, DV).astype(bf16),
                s_raw=_mm_nt(q16, kt16), a_raw=_mm_nt(qh16, kh.astype(bf16)),
                ci=_mm(q16, caug.astype(bf16)), oi=_mm(qh16, s_st.astype(bf16))))
        for p in heads:
            p["ni"] = _mm(jnp.where(tri, p["s_raw"], 0.0).astype(bf16), p["va"])
            p["o"] = _mm(jnp.where(tri, p["a_raw"], 0.0).astype(bf16), p["vb16"]) + p["oi"]
        for h, p in enumerate(heads):
            p["eb_last"] = jnp.exp(bg_tail[SUBLANES - 1:SUBLANES, h * DK:(h + 1) * DK])
            p["kv"] = _mm_tn(p["kt16"], p["va"])
            p["skv"] = _mm_tn((p["kh"] * p["eb_last"]).astype(bf16), p["vb16"])
        m_vec = m_prev
        for h, p in enumerate(heads):
            m_st = m_prev[:, h:h + 1]
            b_col = xc[:, LANE_FG + h:LANE_FG + h + 1]
            b_last = b_col[L - 1:L, :]
            c0 = p["c0"]
            mm = jnp.maximum(c0, m_st)
            comb = jnp.exp(c0 - mm) * p["ni"] + jnp.exp(m_st - mm) * p["ci"]
            inv = 1.0 / jnp.maximum(jnp.abs(comb[:, DV:]), jnp.exp(-(b_col + mm)))
            ha_s[rs, h * DV:h * DV + DK] = comb[:, :DK] * inv
            ha_s[rs, h * DV + DK:(h + 1) * DV] = comb[:, DK:DV] * inv
            hb_s[rs, h * DV:(h + 1) * DV] = p["o"]
            m_new = jnp.maximum(b_last + m_st, b_last + c0)
            caug_new = jnp.exp(b_last + m_st - m_new) * p["caug"] + jnp.exp(b_last + c0 - m_new) * p["kv"]
            s_new = _row_to_col(p["eb_last"], eye) * p["s_st"] + p["skv"]
            store_state(g, h, caug_new, s_new)
            m_vec = jnp.where(lane_h == h, m_new, m_vec)
        mo_ref[g if sample else 0] = m_vec

    def group_exact(g):
        static = isinstance(g, int)
        r0 = g * L if static else pl.multiple_of(g * L, L)
        tail0 = r0 + L - SUBLANES if static else pl.multiple_of(r0 + L - SUBLANES, SUBLANES)
        rs = pl.ds(r0, L)
        xc = x_s[rs, :]
        rc = r_s[rs, :]
        sr = lax.broadcasted_iota(jnp.int32, (2 * SUBLANES, N_SMALL), 0)
        sc = lax.broadcasted_iota(jnp.int32, (2 * SUBLANES, N_SMALL), 1)
        sel = jnp.where((sr < N_HEADS) & (sc == sr + LANE_IG), 1.0,
                        jnp.where((sr < N_HEADS) & (sc == sr + LANE_FG), -1.0, 0.0)).astype(bf16)
        xh, xm, xl = _split3(xc)
        r_rows = _mm_nt(sel, xh) + (_mm_nt(sel, xm) + _mm_nt(sel, xl))
        m_prev = m0_ref[g] if sample else mo_ref[0]
        m_vec = m_prev
        bg_tail = bg_s[pl.ds(tail0, SUBLANES), :]
        for h in range(N_HEADS):
            ks = slice(h * DK, (h + 1) * DK)
            vs = slice(h * DV, (h + 1) * DV)
            caug, s_st = load_state(g, h)
            m_st = m_prev[:, h:h + 1]
            b_col = xc[:, LANE_FG + h:LANE_FG + h + 1]
            r_col = rc[:, h:h + 1]
            b_last = b_col[L - 1:L, :]
            v16 = zcols(rs, OFF_VA + h * DV, DV).astype(bf16)
            vb16 = zcols(rs, OFF_VB + h * DV, DV).astype(bf16)
            eb_last = jnp.exp(bg_tail[SUBLANES - 1:SUBLANES, ks])
            eb_col = _row_to_col(eb_last, eye)
            c_st = caug[:, :DV]
            n_row = _diag_row(caug[:, DV:], eye)
            r_row = r_rows[h:h + 1, :]
            if masked:
                r_row = jnp.where(vrow, r_row, NEG)
            dm = jnp.where(tri, b_col + r_row, NEG)
            m_t = jnp.maximum(jnp.max(dm, axis=1, keepdims=True), b_col + m_st)
            w = jnp.exp(dm - m_t)
            wi = jnp.exp(b_col + m_st - m_t)
            q = qa_s[rs, ks]
            k = ka_s[rs, ks]
            q16 = q.astype(bf16)
            s = _mm_nt(q16, k.astype(bf16)) * w
            num = _mm(s.astype(bf16), v16) + wi * _mm(q16, c_st.astype(bf16))
            den = jnp.sum(s, axis=1, keepdims=True) + wi * jnp.sum(q * n_row, axis=1, keepdims=True)
            ha_s[rs, vs] = num / jnp.maximum(jnp.abs(den), jnp.exp(-m_t))
            m_new = jnp.maximum(b_last + m_st, b_last + jnp.max(r_row, axis=1, keepdims=True))
            wc = jnp.exp(b_last + m_st - m_new)
            kt = k * jnp.exp(b_last + r_col - m_new)
            n_new = wc * n_row + jnp.sum(kt, axis=0, keepdims=True)
            caug_new = jnp.concatenate(
                [wc * c_st + _mm_tn(kt.astype(bf16), v16),
                 jnp.broadcast_to(_row_to_col(n_new, eye), (DK, DK))], axis=1)
            qb = zcols(rs, OFF_QB + h * DK, DK).astype(f32) * scale
            bgc = bg_s[rs, ks]
            sub = lax.broadcasted_iota(jnp.int32, (SUBLANES, 1), 0)

            def score_col(s_i, a, ks=ks, qb=qb, bgc=bgc, sub=sub):
                s8 = lax.shift_left(lax.shift_right_logical(s_i, 3), 3)
                base = pl.multiple_of(r0 + s8, SUBLANES)
                pick = sub == (s_i - s8)
                b_s = jnp.sum(jnp.where(pick, bg_s[pl.ds(base, SUBLANES), ks], 0.0), axis=0, keepdims=True)
                k_s = jnp.sum(jnp.where(pick, kb_s[pl.ds(base, SUBLANES), ks], 0.0), axis=0, keepdims=True)
                e = jnp.exp(jnp.where(lcol >= s_i, bgc - b_s, NEG))
                col = jnp.sum(qb * e * k_s, axis=1, keepdims=True)
                return jnp.where(lrow == s_i, col, a)

            a = lax.fori_loop(0, L, score_col, jnp.zeros((L, L), f32))
            hb_s[rs, vs] = (_mm(a.astype(bf16), vb16)
                            + _mm((qb * jnp.exp(bgc)).astype(bf16), s_st.astype(bf16)))
            kt_b = kb_s[rs, ks] * jnp.exp(bg_tail[SUBLANES - 1:SUBLANES, ks] - bgc)
            s_new = eb_col * s_st + _mm_tn(kt_b.astype(bf16), vb16)
            store_state(g, h, caug_new, s_new)
            m_vec = jnp.where(lane_h == h, m_new, m_vec)
        mo_ref[g if sample else 0] = m_vec

    def groups_fast_batched():
        r_all_m = r_s[...]
        c0_l, m_l, mm_l, fac, wc = [], [], [], [], []
        for g in range(n_groups):
            gs = slice(g * L, (g + 1) * L)
            c0_g = jnp.max(r_all_m[gs], axis=0, keepdims=True)
            m_g = m0_ref[g]
            b_last = b_al[gs][L - 1:L, :]
            m_new = jnp.maximum(b_last + m_g, b_last + c0_g)
            mo_ref[g] = m_new
            c0_l.append(c0_g)
            m_l.append(m_g)
            mm_l.append(jnp.maximum(c0_g, m_g))
            fac.append(jnp.exp(b_last + c0_g - m_new))
            wc.append(jnp.exp(b_last + m_g - m_new))
        rows_of = lambda lst: jnp.concatenate([jnp.broadcast_to(v, (L, N_SMALL)) for v in lst], axis=0)
        c0_r, m_r, mm_r = rows_of(c0_l), rows_of(m_l), rows_of(mm_l)
        beta = jnp.exp(r_all_m - c0_r)
        gi_r = jnp.exp(c0_r - mm_r)
        wi_r = jnp.exp(m_r - mm_r)
        floor = jnp.exp(-(b_al + mm_r))
        ones_g = jnp.ones((L, DK), bf16)
        for h in range(N_HEADS):
            ks = slice(h * DK, (h + 1) * DK)
            vs = slice(h * DV, (h + 1) * DV)
            kt = ka_s[:, ks] * beta[:, h:h + 1]
            v = zcols(every, OFF_VA + h * DV, DV)
            vb = zcols(every, OFF_VB + h * DV, DV)
            q16 = qa_s[:, ks].astype(bf16)
            va = jnp.concatenate([v.astype(bf16), jnp.ones((rows, DK), bf16)], axis=1)
            s16 = jnp.where(tril_mask, _mm_nt(q16, kt.astype(bf16)), 0.0).astype(bf16)
            ni = _mm(s16, va)
            qh16 = qh_s[:, ks].astype(bf16)
            kh = kh_s[:, ks]
            a16 = jnp.where(tril_mask, _mm_nt(qh16, kh.astype(bf16)), 0.0).astype(bf16)
            o_intra = _mm(a16, vb.astype(bf16))
            ci_parts, oi_parts = [], []
            for g in range(n_groups):
                gs = slice(g * L, (g + 1) * L)
                caug, s_st = load_state(g, h)
                va_g = jnp.concatenate([v[gs].astype(bf16), ones_g], axis=1)
                ci_parts.append(_mm(qa_s[gs, ks].astype(bf16), caug.astype(bf16)))
                caug_new = wc[g][:, h:h + 1] * caug + fac[g][:, h:h + 1] * _mm_tn(kt[gs].astype(bf16), va_g)
                oi_parts.append(_mm(qh_s[gs, ks].astype(bf16), s_st.astype(bf16)))
                eb_last = jnp.exp(bg_s[gs, ks][L - 1:L, :])
                s_new = (_row_to_col(eb_last, eye) * s_st
                         + _mm_tn((kh[gs] * eb_last).astype(bf16), vb[gs].astype(bf16)))
                store_state(g, h, caug_new, s_new)
            comb = gi_r[:, h:h + 1] * ni + wi_r[:, h:h + 1] * jnp.concatenate(ci_parts, axis=0)
            inv = 1.0 / jnp.maximum(jnp.abs(comb[:, DV:]), floor[:, h:h + 1])
            ha_s[:, h * DV:h * DV + DK] = comb[:, :DK] * inv
            ha_s[:, h * DV + DK:(h + 1) * DV] = comb[:, DK:DV] * inv
            hb_s[:, vs] = o_intra + jnp.concatenate(oi_parts, axis=0)

    def run_groups(exact):
        if not exact:
            if sample and L == SUBLANES:
                groups_fast_batched()
            else:
                for g in range(n_groups):
                    group_fast(g)
        elif n_groups == 1:
            group_exact(0)
        else:
            def body(g, carry_):
                group_exact(g)
                return carry_
            lax.fori_loop(0, n_groups, body, 0)

    @pl.when(fast_ok)
    def _fast():
        run_groups(False)

    @pl.when(jnp.logical_not(fast_ok))
    def _exact():
        kb_s[...] = kb
        run_groups(True)

    for h in range(N_HEADS):
        vs = slice(h * DV, (h + 1) * DV)
        u = pa_s[:, vs] * _rms(ha_s[:, vs]) + pb_s[:, vs] * _rms(hb_s[:, vs])
        u_ref[:, vs] = u.astype(u_ref.dtype)


def _mixer_call(zb, zs, init, consts, *, nb, nt, bt, L, vlo, vhi):
    sample = init is not None
    rows = bt * L
    tok_rows = bt * (vhi - vlo) if sample else rows
    n_state = nb * bt
    row_spec = lambda n, r=rows: pl.BlockSpec((r, n), lambda i, j: (i * nt + j, 0))
    st4 = pl.BlockSpec((bt, N_HEADS, DK, DV), lambda i, j: (i, 0, 0, 0))
    st_n = pl.BlockSpec((bt, N_HEADS, DK), lambda i, j: (i, 0, 0))
    st_m = pl.BlockSpec((bt, 1, N_SMALL), lambda i, j: (i, 0, 0))
    in_specs = [row_spec(N_BIG, tok_rows), row_spec(N_SMALL, tok_rows)]
    args = [zb, zs]
    scratch = _mixer_scratch(rows)
    if sample:
        in_specs += [row_spec(2 * QK), st4, st_n, st_m, st4]
        args += list(init)
        scratch += [pltpu.VMEM((rows, N_BIG), f32), pltpu.VMEM((rows, N_SMALL), f32),
                    pltpu.VMEM((rows, D_MODEL), f32)]
    in_specs += [_const_spec(c.shape) for c in consts]
    args += list(consts)
    return pl.pallas_call(
        functools.partial(_mixer_body, L=L, vlo=vlo, vhi=vhi, sample=sample),
        out_shape=(
            jax.ShapeDtypeStruct((zb.shape[0], D_MODEL), bf16),
            jax.ShapeDtypeStruct((n_state, N_HEADS, DK, DV), f32),
            jax.ShapeDtypeStruct((n_state, N_HEADS, DK), f32),
            jax.ShapeDtypeStruct((n_state, 1, N_SMALL), f32),
            jax.ShapeDtypeStruct((n_state, N_HEADS, DK, DV), f32),
        ),
        grid=(nb, nt),
        in_specs=in_specs,
        out_specs=(row_spec(D_MODEL, tok_rows), st4, st_n, st_m, st4),
        scratch_shapes=scratch,
        compiler_params=pltpu.CompilerParams(
            dimension_semantics=("arbitrary", "arbitrary"), vmem_limit_bytes=VMEM_LIMIT_BYTES),
        name="mixer_sample" if sample else "mixer_prompt",
    )(*args)


def _fused_call(x2d, ada4, g_norm, w_big, ws_hi, ws_lo, consts, *, nb, nt, L):
    ada, row0 = ada4
    row_spec = lambda n: pl.BlockSpec((L, n), lambda i, j: (i * nt + j, 0))
    st4 = pl.BlockSpec((1, N_HEADS, DK, DV), lambda i, j: (i, 0, 0, 0))
    st_n = pl.BlockSpec((1, N_HEADS, DK), lambda i, j: (i, 0, 0))
    st_m = pl.BlockSpec((1, 1, N_SMALL), lambda i, j: (i, 0, 0))
    return pl.pallas_call(
        functools.partial(_fused_body, L=L),
        out_shape=(
            jax.ShapeDtypeStruct((nb * nt, SUBLANES, PROJ_COLS), f32),
            jax.ShapeDtypeStruct((x2d.shape[0], D_MODEL), bf16),
            jax.ShapeDtypeStruct((nb, N_HEADS, DK, DV), f32),
            jax.ShapeDtypeStruct((nb, N_HEADS, DK), f32),
            jax.ShapeDtypeStruct((nb, 1, N_SMALL), f32),
            jax.ShapeDtypeStruct((nb, N_HEADS, DK, DV), f32),
        ),
        grid=(nb, nt),
        in_specs=[
            row_spec(D_MODEL), _mod_spec(1, 0, row0), _mod_spec(1, 1, row0),
            _const_spec((1, D_MODEL)), _const_spec((D_MODEL, N_BIG)),
            _const_spec((D_MODEL, N_SMALL)), _const_spec((D_MODEL, N_SMALL)),
        ] + [_const_spec(c.shape) for c in consts],
        out_specs=(
            pl.BlockSpec((1, SUBLANES, PROJ_COLS), lambda i, j: (i * nt + j, 0, 0)),
            row_spec(D_MODEL), st4, st_n, st_m, st4,
        ),
        scratch_shapes=([pltpu.VMEM((L, PROJ_COLS), f32)] * (N_BIG // PROJ_COLS)
                        + [pltpu.VMEM((L, N_SMALL), f32)] + _mixer_scratch(L)),
        compiler_params=pltpu.CompilerParams(
            dimension_semantics=("arbitrary", "arbitrary"), vmem_limit_bytes=VMEM_LIMIT_BYTES),
        name="proj_mixer_prompt",
    )(x2d, ada, ada, g_norm, w_big, ws_hi, ws_lo, *consts)


def _out_body(u_ref, x_ref, gate1_ref, shift2_ref, scale2_ref, gate2_ref, gn2_ref, gfin_ref,
              wout_ref, wg_ref, wu_ref, wd_ref, y_ref, *, bt, tt):
    x1 = x_ref[...] + _mod_rows(gate1_ref, bt, tt) * _mm(u_ref[...], wout_ref[...])
    h2 = _rms(x1) * gn2_ref[...]
    h2 = (h2 * (1.0 + _mod_rows(scale2_ref, bt, tt)) + _mod_rows(shift2_ref, bt, tt)).astype(bf16)
    acc = jnp.zeros(x1.shape, f32)
    for c in range(0, D_FF, FFN_COLS):
        g = _mm(h2, wg_ref[:, c:c + FFN_COLS])
        up = _mm(h2, wu_ref[:, c:c + FFN_COLS])
        acc = acc + _mm((g * _sigmoid(g) * up).astype(bf16), wd_ref[c:c + FFN_COLS, :])
    x2 = x1 + _mod_rows(gate2_ref, bt, tt) * acc
    y_ref[...] = _rms(x2) * gfin_ref[...]


N_OUT_MODS = 4


def _out_both_body(*refs, n_prompt, tt_p, bt_s, tt_s):
    up_ref, xp_ref = refs[:2]
    mods_p = refs[2:2 + N_OUT_MODS]
    us_ref, xs_ref = refs[2 + N_OUT_MODS:4 + N_OUT_MODS]
    mods_s = refs[4 + N_OUT_MODS:4 + 2 * N_OUT_MODS]
    shared = refs[4 + 2 * N_OUT_MODS:10 + 2 * N_OUT_MODS]
    yp_ref, ys_ref = refs[10 + 2 * N_OUT_MODS:]
    step = pl.program_id(0)

    @pl.when(step < n_prompt)
    def _prompt():
        _out_body(up_ref, xp_ref, *mods_p, *shared, yp_ref, bt=1, tt=tt_p)

    @pl.when(step == n_prompt)
    def _sample():
        _out_body(us_ref, xs_ref, *mods_s, *shared, ys_ref, bt=bt_s, tt=tt_s)


def _out_call(u_p, x_p, u_s, x_s, ada, g_norm2, g_final, w_out, w_gate, w_up, w_down, *,
              row0_p, tiles_per_seq, tt_p, bt_s, tt_s):
    n_prompt = x_p.shape[0] // tt_p
    rows_s = bt_s * tt_s
    assert x_s.shape[0] == rows_s
    tile_p = lambda s: jnp.minimum(s, n_prompt - 1)
    prompt_rows = pl.BlockSpec((tt_p, D_MODEL), lambda s: (tile_p(s), 0))
    sample_rows = lambda: pl.BlockSpec((rows_s, D_MODEL), lambda s: (0, 0), pipeline_mode=pl.Buffered(1))
    mod_p = lambda k: pl.BlockSpec((1, 1, 1, D_MODEL), lambda s: (row0_p + tile_p(s) // tiles_per_seq, k, 0, 0))
    mod_s = lambda k: pl.BlockSpec((bt_s, 1, 1, D_MODEL), lambda s: (0, k, 0, 0), pipeline_mode=pl.Buffered(1))
    const = lambda shape: pl.BlockSpec(shape, lambda s: (0, 0), pipeline_mode=pl.Buffered(1))
    ks = range(N_ADA - N_OUT_MODS, N_ADA)
    return pl.pallas_call(
        functools.partial(_out_both_body, n_prompt=n_prompt, tt_p=tt_p, bt_s=bt_s, tt_s=tt_s),
        out_shape=(jax.ShapeDtypeStruct(x_p.shape, f32), jax.ShapeDtypeStruct(x_s.shape, f32)),
        grid=(n_prompt + 1,),
        in_specs=(
            [prompt_rows, prompt_rows] + [mod_p(k) for k in ks]
            + [sample_rows(), sample_rows()] + [mod_s(k) for k in ks]
            + [const((1, D_MODEL)), const((1, D_MODEL)), const((D_MODEL, D_MODEL)),
               const((D_MODEL, D_FF)), const((D_MODEL, D_FF)), const((D_FF, D_MODEL))]),
        out_specs=(prompt_rows, pl.BlockSpec((rows_s, D_MODEL), lambda s: (0, 0))),
        compiler_params=pltpu.CompilerParams(
            dimension_semantics=("arbitrary",), vmem_limit_bytes=VMEM_LIMIT_BYTES),
        name="out_ffn",
    )(u_p, x_p, *([ada] * N_OUT_MODS), u_s, x_s, *([ada] * N_OUT_MODS),
      g_norm2, g_final, w_out, w_gate, w_up, w_down)


PROMPT_CHUNK = 256
PROMPT_FFN_ROWS = 512
SAMPLE_L = SUBLANES
SAMPLE_MIX_GROUPS = 8
SAMPLE_DENSE_GROUPS = 64


W_IN_SEG_A = (0, 3072)
W_IN_GATES = 3072
W_IN_SEG_B = (3080, 6152)
W_IN_LR = 6152
W_IN_SEG_C = (6168, 8216)
LANES = 128


PREP_COLS = 512
SHIFT_B = W_IN_SEG_B[0] - OFF_QB
SHIFT_C = W_IN_SEG_C[0] - OFF_GA
PREP_HALO = 32


def _prep_body(a_ref, b_ref, gt_ref, lt_ref, wb_ref, wh_ref, wl_ref):
    j = pl.program_id(0)
    both = jnp.concatenate([a_ref[...], b_ref[...]], axis=0)

    def emit(shift):
        wb_ref[...] = both[shift:shift + PREP_COLS, :].T.astype(bf16)

    pl.when(j < OFF_QB // PREP_COLS)(lambda: emit(0))
    pl.when((j >= OFF_QB // PREP_COLS) & (j < OFF_GA // PREP_COLS))(lambda: emit(SHIFT_B))
    pl.when(j >= OFF_GA // PREP_COLS)(lambda: emit(SHIFT_C))

    @pl.when(j == 0)
    def _narrow():
        row = lax.broadcasted_iota(jnp.int32, (LANES, 1), 0)
        small = jnp.where(row < LANE_LR, gt_ref[...], jnp.where(row < LANE_LR + GLA_RANK, lt_ref[...], 0.0)).T
        hi = small.astype(bf16)
        wh_ref[...] = hi
        wl_ref[...] = (small - hi.astype(f32)).astype(bf16)


def _prep_call(w_in):
    assert W_IN_GATES % LANES == 0 and (W_IN_LR - LANE_LR) % LANES == 0
    assert max(SHIFT_B, SHIFT_C) <= PREP_HALO and SHIFT_B % SUBLANES == 0 and SHIFT_C % SUBLANES == 0
    wt = jnp.transpose(w_in)
    halo_per_block = PREP_COLS // PREP_HALO
    last = (wt.shape[0] - 1) // PREP_HALO
    const = lambda shape: pl.BlockSpec(shape, lambda j: (0, 0))
    return pl.pallas_call(
        _prep_body,
        out_shape=(
            jax.ShapeDtypeStruct((D_MODEL, N_BIG), bf16),
            jax.ShapeDtypeStruct((D_MODEL, N_SMALL), bf16),
            jax.ShapeDtypeStruct((D_MODEL, N_SMALL), bf16),
        ),
        grid=(N_BIG // PREP_COLS,),
        in_specs=[
            pl.BlockSpec((PREP_COLS, D_MODEL), lambda j: (j, 0)),
            pl.BlockSpec((PREP_HALO, D_MODEL), lambda j: (jnp.minimum((j + 1) * halo_per_block, last), 0)),
            pl.BlockSpec((LANES, D_MODEL), lambda j: (W_IN_GATES // LANES, 0)),
            pl.BlockSpec((LANES, D_MODEL), lambda j: ((W_IN_LR - LANE_LR) // LANES, 0)),
        ],
        out_specs=(
            pl.BlockSpec((D_MODEL, PREP_COLS), lambda j: (0, j)),
            const((D_MODEL, N_SMALL)), const((D_MODEL, N_SMALL)),
        ),
        compiler_params=pltpu.CompilerParams(
            dimension_semantics=("arbitrary",), vmem_limit_bytes=VMEM_LIMIT_BYTES),
        name="w_in_prep",
    )(wt, wt, wt, wt)


def _prep_layer_weights(w_in, w_alpha, b_igate, b_fgate):
    w_big, ws_hi, ws_lo = _prep_call(w_in)
    w_al = jnp.zeros((N_SMALL, QK), f32).at[LANE_LR:LANE_LR + GLA_RANK].set(w_alpha).astype(bf16)
    g_bias = jnp.concatenate([b_igate, b_fgate, jnp.zeros((N_SMALL - 2 * N_HEADS,), f32)])[None, :]
    return w_big, ws_hi, ws_lo, w_al, g_bias


def _layer(xp2d, xs2d, st_s, ada_p, ada_s, p, *, bp, tp, bs):
    (g_norm1, g_norm2, w_in, conv_w, conv_b, b_igate, b_fgate, g_head_a,
     w_alpha, b_alpha, g_head_b, w_out, w_ffn_gate, w_ffn_up, w_ffn_down, g_final_or_ones) = p
    w_big, ws_hi, ws_lo, w_al, g_bias = _prep_layer_weights(w_in, w_alpha, b_igate, b_fgate)
    consts = (conv_w, conv_b[None, :], g_bias, w_al, b_alpha[None, :],
              g_head_a.reshape(1, D_MODEL), g_head_b.reshape(1, D_MODEL))
    w_out16, w_g16, w_u16, w_d16 = (w.astype(bf16) for w in (w_out, w_ffn_gate, w_ffn_up, w_ffn_down))
    gn1, gn2, gfin = g_norm1[None, :], g_norm2[None, :], g_final_or_ones[None, :]

    ntp = tp // PROMPT_CHUNK
    tail, u, c_p, n_p, m_p, s_p = _fused_call(xp2d, ada_p, gn1, w_big, ws_hi, ws_lo, consts,
                                              nb=bp, nt=ntp, L=PROMPT_CHUNK)
    conv_p = tail.reshape(bp, ntp, SUBLANES, 2 * QK)[:, ntp - 1, SUBLANES - (CONV_W - 1):, :]

    conv0, c0, n0, m0, s0 = st_s
    lo = CONV_W - 1
    n_tok = SAMPLE_L - lo - 1
    nbd = bs // SAMPLE_DENSE_GROUPS
    zb_s, zs_s, _ = _inproj_call(xs2d, ada_s, gn1, w_big, ws_hi, ws_lo,
                                 nb=nbd, nt=1, bt=SAMPLE_DENSE_GROUPS, tt=n_tok, zdtype=f32)
    conv_s = zb_s[:, OFF_QKA:OFF_QKA + 2 * QK].reshape(bs, n_tok, 2 * QK)[:, n_tok - lo:, :]
    conv0_rows = jnp.pad(conv0, ((0, 0), (0, SAMPLE_L - lo), (0, 0))).reshape(bs * SAMPLE_L, 2 * QK)
    m0_lanes = jnp.pad(m0[:, None, :], ((0, 0), (0, 0), (0, N_SMALL - N_HEADS)))
    u_s, c_s, n_s, m_s, s_s = _mixer_call(
        zb_s, zs_s, (conv0_rows, c0, n0, m0_lanes, s0), consts,
        nb=bs // SAMPLE_MIX_GROUPS, nt=1, bt=SAMPLE_MIX_GROUPS, L=SAMPLE_L, vlo=lo, vhi=lo + n_tok)
    assert ada_s[1] == 0 and ada_p[0] is ada_s[0]
    yp, ys = _out_call(u, xp2d, u_s, xs2d, ada_p[0], gn2, gfin, w_out16, w_g16, w_u16, w_d16,
                       row0_p=ada_p[1], tiles_per_seq=tp // PROMPT_FFN_ROWS, tt_p=PROMPT_FFN_ROWS,
                       bt_s=bs, tt_s=n_tok)
    st_p = (conv_p, c_p, n_p, m_p[:, 0, :N_HEADS], s_p)
    st_s_new = (conv_s, c_s, n_s, m_s[:, 0, :N_HEADS], s_s)
    return yp, ys, st_p, st_s_new


def kernel(x_prompt, x_sample, state_conv, state_C, state_n, state_m, state_S, c_prompt, c_sample,
           g_norm1, g_norm2, w_ada, b_ada, w_in, conv_w, conv_b, b_igate, b_fgate, g_head_a,
           w_alpha, b_alpha, g_head_b, w_out, w_ffn_gate, w_ffn_up, w_ffn_down, g_final):
    bp, tp, _ = x_prompt.shape
    bs, ts, _ = x_sample.shape
    depth = w_in.shape[0]
    assert depth == 1, "the final norm is fused into the (single) layer's FFN kernel"
    lo = CONV_W - 1
    assert ts == SAMPLE_L - lo - 1
    xp2d = x_prompt.reshape(bp * tp, D_MODEL)
    xs2d = x_sample.reshape(bs * ts, D_MODEL)
    c_all = jnp.concatenate([c_sample, c_prompt], axis=0)
    new_p = [[] for _ in range(5)]
    new_s = [[] for _ in range(5)]
    for l in range(depth):
        ada = _ada_call(c_all, w_ada[l], b_ada[l][None, :])
        p = (g_norm1[l], g_norm2[l], w_in[l], conv_w[l], conv_b[l], b_igate[l], b_fgate[l], g_head_a[l],
             w_alpha[l], b_alpha[l], g_head_b[l], w_out[l], w_ffn_gate[l], w_ffn_up[l], w_ffn_down[l], g_final)
        st_s = (state_conv[l], state_C[l], state_n[l], state_m[l], state_S[l])
        xp2d, xs2d, st_p, st_s_new = _layer(xp2d, xs2d, st_s, (ada, bs), (ada, 0), p, bp=bp, tp=tp, bs=bs)
        for j in range(5):
            new_p[j].append(st_p[j])
            new_s[j].append(st_s_new[j])
    y_prompt = xp2d.reshape(bp, tp, D_MODEL)
    y_sample = xs2d.reshape(bs, ts, D_MODEL)
    outs_p = [jnp.stack(a, axis=0) for a in new_p]
    outs_s = [jnp.stack(a, axis=0) for a in new_s]
    return (y_prompt, y_sample, *outs_p, *outs_s)
```

```python
import functools

import jax
import jax.numpy as jnp
from jax import lax
from jax.experimental import pallas as pl
from jax.experimental.pallas import tpu as pltpu

f32 = jnp.float32
bf16 = jnp.bfloat16

D_MODEL = 1024
N_HEADS = 4
DK = 128
DV = 256
QK = N_HEADS * DK
CONV_W = 4
GLA_RANK = 16
GLA_TAU = 16.0
D_FF = 2816
EPS = 1e-6
N_ADA = 6

N_BIG = 8192
N_SMALL = 128
OFF_QKA, OFF_VA, OFF_OA, OFF_QB, OFF_KB, OFF_VB, OFF_RB, OFF_GA, OFF_GB = (
    0, 1024, 2048, 3072, 3584, 4096, 5120, 6144, 7168)
LANE_IG, LANE_FG, LANE_LR = 0, N_HEADS, 2 * N_HEADS

SUBLANES = 8
NEG = -1e30
SAFE_LOG_RANGE = 60.0
V7X_VMEM_BYTES = 64 * 1024 * 1024
VMEM_LIMIT_BYTES = V7X_VMEM_BYTES - 8 * 1024 * 1024
FFN_COLS = 256
PROJ_COLS = 1024
PIECE_COLS = 256
DVA = DV + DK


def _mm(a, b):
    return jnp.dot(a, b, preferred_element_type=f32)


def _mm_nt(a, b):
    return lax.dot_general(a, b, (((1,), (1,)), ((), ())), preferred_element_type=f32)


def _mm_tn(a, b):
    return lax.dot_general(a, b, (((0,), (0,)), ((), ())), preferred_element_type=f32)


def _split3(x):
    hi = x.astype(bf16)
    r = x - hi.astype(f32)
    mid = r.astype(bf16)
    lo = (r - mid.astype(f32)).astype(bf16)
    return hi, mid, lo


def _sigmoid(x):
    return 0.5 * jnp.tanh(0.5 * x) + 0.5


def _log_sigmoid(x):
    return jnp.minimum(x, 0.0) - jnp.log(1.0 + jnp.exp(-jnp.abs(x)))


def _rms(x):
    return x * lax.rsqrt(jnp.mean(x * x, axis=-1, keepdims=True) + EPS)


def _row_to_col(row, eye):
    return jnp.sum(jnp.where(eye, row, 0.0), axis=1, keepdims=True)


def _diag_row(mat, eye):
    return jnp.sum(jnp.where(eye, mat, 0.0), axis=0, keepdims=True)


def _mod_rows(mod_ref, bt, tt):
    if bt == 1:
        return mod_ref[0, 0]
    if tt % SUBLANES == 0:
        v = mod_ref[...].reshape(bt, 1, D_MODEL)
        return jnp.broadcast_to(v, (bt, tt, D_MODEL)).reshape(bt * tt, D_MODEL)
    assert 2 * tt == SUBLANES and bt % 2 == 0
    first = lax.broadcasted_iota(jnp.int32, (SUBLANES, 1), 0) < tt
    return jnp.concatenate([jnp.where(first, mod_ref[2 * p, 0], mod_ref[2 * p + 1, 0]) for p in range(bt // 2)],
                           axis=0)


def _spread_tokens(c, lo, n_tok):
    assert 2 * n_tok == SUBLANES
    row = lax.broadcasted_iota(jnp.int32, (SUBLANES, 1), 0)
    mid = (row >= lo) & (row < lo + n_tok)
    tiles = []
    for p in range(c.shape[0] // SUBLANES):
        t = c[p * SUBLANES:(p + 1) * SUBLANES]
        tiles.append(jnp.where(mid, pltpu.roll(t, lo, axis=0), 0.0))
        tiles.append(jnp.where(mid, pltpu.roll(t, (lo + n_tok) % SUBLANES, axis=0), 0.0))
    return jnp.concatenate(tiles, axis=0)


def _gather_tokens(x, lo, n_tok):
    assert 2 * n_tok == SUBLANES
    first = lax.broadcasted_iota(jnp.int32, (SUBLANES, 1), 0) < n_tok
    tiles = []
    for p in range(x.shape[0] // (2 * SUBLANES)):
        even = x[2 * p * SUBLANES:(2 * p + 1) * SUBLANES]
        odd = x[(2 * p + 1) * SUBLANES:(2 * p + 2) * SUBLANES]
        tiles.append(jnp.where(first, pltpu.roll(even, SUBLANES - lo, axis=0),
                               pltpu.roll(odd, (SUBLANES - lo + n_tok) % SUBLANES, axis=0)))
    return jnp.concatenate(tiles, axis=0)


def _ada_body(c_ref, w_ref, b_ref, o_ref):
    o_ref[:, 0, 0, :] = _mm(c_ref[...].astype(bf16), w_ref[...].astype(bf16)) + b_ref[...]


def _ada_call(c_all, w_ada, b_ada):
    m = c_all.shape[0]
    n = w_ada.shape[1]
    return pl.pallas_call(
        _ada_body,
        out_shape=jax.ShapeDtypeStruct((m, n // D_MODEL, 1, D_MODEL), f32),
        grid=(n // D_MODEL,),
        in_specs=[
            pl.BlockSpec((m, D_MODEL), lambda i: (0, 0)),
            pl.BlockSpec((D_MODEL, D_MODEL), lambda i: (0, i)),
            pl.BlockSpec((1, D_MODEL), lambda i: (0, i)),
        ],
        out_specs=pl.BlockSpec((m, 1, 1, D_MODEL), lambda i: (0, i, 0, 0)),
        compiler_params=pltpu.CompilerParams(
            dimension_semantics=("arbitrary",), vmem_limit_bytes=VMEM_LIMIT_BYTES),
        name="ada_proj",
    )(c_all, w_ada, b_ada)


def _inproj_parts(x_ref, shift_ref, scale_ref, g_ref, wbig_ref, wsh_ref, wsl_ref,
                  zb_ref, zs_ref, tail_ref, *, bt, tt):
    rows = bt * tt
    h = _rms(x_ref[...]) * g_ref[...]
    h = h * (1.0 + _mod_rows(scale_ref, bt, tt)) + _mod_rows(shift_ref, bt, tt)
    hh = h.astype(bf16)
    hl = (h - hh.astype(f32)).astype(bf16)

    def narrow():
        wsh = wsh_ref[...]
        rhs = jnp.concatenate([jnp.concatenate([wsh, wsl_ref[...]], axis=1),
                               jnp.concatenate([wsh, jnp.zeros_like(wsh)], axis=1)], axis=0)
        t = _mm(jnp.concatenate([hh, hl], axis=1), rhs)
        zs_ref[...] = t[:, :N_SMALL] + t[:, N_SMALL:]

    def wide(idx, lo=0, width=PROJ_COLS):
        c = idx * PROJ_COLS + lo
        z = _mm(hh, wbig_ref[:, c:c + width])
        if isinstance(zb_ref, (list, tuple)):
            zb_ref[idx][:, lo:lo + width] = z
        else:
            zb_ref[:, c:c + width] = z.astype(zb_ref.dtype)
        if idx == OFF_QKA // PROJ_COLS:
            tail_ref[0, :, lo:lo + width] = z[rows - SUBLANES:rows, :]

    return narrow, wide


def _inproj_body(*refs, bt, tt):
    narrow, wide = _inproj_parts(*refs, bt=bt, tt=tt)
    narrow()
    for idx in range(N_BIG // PROJ_COLS):
        wide(idx)


def _const_spec(shape):
    nd = len(shape)
    return pl.BlockSpec(shape, lambda i, j: (0,) * nd, pipeline_mode=pl.Buffered(1))


def _mod_spec(bt, k, row0):
    assert row0 % bt == 0
    return pl.BlockSpec((bt, 1, 1, D_MODEL), lambda i, j: (row0 // bt + i, k, 0, 0))


def _inproj_call(x2d, ada4, g_norm, w_big, ws_hi, ws_lo, *, nb, nt, bt, tt, zdtype):
    ada, row0 = ada4
    rows = bt * tt
    total = x2d.shape[0]
    row_spec = lambda n: pl.BlockSpec((rows, n), lambda i, j: (i * nt + j, 0))
    return pl.pallas_call(
        functools.partial(_inproj_body, bt=bt, tt=tt),
        out_shape=(
            jax.ShapeDtypeStruct((total, N_BIG), zdtype),
            jax.ShapeDtypeStruct((total, N_SMALL), f32),
            jax.ShapeDtypeStruct((nb * nt, SUBLANES, PROJ_COLS), f32),
        ),
        grid=(nb, nt),
        in_specs=[
            row_spec(D_MODEL), _mod_spec(bt, 0, row0), _mod_spec(bt, 1, row0),
            _const_spec((1, D_MODEL)), _const_spec((D_MODEL, N_BIG)),
            _const_spec((D_MODEL, N_SMALL)), _const_spec((D_MODEL, N_SMALL)),
        ],
        out_specs=(
            row_spec(N_BIG), row_spec(N_SMALL),
            pl.BlockSpec((1, SUBLANES, PROJ_COLS), lambda i, j: (i * nt + j, 0, 0)),
        ),
        compiler_params=pltpu.CompilerParams(
            dimension_semantics=("arbitrary", "arbitrary"), vmem_limit_bytes=VMEM_LIMIT_BYTES),
        name="in_proj",
    )(x2d, ada, ada, g_norm, w_big, ws_hi, ws_lo)


N_MIXER_CONSTS = 7
N_MIXER_INIT = 5
N_MIXER_OUTS = 5


def _mixer_scratch(rows):
    return [pltpu.VMEM((rows, QK), f32)] * 5 + [
        pltpu.VMEM((rows, N_SMALL), f32), pltpu.VMEM((rows, N_SMALL), f32), pltpu.VMEM((rows, QK), f32),
    ] + [pltpu.VMEM((rows, D_MODEL), f32)] * 4 + [
        pltpu.VMEM((SUBLANES, 2 * QK), f32),
        pltpu.VMEM((2, N_HEADS, DK, DVA), f32), pltpu.VMEM((2, N_HEADS, DK, DV), f32),
        pltpu.VMEM((2, 1, N_SMALL), f32),
    ]


def _mixer_body(*refs, L, vlo, vhi, sample):
    zb_ref, zs_ref = refs[:2]
    k = 2
    init = None
    if sample:
        init = refs[k:k + N_MIXER_INIT]
        k += N_MIXER_INIT
    consts = refs[k:k + N_MIXER_CONSTS]
    k += N_MIXER_CONSTS
    outs = refs[k:k + N_MIXER_OUTS]
    assert sample, "the prompt stream runs in the fused projection + mixer call"
    zx_s, zsx_s, ux_s = refs[-3:]
    n_tok = vhi - vlo
    for c in range(0, N_BIG, PROJ_COLS):
        zx_s[:, c:c + PROJ_COLS] = _spread_tokens(zb_ref[:, c:c + PROJ_COLS], vlo, n_tok)
    zsx_s[...] = _spread_tokens(zs_ref[...], vlo, n_tok)
    _mixer_compute(zx_s, zsx_s, init, consts, (ux_s,) + tuple(outs[1:]), refs[k + N_MIXER_OUTS:-3],
                   L=L, vlo=vlo, vhi=vhi)
    outs[0][...] = _gather_tokens(ux_s[...], vlo, n_tok).astype(outs[0].dtype)


def _fused_body(x_ref, shift_ref, scale_ref, g_ref, wbig_ref, wsh_ref, wsl_ref, *refs, L):
    consts = refs[:N_MIXER_CONSTS]
    tail_ref = refs[N_MIXER_CONSTS]
    outs = refs[N_MIXER_CONSTS + 1:N_MIXER_CONSTS + 1 + N_MIXER_OUTS]
    k = N_MIXER_CONSTS + 1 + N_MIXER_OUTS
    zb_s = list(refs[k:k + N_BIG // PROJ_COLS])
    zs_s = refs[k + N_BIG // PROJ_COLS]
    scratch = refs[k + N_BIG // PROJ_COLS + 1:]
    _zero_carried_state(scratch)

    def step(slots):
        narrow, wide = _inproj_parts(x_ref, shift_ref, scale_ref, g_ref, wbig_ref, wsh_ref, wsl_ref,
                                     zb_s, zs_s, tail_ref, bt=1, tt=L)

        order = (OFF_QKA, OFF_QB, OFF_GA, OFF_OA, OFF_GB, OFF_RB, OFF_VA, OFF_VB)
        pieces = [(off // PROJ_COLS, lo) for off in order for lo in range(0, PROJ_COLS, PIECE_COLS)]
        emitted = []

        def tick(n=1):
            if not emitted:
                narrow()
            for _ in range(n):
                if len(emitted) < len(pieces):
                    idx, lo = pieces[len(emitted)]
                    wide(idx, lo, PIECE_COLS)
                    emitted.append((idx, lo))

        def need(off, width):
            wanted = [(off // PROJ_COLS, lo) for lo in range(0, PROJ_COLS, PIECE_COLS)
                      if lo < off % PROJ_COLS + width and lo + PIECE_COLS > off % PROJ_COLS]
            while any(w not in emitted for w in wanted):
                tick()

        _mixer_compute(zb_s, zs_s, None, consts, outs, scratch, L=L, vlo=0, vhi=L, slots=slots,
                       hooks=(tick, need))

    parity = lax.rem(pl.program_id(1), 2)
    last = pl.program_id(1) == pl.num_programs(1) - 1
    pl.when(parity == 0)(functools.partial(step, (0, 1, last)))
    pl.when(parity == 1)(functools.partial(step, (1, 0, last)))


def _zero_carried_state(scratch):
    carry_s, caug_s, sst_s, mst_s = scratch[-4:]

    @pl.when(pl.program_id(1) == 0)
    def _zero():
        caug_s[0] = jnp.zeros(caug_s.shape[1:], f32)
        sst_s[0] = jnp.zeros(sst_s.shape[1:], f32)
        mst_s[0] = jnp.zeros(mst_s.shape[1:], f32)
        carry_s[...] = jnp.zeros(carry_s.shape, f32)


def _mixer_compute(zb_ref, zs_ref, init, consts, outs, scratch, *, L, vlo, vhi, slots=None,
                   hooks=(lambda n=1: None, lambda off, width: None)):
    sample = init is not None
    if sample:
        conv0_ref, C0_ref, n0_ref, m0_ref, S0_ref = init
    cw_ref, cb_ref, gb_ref, wal_ref, bal_ref, gha_ref, ghb_ref = consts
    u_ref, Co_ref, no_ref, mo_ref, So_ref = outs
    (qa_s, ka_s, qh_s, kh_s, kb_s, x_s, r_s, bg_s, ha_s, hb_s, pa_s, pb_s,
     carry_s, caug_s, sst_s, mst_s) = scratch
    if not sample:
        cur, nxt, last_tile = slots
    rows = zs_ref.shape[0]
    every = slice(None)

    tick, need = hooks

    def zcols(rsel, off, width):
        need(off, width)
        if isinstance(zb_ref, (list, tuple)):
            grp, o = divmod(off, PROJ_COLS)
            assert o + width <= PROJ_COLS
            return zb_ref[grp][rsel, o:o + width]
        return zb_ref[rsel, off:off + width]
    n_groups = rows // L
    log2_l = L.bit_length() - 1
    masked = not (vlo == 0 and vhi == L)
    scale = DK ** -0.5

    row_in = lax.broadcasted_iota(jnp.int32, (rows, 1), 0) & (L - 1)
    valid = (row_in >= vlo) & (row_in < vhi)

    cw = cw_ref[...]
    cb = cb_ref[...]
    for c0 in range(0, 2 * QK, PIECE_COLS):
        cs = slice(c0, c0 + PIECE_COLS)
        zq = zcols(every, OFF_QKA + c0, PIECE_COLS).astype(f32)
        if sample:
            zq = jnp.where(row_in < CONV_W - 1, conv0_ref[:, cs], zq)
            carry = jnp.zeros((SUBLANES, PIECE_COLS), f32)
        else:
            carry = carry_s[:, cs]
            carry_s[:, cs] = zq[rows - SUBLANES:rows, :]
        xe = jnp.concatenate([carry, zq], axis=0)
        conv = cb[:, cs] + zq * cw[CONV_W - 1:CONV_W, cs]
        for k in range(CONV_W - 1):
            o = SUBLANES - (CONV_W - 1) + k
            conv = conv + xe[o:o + rows] * cw[k:k + 1, cs]
        qk = conv * _sigmoid(conv)
        if c0 < QK:
            qa_s[:, cs] = qk * scale
        else:
            ka_s[:, c0 - QK:c0 - QK + PIECE_COLS] = qk
        tick()

    lane = lax.broadcasted_iota(jnp.int32, (1, N_SMALL), 1)
    zs = zs_ref[...]
    pre = zs + gb_ref[...]
    gate = jnp.where(lane < LANE_FG, pre, _log_sigmoid(pre))
    gate_f = jnp.where(valid, gate, 0.0) if masked else gate
    rr = lax.broadcasted_iota(jnp.int32, (rows, rows), 0)
    cc = lax.broadcasted_iota(jnp.int32, (rows, rows), 1)
    tril_mask = ((rr >> log2_l) == (cc >> log2_l)) & (cc <= rr)
    tril = jnp.where(tril_mask, 1.0, 0.0).astype(bf16)

    la_lin = _mm(zs.astype(bf16), wal_ref[...]) + bal_ref[...]
    tick(2)

    la_parts, kb_parts = [], []
    for h in range(N_HEADS):
        ks = slice(h * DK, (h + 1) * DK)
        la_h = _log_sigmoid(la_lin[:, ks]) * (1.0 / GLA_TAU)
        la_parts.append(jnp.where(valid, la_h, 0.0) if masked else la_h)
        tick()

    both = jnp.concatenate([gate_f] + la_parts, axis=1)
    wb = N_SMALL + QK
    t = _mm(tril, jnp.concatenate(_split3(both), axis=1))
    csum = t[:, :wb] + (t[:, wb:2 * wb] + t[:, 2 * wb:])
    tick(2)
    x_all = jnp.where(lane < LANE_FG, gate, csum[:, :N_SMALL])
    x_s[...] = x_all
    b_al = pltpu.roll(x_all, N_SMALL - N_HEADS, axis=1)
    r_all = x_all - b_al
    r_ok = (lane < N_HEADS) & valid if masked else jnp.broadcast_to(lane < N_HEADS, (rows, N_SMALL))
    r_s[...] = jnp.where(r_ok, r_all, NEG)
    range_m = jnp.max(jnp.where(r_ok, r_all, NEG)) - jnp.min(jnp.where(r_ok, r_all, -NEG))
    bg = csum[:, N_SMALL:]
    bg_s[...] = bg
    range_g = jnp.max(-bg)
    tick()
    for h in range(N_HEADS):
        ks = slice(h * DK, (h + 1) * DK)
        kb_h = zcols(every, OFF_KB + h * DK, DK).astype(f32)
        kb_parts.append(jnp.where(valid, kb_h, 0.0) if masked else kb_h)
        qh_s[:, ks] = zcols(every, OFF_QB + h * DK, DK).astype(f32) * scale * jnp.exp(bg[:, ks])
        kh_s[:, ks] = kb_parts[h] * jnp.exp(-bg[:, ks])
        tick()
    kb = jnp.concatenate(kb_parts, axis=1)
    fast_ok = jnp.logical_and(range_m <= SAFE_LOG_RANGE, range_g <= SAFE_LOG_RANGE)

    for c0 in range(0, D_MODEL, PIECE_COLS):
        cs = slice(c0, c0 + PIECE_COLS)
        col = lambda off: zcols(every, off + c0, PIECE_COLS).astype(f32)
        pa_s[:, cs] = _sigmoid(col(OFF_GA)) * _sigmoid(col(OFF_OA)) * gha_ref[:, cs]
        tick(2)
    for c0 in range(0, D_MODEL, PIECE_COLS):
        cs = slice(c0, c0 + PIECE_COLS)
        col = lambda off: zcols(every, off + c0, PIECE_COLS).astype(f32)
        rb = col(OFF_RB)
        pb_s[:, cs] = _sigmoid(col(OFF_GB)) * (rb * _sigmoid(rb)) * ghb_ref[:, cs]
        tick(2)
    need(OFF_VA, D_MODEL)
    need(OFF_VB, D_MODEL)

    tri = lax.broadcasted_iota(jnp.int32, (L, L), 0) >= lax.broadcasted_iota(jnp.int32, (L, L), 1)
    eye = lax.broadcasted_iota(jnp.int32, (DK, DK), 0) == lax.broadcasted_iota(jnp.int32, (DK, DK), 1)
    lane_h = lane
    lrow = lax.broadcasted_iota(jnp.int32, (1, L), 1)
    lcol = lax.broadcasted_iota(jnp.int32, (L, 1), 0)
    vrow = (lrow >= vlo) & (lrow < vhi)
    ones16 = jnp.ones((L, DK), bf16)

    def load_state(g, h):
        if sample:
            n_mat = jnp.broadcast_to(_row_to_col(n0_ref[g, pl.ds(h, 1), :], eye), (DK, DK))
            return jnp.concatenate([C0_ref[g, h], n_mat], axis=1), S0_ref[g, h]
        return caug_s[cur, h], sst_s[cur, h]

    def emit_state(g, h, caug, s_new):
        Co_ref[g, h] = caug[:, :DV]
        no_ref[g, pl.ds(h, 1), :] = _diag_row(caug[:, DV:], eye)
        So_ref[g, h] = s_new

    def store_state(g, h, caug, s_new):
        if sample:
            emit_state(g, h, caug, s_new)
        else:
            caug_s[nxt, h] = caug
            sst_s[nxt, h] = s_new

    def load_m(g):
        return m0_ref[g] if sample else mst_s[cur]

    def store_m(g, m_vec):
        if sample:
            mo_ref[g] = m_vec
        else:
            mst_s[nxt] = m_vec

    def group_fast(g):
        r0 = g * L
        rs = pl.ds(r0, L)
        xc = x_s[rs, :]
        rc = r_s[rs, :]
        m_prev = load_m(g)
        bg_tail = bg_s[pl.ds(r0 + L - SUBLANES, SUBLANES), :]
        heads = []
        for h in range(N_HEADS):
            ks = slice(h * DK, (h + 1) * DK)
            caug, s_st = load_state(g, h)
            c0 = jnp.max(rc[:, h:h + 1], axis=0, keepdims=True)
            q16 = qa_s[rs, ks].astype(bf16)
            kt16 = (ka_s[rs, ks] * jnp.exp(rc[:, h:h + 1] - c0)).astype(bf16)
            qh16 = qh_s[rs, ks].astype(bf16)
            kh = kh_s[rs, ks]
            heads.append(dict(
                caug=caug, s_st=s_st, c0=c0, kt16=kt16, kh=kh,
                va=jnp.concatenate([zcols(rs, OFF_VA + h * DV, DV).astype(bf16), ones16], axis=1),
                vb16=zcols(rs, OFF_VB + h * DV, DV).astype(bf16),
                s_raw=_mm_nt(q16, kt16), a_raw=_mm_nt(qh16, kh.astype(bf16)),
                ci=_mm(q16, caug.astype(bf16)), oi=_mm(qh16, s_st.astype(bf16))))
        for p in heads:
            p["ni"] = _mm(jnp.where(tri, p["s_raw"], 0.0).astype(bf16), p["va"])
            p["o"] = _mm(jnp.where(tri, p["a_raw"], 0.0).astype(bf16), p["vb16"]) + p["oi"]
        for h, p in enumerate(heads):
            p["eb_last"] = jnp.exp(bg_tail[SUBLANES - 1:SUBLANES, h * DK:(h + 1) * DK])
            p["kv"] = _mm_tn(p["kt16"], p["va"])
            p["skv"] = _mm_tn((p["kh"] * p["eb_last"]).astype(bf16), p["vb16"])
        m_vec = m_prev
        for h, p in enumerate(heads):
            m_st = m_prev[:, h:h + 1]
            b_col = xc[:, LANE_FG + h:LANE_FG + h + 1]
            b_last = b_col[L - 1:L, :]
            c0 = p["c0"]
            mm = jnp.maximum(c0, m_st)
            comb = jnp.exp(c0 - mm) * p["ni"] + jnp.exp(m_st - mm) * p["ci"]
            inv = 1.0 / jnp.maximum(jnp.abs(comb[:, DV:]), jnp.exp(-(b_col + mm)))
            ha_s[rs, h * DV:h * DV + DK] = comb[:, :DK] * inv
            ha_s[rs, h * DV + DK:(h + 1) * DV] = comb[:, DK:DV] * inv
            hb_s[rs, h * DV:(h + 1) * DV] = p["o"]
            m_new = jnp.maximum(b_last + m_st, b_last + c0)
            caug_new = jnp.exp(b_last + m_st - m_new) * p["caug"] + jnp.exp(b_last + c0 - m_new) * p["kv"]
            s_new = _row_to_col(p["eb_last"], eye) * p["s_st"] + p["skv"]
            store_state(g, h, caug_new, s_new)
            m_vec = jnp.where(lane_h == h, m_new, m_vec)
        store_m(g, m_vec)

    def group_exact(g):
        static = isinstance(g, int)
        r0 = g * L if static else pl.multiple_of(g * L, L)
        tail0 = r0 + L - SUBLANES if static else pl.multiple_of(r0 + L - SUBLANES, SUBLANES)
        rs = pl.ds(r0, L)
        xc = x_s[rs, :]
        rc = r_s[rs, :]
        sr = lax.broadcasted_iota(jnp.int32, (2 * SUBLANES, N_SMALL), 0)
        sc = lax.broadcasted_iota(jnp.int32, (2 * SUBLANES, N_SMALL), 1)
        sel = jnp.where((sr < N_HEADS) & (sc == sr + LANE_IG), 1.0,
                        jnp.where((sr < N_HEADS) & (sc == sr + LANE_FG), -1.0, 0.0)).astype(bf16)
        xh, xm, xl = _split3(xc)
        r_rows = _mm_nt(sel, xh) + (_mm_nt(sel, xm) + _mm_nt(sel, xl))
        m_prev = load_m(g)
        m_vec = m_prev
        bg_tail = bg_s[pl.ds(tail0, SUBLANES), :]
        for h in range(N_HEADS):
            ks = slice(h * DK, (h + 1) * DK)
            vs = slice(h * DV, (h + 1) * DV)
            caug, s_st = load_state(g, h)
            m_st = m_prev[:, h:h + 1]
            b_col = xc[:, LANE_FG + h:LANE_FG + h + 1]
            r_col = rc[:, h:h + 1]
            b_last = b_col[L - 1:L, :]
            v16 = zcols(rs, OFF_VA + h * DV, DV).astype(bf16)
            vb16 = zcols(rs, OFF_VB + h * DV, DV).astype(bf16)
            eb_last = jnp.exp(bg_tail[SUBLANES - 1:SUBLANES, ks])
            eb_col = _row_to_col(eb_last, eye)
            c_st = caug[:, :DV]
            n_row = _diag_row(caug[:, DV:], eye)
            r_row = r_rows[h:h + 1, :]
            if masked:
                r_row = jnp.where(vrow, r_row, NEG)
            dm = jnp.where(tri, b_col + r_row, NEG)
            m_t = jnp.maximum(jnp.max(dm, axis=1, keepdims=True), b_col + m_st)
            w = jnp.exp(dm - m_t)
            wi = jnp.exp(b_col + m_st - m_t)
            q = qa_s[rs, ks]
            k = ka_s[rs, ks]
            q16 = q.astype(bf16)
            s = _mm_nt(q16, k.astype(bf16)) * w
            num = _mm(s.astype(bf16), v16) + wi * _mm(q16, c_st.astype(bf16))
            den = jnp.sum(s, axis=1, keepdims=True) + wi * jnp.sum(q * n_row, axis=1, keepdims=True)
            ha_s[rs, vs] = num / jnp.maximum(jnp.abs(den), jnp.exp(-m_t))
            m_new = jnp.maximum(b_last + m_st, b_last + jnp.max(r_row, axis=1, keepdims=True))
            wc = jnp.exp(b_last + m_st - m_new)
            kt = k * jnp.exp(b_last + r_col - m_new)
            n_new = wc * n_row + jnp.sum(kt, axis=0, keepdims=True)
            caug_new = jnp.concatenate(
                [wc * c_st + _mm_tn(kt.astype(bf16), v16),
                 jnp.broadcast_to(_row_to_col(n_new, eye), (DK, DK))], axis=1)
            qb = zcols(rs, OFF_QB + h * DK, DK).astype(f32) * scale
            bgc = bg_s[rs, ks]
            sub = lax.broadcasted_iota(jnp.int32, (SUBLANES, 1), 0)

            def score_col(s_i, a, ks=ks, qb=qb, bgc=bgc, sub=sub):
                s8 = lax.shift_left(lax.shift_right_logical(s_i, 3), 3)
                base = pl.multiple_of(r0 + s8, SUBLANES)
                pick = sub == (s_i - s8)
                b_s = jnp.sum(jnp.where(pick, bg_s[pl.ds(base, SUBLANES), ks], 0.0), axis=0, keepdims=True)
                k_s = jnp.sum(jnp.where(pick, kb_s[pl.ds(base, SUBLANES), ks], 0.0), axis=0, keepdims=True)
                e = jnp.exp(jnp.where(lcol >= s_i, bgc - b_s, NEG))
                col = jnp.sum(qb * e * k_s, axis=1, keepdims=True)
                return jnp.where(lrow == s_i, col, a)

            a = lax.fori_loop(0, L, score_col, jnp.zeros((L, L), f32))
            hb_s[rs, vs] = (_mm(a.astype(bf16), vb16)
                            + _mm((qb * jnp.exp(bgc)).astype(bf16), s_st.astype(bf16)))
            kt_b = kb_s[rs, ks] * jnp.exp(bg_tail[SUBLANES - 1:SUBLANES, ks] - bgc)
            s_new = eb_col * s_st + _mm_tn(kt_b.astype(bf16), vb16)
            store_state(g, h, caug_new, s_new)
            m_vec = jnp.where(lane_h == h, m_new, m_vec)
        store_m(g, m_vec)

    def groups_fast_batched():
        r_all_m = r_s[...]
        c0_l, m_l, mm_l, fac, wc = [], [], [], [], []
        for g in range(n_groups):
            gs = slice(g * L, (g + 1) * L)
            c0_g = jnp.max(r_all_m[gs], axis=0, keepdims=True)
            m_g = m0_ref[g]
            b_last = b_al[gs][L - 1:L, :]
            m_new = jnp.maximum(b_last + m_g, b_last + c0_g)
            mo_ref[g] = m_new
            c0_l.append(c0_g)
            m_l.append(m_g)
            mm_l.append(jnp.maximum(c0_g, m_g))
            fac.append(jnp.exp(b_last + c0_g - m_new))
            wc.append(jnp.exp(b_last + m_g - m_new))
        rows_of = lambda lst: jnp.concatenate([jnp.broadcast_to(v, (L, N_SMALL)) for v in lst], axis=0)
        c0_r, m_r, mm_r = rows_of(c0_l), rows_of(m_l), rows_of(mm_l)
        beta = jnp.exp(r_all_m - c0_r)
        gi_r = jnp.exp(c0_r - mm_r)
        wi_r = jnp.exp(m_r - mm_r)
        floor = jnp.exp(-(b_al + mm_r))
        ones_g = jnp.ones((L, DK), bf16)
        for h in range(N_HEADS):
            ks = slice(h * DK, (h + 1) * DK)
            vs = slice(h * DV, (h + 1) * DV)
            kt = ka_s[:, ks] * beta[:, h:h + 1]
            v = zcols(every, OFF_VA + h * DV, DV)
            vb = zcols(every, OFF_VB + h * DV, DV)
            q16 = qa_s[:, ks].astype(bf16)
            va = jnp.concatenate([v.astype(bf16), jnp.ones((rows, DK), bf16)], axis=1)
            s16 = jnp.where(tril_mask, _mm_nt(q16, kt.astype(bf16)), 0.0).astype(bf16)
            ni = _mm(s16, va)
            qh16 = qh_s[:, ks].astype(bf16)
            kh = kh_s[:, ks]
            a16 = jnp.where(tril_mask, _mm_nt(qh16, kh.astype(bf16)), 0.0).astype(bf16)
            o_intra = _mm(a16, vb.astype(bf16))
            ci_parts, oi_parts = [], []
            for g in range(n_groups):
                gs = slice(g * L, (g + 1) * L)
                caug, s_st = load_state(g, h)
                va_g = jnp.concatenate([v[gs].astype(bf16), ones_g], axis=1)
                ci_parts.append(_mm(qa_s[gs, ks].astype(bf16), caug.astype(bf16)))
                caug_new = wc[g][:, h:h + 1] * caug + fac[g][:, h:h + 1] * _mm_tn(kt[gs].astype(bf16), va_g)
                oi_parts.append(_mm(qh_s[gs, ks].astype(bf16), s_st.astype(bf16)))
                eb_last = jnp.exp(bg_s[gs, ks][L - 1:L, :])
                s_new = (_row_to_col(eb_last, eye) * s_st
                         + _mm_tn((kh[gs] * eb_last).astype(bf16), vb[gs].astype(bf16)))
                store_state(g, h, caug_new, s_new)
            comb = gi_r[:, h:h + 1] * ni + wi_r[:, h:h + 1] * jnp.concatenate(ci_parts, axis=0)
            inv = 1.0 / jnp.maximum(jnp.abs(comb[:, DV:]), floor[:, h:h + 1])
            ha_s[:, h * DV:h * DV + DK] = comb[:, :DK] * inv
            ha_s[:, h * DV + DK:(h + 1) * DV] = comb[:, DK:DV] * inv
            hb_s[:, vs] = o_intra + jnp.concatenate(oi_parts, axis=0)

    def run_groups(exact):
        if not exact:
            if sample and L == SUBLANES:
                groups_fast_batched()
            else:
                for g in range(n_groups):
                    group_fast(g)
        elif n_groups == 1:
            group_exact(0)
        else:
            def body(g, carry_):
                group_exact(g)
                return carry_
            lax.fori_loop(0, n_groups, body, 0)

    def merge_heads():
        for h in range(N_HEADS):
            vs = slice(h * DV, (h + 1) * DV)
            u = pa_s[:, vs] * _rms(ha_s[:, vs]) + pb_s[:, vs] * _rms(hb_s[:, vs])
            u_ref[:, vs] = u.astype(u_ref.dtype)

    run_groups(False)
    merge_heads()

    @pl.when(jnp.logical_not(fast_ok))
    def _exact():
        kb_s[...] = kb
        run_groups(True)
        merge_heads()

    if not sample:
        @pl.when(last_tile)
        def _emit_final_states():
            for h in range(N_HEADS):
                emit_state(0, h, caug_s[nxt, h], sst_s[nxt, h])
            mo_ref[0] = mst_s[nxt]


def _mixer_call(zb, zs, init, consts, *, nb, nt, bt, L, vlo, vhi):
    sample = init is not None
    rows = bt * L
    tok_rows = bt * (vhi - vlo) if sample else rows
    n_state = nb * bt
    row_spec = lambda n, r=rows: pl.BlockSpec((r, n), lambda i, j: (i * nt + j, 0))
    st4 = pl.BlockSpec((bt, N_HEADS, DK, DV), lambda i, j: (i, 0, 0, 0))
    st_n = pl.BlockSpec((bt, N_HEADS, DK), lambda i, j: (i, 0, 0))
    st_m = pl.BlockSpec((bt, 1, N_SMALL), lambda i, j: (i, 0, 0))
    in_specs = [row_spec(N_BIG, tok_rows), row_spec(N_SMALL, tok_rows)]
    args = [zb, zs]
    scratch = _mixer_scratch(rows)
    if sample:
        in_specs += [row_spec(2 * QK), st4, st_n, st_m, st4]
        args += list(init)
        scratch += [pltpu.VMEM((rows, N_BIG), f32), pltpu.VMEM((rows, N_SMALL), f32),
                    pltpu.VMEM((rows, D_MODEL), f32)]
    in_specs += [_const_spec(c.shape) for c in consts]
    args += list(consts)
    return pl.pallas_call(
        functools.partial(_mixer_body, L=L, vlo=vlo, vhi=vhi, sample=sample),
        out_shape=(
            jax.ShapeDtypeStruct((zb.shape[0], D_MODEL), bf16),
            jax.ShapeDtypeStruct((n_state, N_HEADS, DK, DV), f32),
            jax.ShapeDtypeStruct((n_state, N_HEADS, DK), f32),
            jax.ShapeDtypeStruct((n_state, 1, N_SMALL), f32),
            jax.ShapeDtypeStruct((n_state, N_HEADS, DK, DV), f32),
        ),
        grid=(nb, nt),
        in_specs=in_specs,
        out_specs=(row_spec(D_MODEL, tok_rows), st4, st_n, st_m, st4),
        scratch_shapes=scratch,
        compiler_params=pltpu.CompilerParams(
            dimension_semantics=("arbitrary", "arbitrary"), vmem_limit_bytes=VMEM_LIMIT_BYTES),
        name="mixer_sample" if sample else "mixer_prompt",
    )(*args)


def _fused_call(x2d, ada4, g_norm, w_big, ws_hi, ws_lo, consts, *, nb, nt, L):
    ada, row0 = ada4
    row_spec = lambda n: pl.BlockSpec((L, n), lambda i, j: (i * nt + j, 0))
    st4 = pl.BlockSpec((1, N_HEADS, DK, DV), lambda i, j: (i, 0, 0, 0))
    st_n = pl.BlockSpec((1, N_HEADS, DK), lambda i, j: (i, 0, 0))
    st_m = pl.BlockSpec((1, 1, N_SMALL), lambda i, j: (i, 0, 0))
    return pl.pallas_call(
        functools.partial(_fused_body, L=L),
        out_shape=(
            jax.ShapeDtypeStruct((nb * nt, SUBLANES, PROJ_COLS), f32),
            jax.ShapeDtypeStruct((x2d.shape[0], D_MODEL), bf16),
            jax.ShapeDtypeStruct((nb, N_HEADS, DK, DV), f32),
            jax.ShapeDtypeStruct((nb, N_HEADS, DK), f32),
            jax.ShapeDtypeStruct((nb, 1, N_SMALL), f32),
            jax.ShapeDtypeStruct((nb, N_HEADS, DK, DV), f32),
        ),
        grid=(nb, nt),
        in_specs=[
            row_spec(D_MODEL), _mod_spec(1, 0, row0), _mod_spec(1, 1, row0),
            _const_spec((1, D_MODEL)), _const_spec((D_MODEL, N_BIG)),
            _const_spec((D_MODEL, N_SMALL)), _const_spec((D_MODEL, N_SMALL)),
        ] + [_const_spec(c.shape) for c in consts],
        out_specs=(
            pl.BlockSpec((1, SUBLANES, PROJ_COLS), lambda i, j: (i * nt + j, 0, 0)),
            row_spec(D_MODEL), st4, st_n, st_m, st4,
        ),
        scratch_shapes=([pltpu.VMEM((L, PROJ_COLS), f32)] * (N_BIG // PROJ_COLS)
                        + [pltpu.VMEM((L, N_SMALL), f32)] + _mixer_scratch(L)),
        compiler_params=pltpu.CompilerParams(
            dimension_semantics=("arbitrary", "arbitrary"), vmem_limit_bytes=VMEM_LIMIT_BYTES),
        name="proj_mixer_prompt",
    )(x2d, ada, ada, g_norm, w_big, ws_hi, ws_lo, *consts)


def _out_body(u_ref, x_ref, gate1_ref, shift2_ref, scale2_ref, gate2_ref, gn2_ref, gfin_ref,
              wout_ref, wg_ref, wu_ref, wd_ref, y_ref, *, bt, tt):
    x1 = x_ref[...] + _mod_rows(gate1_ref, bt, tt) * _mm(u_ref[...], wout_ref[...])
    h2 = _rms(x1) * gn2_ref[...]
    h2 = (h2 * (1.0 + _mod_rows(scale2_ref, bt, tt)) + _mod_rows(shift2_ref, bt, tt)).astype(bf16)
    acc = jnp.zeros(x1.shape, f32)
    for c in range(0, D_FF, FFN_COLS):
        g = _mm(h2, wg_ref[:, c:c + FFN_COLS])
        up = _mm(h2, wu_ref[:, c:c + FFN_COLS])
        acc = acc + _mm((g * _sigmoid(g) * up).astype(bf16), wd_ref[c:c + FFN_COLS, :])
    x2 = x1 + _mod_rows(gate2_ref, bt, tt) * acc
    y_ref[...] = _rms(x2) * gfin_ref[...]


N_OUT_MODS = 4
STAGE_WIDE_ROWS = 128
STAGE_TALL_ROWS = 256


def _load_cast(w_hbm, w16_ref, stage, sem):
    n_rows, _ = w_hbm.shape
    chunk = stage.shape[1]
    assert n_rows % chunk == 0
    n = n_rows // chunk

    def copy(k):
        slot = k % 2
        return pltpu.make_async_copy(w_hbm.at[pl.ds(k * chunk, chunk), :], stage.at[slot], sem.at[slot])

    copy(0).start()
    for k in range(n):
        if k + 1 < n:
            copy(k + 1).start()
        copy(k).wait()
        w16_ref[k * chunk:(k + 1) * chunk, :] = stage[k % 2].astype(bf16)


def _out_both_body(*refs, n_prompt, tt_p, bt_s, tt_s):
    up_ref, xp_ref = refs[:2]
    mods_p = refs[2:2 + N_OUT_MODS]
    us_ref, xs_ref = refs[2 + N_OUT_MODS:4 + N_OUT_MODS]
    mods_s = refs[4 + N_OUT_MODS:4 + 2 * N_OUT_MODS]
    gn2_ref, gfin_ref = refs[4 + 2 * N_OUT_MODS:6 + 2 * N_OUT_MODS]
    w_hbm = refs[6 + 2 * N_OUT_MODS:10 + 2 * N_OUT_MODS]
    yp_ref, ys_ref = refs[10 + 2 * N_OUT_MODS:12 + 2 * N_OUT_MODS]
    w16 = refs[12 + 2 * N_OUT_MODS:16 + 2 * N_OUT_MODS]
    stage_wide, stage_tall, sem = refs[16 + 2 * N_OUT_MODS:]
    shared = (gn2_ref, gfin_ref) + tuple(w16)
    step = pl.program_id(0)

    @pl.when(step == 0)
    def _load_weights():
        for w_ref, w16_ref in zip(w_hbm, w16):
            stage = stage_wide if w_ref.shape[1] == D_FF else stage_tall
            _load_cast(w_ref, w16_ref, stage, sem)

    @pl.when(step < n_prompt)
    def _prompt():
        _out_body(up_ref, xp_ref, *mods_p, *shared, yp_ref, bt=1, tt=tt_p)

    @pl.when(step == n_prompt)
    def _sample():
        _out_body(us_ref, xs_ref, *mods_s, *shared, ys_ref, bt=bt_s, tt=tt_s)


def _out_call(u_p, x_p, u_s, x_s, ada, g_norm2, g_final, w_out, w_gate, w_up, w_down, *,
              row0_p, tiles_per_seq, tt_p, bt_s, tt_s):
    n_prompt = x_p.shape[0] // tt_p
    rows_s = bt_s * tt_s
    assert x_s.shape[0] == rows_s
    tile_p = lambda s: jnp.minimum(s, n_prompt - 1)
    prompt_rows = pl.BlockSpec((tt_p, D_MODEL), lambda s: (tile_p(s), 0))
    sample_rows = lambda: pl.BlockSpec((rows_s, D_MODEL), lambda s: (0, 0), pipeline_mode=pl.Buffered(1))
    mod_p = lambda k: pl.BlockSpec((1, 1, 1, D_MODEL), lambda s: (row0_p + tile_p(s) // tiles_per_seq, k, 0, 0))
    mod_s = lambda k: pl.BlockSpec((bt_s, 1, 1, D_MODEL), lambda s: (0, k, 0, 0), pipeline_mode=pl.Buffered(1))
    const = lambda shape: pl.BlockSpec(shape, lambda s: (0, 0), pipeline_mode=pl.Buffered(1))
    ks = range(N_ADA - N_OUT_MODS, N_ADA)
    return pl.pallas_call(
        functools.partial(_out_both_body, n_prompt=n_prompt, tt_p=tt_p, bt_s=bt_s, tt_s=tt_s),
        out_shape=(jax.ShapeDtypeStruct(x_p.shape, f32), jax.ShapeDtypeStruct(x_s.shape, f32)),
        grid=(n_prompt + 1,),
        in_specs=(
            [prompt_rows, prompt_rows] + [mod_p(k) for k in ks]
            + [sample_rows(), sample_rows()] + [mod_s(k) for k in ks]
            + [const((1, D_MODEL)), const((1, D_MODEL))] + [pl.BlockSpec(memory_space=pl.ANY)] * 4),
        out_specs=(prompt_rows, pl.BlockSpec((rows_s, D_MODEL), lambda s: (0, 0))),
        scratch_shapes=[
            pltpu.VMEM((D_MODEL, D_MODEL), bf16), pltpu.VMEM((D_MODEL, D_FF), bf16),
            pltpu.VMEM((D_MODEL, D_FF), bf16), pltpu.VMEM((D_FF, D_MODEL), bf16),
            pltpu.VMEM((2, STAGE_WIDE_ROWS, D_FF), f32), pltpu.VMEM((2, STAGE_TALL_ROWS, D_MODEL), f32),
            pltpu.SemaphoreType.DMA((2,)),
        ],
        compiler_params=pltpu.CompilerParams(
            dimension_semantics=("arbitrary",), vmem_limit_bytes=VMEM_LIMIT_BYTES),
        name="out_ffn",
    )(u_p, x_p, *([ada] * N_OUT_MODS), u_s, x_s, *([ada] * N_OUT_MODS),
      g_norm2, g_final, w_out, w_gate, w_up, w_down)


PROMPT_CHUNK = 256
PROMPT_FFN_ROWS = 512
SAMPLE_L = SUBLANES
SAMPLE_MIX_GROUPS = 8
SAMPLE_DENSE_GROUPS = 64


W_IN_SEG_A = (0, 3072)
W_IN_GATES = 3072
W_IN_SEG_B = (3080, 6152)
W_IN_LR = 6152
W_IN_SEG_C = (6168, 8216)
LANES = 128


PREP_COLS = 512
SHIFT_B = W_IN_SEG_B[0] - OFF_QB
SHIFT_C = W_IN_SEG_C[0] - OFF_GA
PREP_HALO = 32


def _prep_body(a_ref, b_ref, gt_ref, lt_ref, wb_ref, wh_ref, wl_ref):
    j = pl.program_id(0)
    both = jnp.concatenate([a_ref[...], b_ref[...]], axis=0)

    def emit(shift):
        wb_ref[...] = both[shift:shift + PREP_COLS, :].T.astype(bf16)

    pl.when(j < OFF_QB // PREP_COLS)(lambda: emit(0))
    pl.when((j >= OFF_QB // PREP_COLS) & (j < OFF_GA // PREP_COLS))(lambda: emit(SHIFT_B))
    pl.when(j >= OFF_GA // PREP_COLS)(lambda: emit(SHIFT_C))

    @pl.when(j == 0)
    def _narrow():
        row = lax.broadcasted_iota(jnp.int32, (LANES, 1), 0)
        small = jnp.where(row < LANE_LR, gt_ref[...], jnp.where(row < LANE_LR + GLA_RANK, lt_ref[...], 0.0)).T
        hi = small.astype(bf16)
        wh_ref[...] = hi
        wl_ref[...] = (small - hi.astype(f32)).astype(bf16)


def _prep_call(w_in):
    assert W_IN_GATES % LANES == 0 and (W_IN_LR - LANE_LR) % LANES == 0
    assert max(SHIFT_B, SHIFT_C) <= PREP_HALO and SHIFT_B % SUBLANES == 0 and SHIFT_C % SUBLANES == 0
    wt = jnp.transpose(w_in)
    halo_per_block = PREP_COLS // PREP_HALO
    last = (wt.shape[0] - 1) // PREP_HALO
    const = lambda shape: pl.BlockSpec(shape, lambda j: (0, 0))
    return pl.pallas_call(
        _prep_body,
        out_shape=(
            jax.ShapeDtypeStruct((D_MODEL, N_BIG), bf16),
            jax.ShapeDtypeStruct((D_MODEL, N_SMALL), bf16),
            jax.ShapeDtypeStruct((D_MODEL, N_SMALL), bf16),
        ),
        grid=(N_BIG // PREP_COLS,),
        in_specs=[
            pl.BlockSpec((PREP_COLS, D_MODEL), lambda j: (j, 0)),
            pl.BlockSpec((PREP_HALO, D_MODEL), lambda j: (jnp.minimum((j + 1) * halo_per_block, last), 0)),
            pl.BlockSpec((LANES, D_MODEL), lambda j: (W_IN_GATES // LANES, 0)),
            pl.BlockSpec((LANES, D_MODEL), lambda j: ((W_IN_LR - LANE_LR) // LANES, 0)),
        ],
        out_specs=(
            pl.BlockSpec((D_MODEL, PREP_COLS), lambda j: (0, j)),
            const((D_MODEL, N_SMALL)), const((D_MODEL, N_SMALL)),
        ),
        compiler_params=pltpu.CompilerParams(
            dimension_semantics=("arbitrary",), vmem_limit_bytes=VMEM_LIMIT_BYTES),
        name="w_in_prep",
    )(wt, wt, wt, wt)


def _prep_layer_weights(w_in, w_alpha, b_igate, b_fgate):
    w_big, ws_hi, ws_lo = _prep_call(w_in)
    w_al = jnp.zeros((N_SMALL, QK), f32).at[LANE_LR:LANE_LR + GLA_RANK].set(w_alpha).astype(bf16)
    g_bias = jnp.concatenate([b_igate, b_fgate, jnp.zeros((N_SMALL - 2 * N_HEADS,), f32)])[None, :]
    return w_big, ws_hi, ws_lo, w_al, g_bias


def _layer(xp2d, xs2d, st_s, ada_p, ada_s, p, *, bp, tp, bs):
    (g_norm1, g_norm2, w_in, conv_w, conv_b, b_igate, b_fgate, g_head_a,
     w_alpha, b_alpha, g_head_b, w_out, w_ffn_gate, w_ffn_up, w_ffn_down, g_final_or_ones) = p
    w_big, ws_hi, ws_lo, w_al, g_bias = _prep_layer_weights(w_in, w_alpha, b_igate, b_fgate)
    consts = (conv_w, conv_b[None, :], g_bias, w_al, b_alpha[None, :],
              g_head_a.reshape(1, D_MODEL), g_head_b.reshape(1, D_MODEL))
    gn1, gn2, gfin = g_norm1[None, :], g_norm2[None, :], g_final_or_ones[None, :]

    ntp = tp // PROMPT_CHUNK
    tail, u, c_p, n_p, m_p, s_p = _fused_call(xp2d, ada_p, gn1, w_big, ws_hi, ws_lo, consts,
                                              nb=bp, nt=ntp, L=PROMPT_CHUNK)
    conv_p = tail.reshape(bp, ntp, SUBLANES, 2 * QK)[:, ntp - 1, SUBLANES - (CONV_W - 1):, :]

    conv0, c0, n0, m0, s0 = st_s
    lo = CONV_W - 1
    n_tok = SAMPLE_L - lo - 1
    nbd = bs // SAMPLE_DENSE_GROUPS
    zb_s, zs_s, _ = _inproj_call(xs2d, ada_s, gn1, w_big, ws_hi, ws_lo,
                                 nb=nbd, nt=1, bt=SAMPLE_DENSE_GROUPS, tt=n_tok, zdtype=f32)
    conv_s = zb_s[:, OFF_QKA:OFF_QKA + 2 * QK].reshape(bs, n_tok, 2 * QK)[:, n_tok - lo:, :]
    conv0_rows = jnp.pad(conv0, ((0, 0), (0, SAMPLE_L - lo), (0, 0))).reshape(bs * SAMPLE_L, 2 * QK)
    m0_lanes = jnp.pad(m0[:, None, :], ((0, 0), (0, 0), (0, N_SMALL - N_HEADS)))
    u_s, c_s, n_s, m_s, s_s = _mixer_call(
        zb_s, zs_s, (conv0_rows, c0, n0, m0_lanes, s0), consts,
        nb=bs // SAMPLE_MIX_GROUPS, nt=1, bt=SAMPLE_MIX_GROUPS, L=SAMPLE_L, vlo=lo, vhi=lo + n_tok)
    assert ada_s[1] == 0 and ada_p[0] is ada_s[0]
    yp, ys = _out_call(u, xp2d, u_s, xs2d, ada_p[0], gn2, gfin, w_out, w_ffn_gate, w_ffn_up, w_ffn_down,
                       row0_p=ada_p[1], tiles_per_seq=tp // PROMPT_FFN_ROWS, tt_p=PROMPT_FFN_ROWS,
                       bt_s=bs, tt_s=n_tok)
    st_p = (conv_p, c_p, n_p, m_p[:, 0, :N_HEADS], s_p)
    st_s_new = (conv_s, c_s, n_s, m_s[:, 0, :N_HEADS], s_s)
    return yp, ys, st_p, st_s_new


def kernel(x_prompt, x_sample, state_conv, state_C, state_n, state_m, state_S, c_prompt, c_sample,
           g_norm1, g_norm2, w_ada, b_ada, w_in, conv_w, conv_b, b_igate, b_fgate, g_head_a,
           w_alpha, b_alpha, g_head_b, w_out, w_ffn_gate, w_ffn_up, w_ffn_down, g_final):
    bp, tp, _ = x_prompt.shape
    bs, ts, _ = x_sample.shape
    depth = w_in.shape[0]
    assert depth == 1, "the final norm is fused into the (single) layer's FFN kernel"
    lo = CONV_W - 1
    assert ts == SAMPLE_L - lo - 1
    xp2d = x_prompt.reshape(bp * tp, D_MODEL)
    xs2d = x_sample.reshape(bs * ts, D_MODEL)
    c_all = jnp.concatenate([c_sample, c_prompt], axis=0)
    new_p = [[] for _ in range(5)]
    new_s = [[] for _ in range(5)]
    for l in range(depth):
        ada = _ada_call(c_all, w_ada[l], b_ada[l][None, :])
        p = (g_norm1[l], g_norm2[l], w_in[l], conv_w[l], conv_b[l], b_igate[l], b_fgate[l], g_head_a[l],
             w_alpha[l], b_alpha[l], g_head_b[l], w_out[l], w_ffn_gate[l], w_ffn_up[l], w_ffn_down[l], g_final)
        st_s = (state_conv[l], state_C[l], state_n[l], state_m[l], state_S[l])
        xp2d, xs2d, st_p, st_s_new = _layer(xp2d, xs2d, st_s, (ada, bs), (ada, 0), p, bp=bp, tp=tp, bs=bs)
        for j in range(5):
            new_p[j].append(st_p[j])
            new_s[j].append(st_s_new[j])
    y_prompt = xp2d.reshape(bp, tp, D_MODEL)
    y_sample = xs2d.reshape(bs, ts, D_MODEL)
    outs_p = [jnp.stack(a, axis=0) for a in new_p]
    outs_s = [jnp.stack(a, axis=0) for a in new_s]
    return (y_prompt, y_sample, *outs_p, *outs_s)
```

```python
import functools

import jax
import jax.numpy as jnp
from jax import lax
from jax.experimental import pallas as pl
from jax.experimental.pallas import tpu as pltpu

f32 = jnp.float32
bf16 = jnp.bfloat16

D_MODEL = 1024
N_HEADS = 4
DK = 128
DV = 256
QK = N_HEADS * DK
CONV_W = 4
GLA_RANK = 16
GLA_TAU = 16.0
D_FF = 2816
EPS = 1e-6
N_ADA = 6

N_BIG = 8192
N_SMALL = 128
OFF_QKA, OFF_VA, OFF_OA, OFF_QB, OFF_KB, OFF_VB, OFF_RB, OFF_GA, OFF_GB = (
    0, 1024, 2048, 3072, 3584, 4096, 5120, 6144, 7168)
LANE_IG, LANE_FG, LANE_LR = 0, N_HEADS, 2 * N_HEADS

SUBLANES = 8
NEG = -1e30
SAFE_LOG_RANGE = 60.0
V7X_VMEM_BYTES = 64 * 1024 * 1024
VMEM_LIMIT_BYTES = V7X_VMEM_BYTES - 8 * 1024 * 1024
FFN_COLS = 256
PROJ_COLS = 1024
PIECE_COLS = 256
DVA = DV + DK


def _mm(a, b):
    return jnp.dot(a, b, preferred_element_type=f32)


def _mm_nt(a, b):
    return lax.dot_general(a, b, (((1,), (1,)), ((), ())), preferred_element_type=f32)


def _mm_tn(a, b):
    return lax.dot_general(a, b, (((0,), (0,)), ((), ())), preferred_element_type=f32)


def _split3(x):
    hi = x.astype(bf16)
    r = x - hi.astype(f32)
    mid = r.astype(bf16)
    lo = (r - mid.astype(f32)).astype(bf16)
    return hi, mid, lo


def _sigmoid(x):
    return 0.5 * jnp.tanh(0.5 * x) + 0.5


def _log_sigmoid(x):
    return jnp.minimum(x, 0.0) - jnp.log(1.0 + jnp.exp(-jnp.abs(x)))


def _rms(x):
    return x * lax.rsqrt(jnp.mean(x * x, axis=-1, keepdims=True) + EPS)


def _row_to_col(row, eye):
    return jnp.sum(jnp.where(eye, row, 0.0), axis=1, keepdims=True)


def _diag_row(mat, eye):
    return jnp.sum(jnp.where(eye, mat, 0.0), axis=0, keepdims=True)


def _mod_rows(mod_ref, bt, tt):
    if bt == 1:
        return mod_ref[0, 0]
    if tt % SUBLANES == 0:
        v = mod_ref[...].reshape(bt, 1, D_MODEL)
        return jnp.broadcast_to(v, (bt, tt, D_MODEL)).reshape(bt * tt, D_MODEL)
    assert 2 * tt == SUBLANES and bt % 2 == 0
    first = lax.broadcasted_iota(jnp.int32, (SUBLANES, 1), 0) < tt
    return jnp.concatenate([jnp.where(first, mod_ref[2 * p, 0], mod_ref[2 * p + 1, 0]) for p in range(bt // 2)],
                           axis=0)


def _spread_tokens(c, lo, n_tok):
    assert 2 * n_tok == SUBLANES
    row = lax.broadcasted_iota(jnp.int32, (SUBLANES, 1), 0)
    mid = (row >= lo) & (row < lo + n_tok)
    tiles = []
    for p in range(c.shape[0] // SUBLANES):
        t = c[p * SUBLANES:(p + 1) * SUBLANES]
        tiles.append(jnp.where(mid, pltpu.roll(t, lo, axis=0), 0.0))
        tiles.append(jnp.where(mid, pltpu.roll(t, (lo + n_tok) % SUBLANES, axis=0), 0.0))
    return jnp.concatenate(tiles, axis=0)


def _gather_tokens(x, lo, n_tok):
    assert 2 * n_tok == SUBLANES
    first = lax.broadcasted_iota(jnp.int32, (SUBLANES, 1), 0) < n_tok
    tiles = []
    for p in range(x.shape[0] // (2 * SUBLANES)):
        even = x[2 * p * SUBLANES:(2 * p + 1) * SUBLANES]
        odd = x[(2 * p + 1) * SUBLANES:(2 * p + 2) * SUBLANES]
        tiles.append(jnp.where(first, pltpu.roll(even, SUBLANES - lo, axis=0),
                               pltpu.roll(odd, (SUBLANES - lo + n_tok) % SUBLANES, axis=0)))
    return jnp.concatenate(tiles, axis=0)


def _ada_body(c_ref, w_ref, b_ref, o_ref):
    o_ref[:, 0, 0, :] = _mm(c_ref[...].astype(bf16), w_ref[...].astype(bf16)) + b_ref[...]


def _ada_call(c_all, w_ada, b_ada):
    m = c_all.shape[0]
    n = w_ada.shape[1]
    return pl.pallas_call(
        _ada_body,
        out_shape=jax.ShapeDtypeStruct((m, n // D_MODEL, 1, D_MODEL), f32),
        grid=(n // D_MODEL,),
        in_specs=[
            pl.BlockSpec((m, D_MODEL), lambda i: (0, 0)),
            pl.BlockSpec((D_MODEL, D_MODEL), lambda i: (0, i)),
            pl.BlockSpec((1, D_MODEL), lambda i: (0, i)),
        ],
        out_specs=pl.BlockSpec((m, 1, 1, D_MODEL), lambda i: (0, i, 0, 0)),
        compiler_params=pltpu.CompilerParams(
            dimension_semantics=("arbitrary",), vmem_limit_bytes=VMEM_LIMIT_BYTES),
        name="ada_proj",
    )(c_all, w_ada, b_ada)


def _inproj_parts(x_ref, shift_ref, scale_ref, g_ref, wbig_ref, wsh_ref, wsl_ref,
                  zb_ref, zs_ref, tail_ref, *, bt, tt):
    rows = bt * tt
    h = _rms(x_ref[...]) * g_ref[...]
    h = h * (1.0 + _mod_rows(scale_ref, bt, tt)) + _mod_rows(shift_ref, bt, tt)
    hh = h.astype(bf16)
    hl = (h - hh.astype(f32)).astype(bf16)

    def narrow():
        wsh = wsh_ref[...]
        rhs = jnp.concatenate([jnp.concatenate([wsh, wsl_ref[...]], axis=1),
                               jnp.concatenate([wsh, jnp.zeros_like(wsh)], axis=1)], axis=0)
        t = _mm(jnp.concatenate([hh, hl], axis=1), rhs)
        zs_ref[...] = t[:, :N_SMALL] + t[:, N_SMALL:]

    def wide(idx, lo=0, width=PROJ_COLS):
        c = idx * PROJ_COLS + lo
        z = _mm(hh, wbig_ref[:, c:c + width])
        if isinstance(zb_ref, (list, tuple)):
            zb_ref[idx][:, lo:lo + width] = z
        else:
            zb_ref[:, c:c + width] = z.astype(zb_ref.dtype)
        if idx == OFF_QKA // PROJ_COLS:
            tail_ref[0, :, lo:lo + width] = z[rows - SUBLANES:rows, :]

    return narrow, wide


def _inproj_body(*refs, bt, tt):
    narrow, wide = _inproj_parts(*refs, bt=bt, tt=tt)
    narrow()
    for idx in range(N_BIG // PROJ_COLS):
        wide(idx)


def _const_spec(shape):
    nd = len(shape)
    return pl.BlockSpec(shape, lambda i, j: (0,) * nd, pipeline_mode=pl.Buffered(1))


def _mod_spec(bt, k, row0):
    assert row0 % bt == 0
    return pl.BlockSpec((bt, 1, 1, D_MODEL), lambda i, j: (row0 // bt + i, k, 0, 0))


def _inproj_call(x2d, ada4, g_norm, w_big, ws_hi, ws_lo, *, nb, nt, bt, tt, zdtype):
    ada, row0 = ada4
    rows = bt * tt
    total = x2d.shape[0]
    row_spec = lambda n: pl.BlockSpec((rows, n), lambda i, j: (i * nt + j, 0))
    return pl.pallas_call(
        functools.partial(_inproj_body, bt=bt, tt=tt),
        out_shape=(
            jax.ShapeDtypeStruct((total, N_BIG), zdtype),
            jax.ShapeDtypeStruct((total, N_SMALL), f32),
            jax.ShapeDtypeStruct((nb * nt, SUBLANES, PROJ_COLS), f32),
        ),
        grid=(nb, nt),
        in_specs=[
            row_spec(D_MODEL), _mod_spec(bt, 0, row0), _mod_spec(bt, 1, row0),
            _const_spec((1, D_MODEL)), _const_spec((D_MODEL, N_BIG)),
            _const_spec((D_MODEL, N_SMALL)), _const_spec((D_MODEL, N_SMALL)),
        ],
        out_specs=(
            row_spec(N_BIG), row_spec(N_SMALL),
            pl.BlockSpec((1, SUBLANES, PROJ_COLS), lambda i, j: (i * nt + j, 0, 0)),
        ),
        compiler_params=pltpu.CompilerParams(
            dimension_semantics=("arbitrary", "arbitrary"), vmem_limit_bytes=VMEM_LIMIT_BYTES),
        name="in_proj",
    )(x2d, ada, ada, g_norm, w_big, ws_hi, ws_lo)


N_MIXER_CONSTS = 7
N_MIXER_INIT = 5
N_MIXER_OUTS = 5
STATE_RING = 3


def _mixer_scratch(rows):
    return [pltpu.VMEM((rows, QK), f32)] * 5 + [
        pltpu.VMEM((rows, N_SMALL), f32), pltpu.VMEM((rows, N_SMALL), f32), pltpu.VMEM((rows, QK), f32),
    ] + [pltpu.VMEM((rows, D_MODEL), f32)] * 4 + [
        pltpu.VMEM((SUBLANES, 2 * QK), f32),
        pltpu.VMEM((2, N_HEADS, DK, DVA), f32), pltpu.VMEM((2, N_HEADS, DK, DV), f32),
        pltpu.VMEM((2, 1, N_SMALL), f32),
    ]


def _mixer_body(*refs, L, vlo, vhi, sample):
    zb_ref, zs_ref = refs[:2]
    k = 2
    init = None
    if sample:
        init = refs[k:k + N_MIXER_INIT]
        k += N_MIXER_INIT
    consts = refs[k:k + N_MIXER_CONSTS]
    k += N_MIXER_CONSTS
    outs = refs[k:k + N_MIXER_OUTS]
    assert sample, "the prompt stream runs in the fused projection + mixer call"
    zx_s, zsx_s, ux_s, cbuf, sbuf, sem = refs[-6:]
    conv0_ref, c_hbm, n0_ref, m0_ref, s_hbm = init
    bt = cbuf.shape[1]
    step = pl.program_id(0)
    n_steps = pl.num_programs(0)

    def fetch(s):
        slot = lax.rem(s, STATE_RING)
        rows_hbm = pl.ds(s * bt, bt)
        return (pltpu.make_async_copy(c_hbm.at[rows_hbm], cbuf.at[slot], sem.at[0, slot]),
                pltpu.make_async_copy(s_hbm.at[rows_hbm], sbuf.at[slot], sem.at[1, slot]))

    @pl.when(step == 0)
    def _prime():
        for s in range(STATE_RING - 1):
            for cp in fetch(s):
                cp.start()

    @pl.when(step + (STATE_RING - 1) < n_steps)
    def _ahead():
        for cp in fetch(step + (STATE_RING - 1)):
            cp.start()

    for cp in fetch(step):
        cp.wait()
    slot = lax.rem(step, STATE_RING)
    init = (conv0_ref, cbuf.at[slot], n0_ref, m0_ref, sbuf.at[slot])

    n_tok = vhi - vlo
    for c in range(0, N_BIG, PROJ_COLS):
        zx_s[:, c:c + PROJ_COLS] = _spread_tokens(zb_ref[:, c:c + PROJ_COLS], vlo, n_tok)
    zsx_s[...] = _spread_tokens(zs_ref[...], vlo, n_tok)
    _mixer_compute(zx_s, zsx_s, init, consts, (ux_s,) + tuple(outs[1:]), refs[k + N_MIXER_OUTS:-6],
                   L=L, vlo=vlo, vhi=vhi)
    outs[0][...] = _gather_tokens(ux_s[...], vlo, n_tok).astype(outs[0].dtype)


def _fused_body(x_ref, shift_ref, scale_ref, g_ref, wbig_ref, wsh_ref, wsl_ref, *refs, L):
    consts = refs[:N_MIXER_CONSTS]
    tail_ref = refs[N_MIXER_CONSTS]
    outs = refs[N_MIXER_CONSTS + 1:N_MIXER_CONSTS + 1 + N_MIXER_OUTS]
    k = N_MIXER_CONSTS + 1 + N_MIXER_OUTS
    zb_s = list(refs[k:k + N_BIG // PROJ_COLS])
    zs_s = refs[k + N_BIG // PROJ_COLS]
    scratch = refs[k + N_BIG // PROJ_COLS + 1:]
    _zero_carried_state(scratch)

    def step(slots):
        narrow, wide = _inproj_parts(x_ref, shift_ref, scale_ref, g_ref, wbig_ref, wsh_ref, wsl_ref,
                                     zb_s, zs_s, tail_ref, bt=1, tt=L)

        order = (OFF_QKA, OFF_QB, OFF_GA, OFF_OA, OFF_GB, OFF_RB, OFF_VA, OFF_VB)
        pieces = [(off // PROJ_COLS, lo) for off in order for lo in range(0, PROJ_COLS, PIECE_COLS)]
        emitted = []

        def tick(n=1):
            if not emitted:
                narrow()
            for _ in range(n):
                if len(emitted) < len(pieces):
                    idx, lo = pieces[len(emitted)]
                    wide(idx, lo, PIECE_COLS)
                    emitted.append((idx, lo))

        def need(off, width):
            wanted = [(off // PROJ_COLS, lo) for lo in range(0, PROJ_COLS, PIECE_COLS)
                      if lo < off % PROJ_COLS + width and lo + PIECE_COLS > off % PROJ_COLS]
            while any(w not in emitted for w in wanted):
                tick()

        _mixer_compute(zb_s, zs_s, None, consts, outs, scratch, L=L, vlo=0, vhi=L, slots=slots,
                       hooks=(tick, need))

    parity = lax.rem(pl.program_id(1), 2)
    last = pl.program_id(1) == pl.num_programs(1) - 1
    pl.when(parity == 0)(functools.partial(step, (0, 1, last)))
    pl.when(parity == 1)(functools.partial(step, (1, 0, last)))


def _zero_carried_state(scratch):
    carry_s, caug_s, sst_s, mst_s = scratch[-4:]

    @pl.when(pl.program_id(1) == 0)
    def _zero():
        caug_s[0] = jnp.zeros(caug_s.shape[1:], f32)
        sst_s[0] = jnp.zeros(sst_s.shape[1:], f32)
        mst_s[0] = jnp.zeros(mst_s.shape[1:], f32)
        carry_s[...] = jnp.zeros(carry_s.shape, f32)


def _mixer_compute(zb_ref, zs_ref, init, consts, outs, scratch, *, L, vlo, vhi, slots=None,
                   hooks=(lambda n=1: None, lambda off, width: None)):
    sample = init is not None
    if sample:
        conv0_ref, C0_ref, n0_ref, m0_ref, S0_ref = init
    cw_ref, cb_ref, gb_ref, wal_ref, bal_ref, gha_ref, ghb_ref = consts
    u_ref, Co_ref, no_ref, mo_ref, So_ref = outs
    (qa_s, ka_s, qh_s, kh_s, kb_s, x_s, r_s, bg_s, ha_s, hb_s, pa_s, pb_s,
     carry_s, caug_s, sst_s, mst_s) = scratch
    if not sample:
        cur, nxt, last_tile = slots
    rows = zs_ref.shape[0]
    every = slice(None)

    tick, need = hooks

    def zcols(rsel, off, width):
        need(off, width)
        if isinstance(zb_ref, (list, tuple)):
            grp, o = divmod(off, PROJ_COLS)
            assert o + width <= PROJ_COLS
            return zb_ref[grp][rsel, o:o + width]
        return zb_ref[rsel, off:off + width]
    n_groups = rows // L
    log2_l = L.bit_length() - 1
    masked = not (vlo == 0 and vhi == L)
    scale = DK ** -0.5

    row_in = lax.broadcasted_iota(jnp.int32, (rows, 1), 0) & (L - 1)
    valid = (row_in >= vlo) & (row_in < vhi)

    cw = cw_ref[...]
    cb = cb_ref[...]
    for c0 in range(0, 2 * QK, PIECE_COLS):
        cs = slice(c0, c0 + PIECE_COLS)
        zq = zcols(every, OFF_QKA + c0, PIECE_COLS).astype(f32)
        if sample:
            zq = jnp.where(row_in < CONV_W - 1, conv0_ref[:, cs], zq)
            carry = jnp.zeros((SUBLANES, PIECE_COLS), f32)
        else:
            carry = carry_s[:, cs]
            carry_s[:, cs] = zq[rows - SUBLANES:rows, :]
        xe = jnp.concatenate([carry, zq], axis=0)
        conv = cb[:, cs] + zq * cw[CONV_W - 1:CONV_W, cs]
        for k in range(CONV_W - 1):
            o = SUBLANES - (CONV_W - 1) + k
            conv = conv + xe[o:o + rows] * cw[k:k + 1, cs]
        qk = conv * _sigmoid(conv)
        if c0 < QK:
            qa_s[:, cs] = qk * scale
        else:
            ka_s[:, c0 - QK:c0 - QK + PIECE_COLS] = qk
        tick()

    lane = lax.broadcasted_iota(jnp.int32, (1, N_SMALL), 1)
    zs = zs_ref[...]
    pre = zs + gb_ref[...]
    gate = jnp.where(lane < LANE_FG, pre, _log_sigmoid(pre))
    gate_f = jnp.where(valid, gate, 0.0) if masked else gate
    rr = lax.broadcasted_iota(jnp.int32, (rows, rows), 0)
    cc = lax.broadcasted_iota(jnp.int32, (rows, rows), 1)
    tril_mask = ((rr >> log2_l) == (cc >> log2_l)) & (cc <= rr)
    tril = jnp.where(tril_mask, 1.0, 0.0).astype(bf16)

    la_lin = _mm(zs.astype(bf16), wal_ref[...]) + bal_ref[...]
    tick(2)

    la_parts, kb_parts = [], []
    for h in range(N_HEADS):
        ks = slice(h * DK, (h + 1) * DK)
        la_h = _log_sigmoid(la_lin[:, ks]) * (1.0 / GLA_TAU)
        la_parts.append(jnp.where(valid, la_h, 0.0) if masked else la_h)
        tick()

    both = jnp.concatenate([gate_f] + la_parts, axis=1)
    wb = N_SMALL + QK
    t = _mm(tril, jnp.concatenate(_split3(both), axis=1))
    csum = t[:, :wb] + (t[:, wb:2 * wb] + t[:, 2 * wb:])
    tick(2)
    x_all = jnp.where(lane < LANE_FG, gate, csum[:, :N_SMALL])
    x_s[...] = x_all
    b_al = pltpu.roll(x_all, N_SMALL - N_HEADS, axis=1)
    r_all = x_all - b_al
    r_ok = (lane < N_HEADS) & valid if masked else jnp.broadcast_to(lane < N_HEADS, (rows, N_SMALL))
    r_s[...] = jnp.where(r_ok, r_all, NEG)
    range_m = jnp.max(jnp.where(r_ok, r_all, NEG)) - jnp.min(jnp.where(r_ok, r_all, -NEG))
    bg = csum[:, N_SMALL:]
    bg_s[...] = bg
    range_g = jnp.max(-bg)
    tick()
    for h in range(N_HEADS):
        ks = slice(h * DK, (h + 1) * DK)
        kb_h = zcols(every, OFF_KB + h * DK, DK).astype(f32)
        kb_parts.append(jnp.where(valid, kb_h, 0.0) if masked else kb_h)
        qh_s[:, ks] = zcols(every, OFF_QB + h * DK, DK).astype(f32) * scale * jnp.exp(bg[:, ks])
        kh_s[:, ks] = kb_parts[h] * jnp.exp(-bg[:, ks])
        tick()
    kb = jnp.concatenate(kb_parts, axis=1)
    fast_ok = jnp.logical_and(range_m <= SAFE_LOG_RANGE, range_g <= SAFE_LOG_RANGE)

    for c0 in range(0, D_MODEL, PIECE_COLS):
        cs = slice(c0, c0 + PIECE_COLS)
        col = lambda off: zcols(every, off + c0, PIECE_COLS).astype(f32)
        pa_s[:, cs] = _sigmoid(col(OFF_GA)) * _sigmoid(col(OFF_OA)) * gha_ref[:, cs]
        tick(2)
    for c0 in range(0, D_MODEL, PIECE_COLS):
        cs = slice(c0, c0 + PIECE_COLS)
        col = lambda off: zcols(every, off + c0, PIECE_COLS).astype(f32)
        rb = col(OFF_RB)
        pb_s[:, cs] = _sigmoid(col(OFF_GB)) * (rb * _sigmoid(rb)) * ghb_ref[:, cs]
        tick(2)
    need(OFF_VA, D_MODEL)
    need(OFF_VB, D_MODEL)

    tri = lax.broadcasted_iota(jnp.int32, (L, L), 0) >= lax.broadcasted_iota(jnp.int32, (L, L), 1)
    eye = lax.broadcasted_iota(jnp.int32, (DK, DK), 0) == lax.broadcasted_iota(jnp.int32, (DK, DK), 1)
    lane_h = lane
    lrow = lax.broadcasted_iota(jnp.int32, (1, L), 1)
    lcol = lax.broadcasted_iota(jnp.int32, (L, 1), 0)
    vrow = (lrow >= vlo) & (lrow < vhi)
    ones16 = jnp.ones((L, DK), bf16)

    def load_state(g, h):
        if sample:
            n_mat = jnp.broadcast_to(_row_to_col(n0_ref[g, pl.ds(h, 1), :], eye), (DK, DK))
            return jnp.concatenate([C0_ref[g, h], n_mat], axis=1), S0_ref[g, h]
        return caug_s[cur, h], sst_s[cur, h]

    def emit_state(g, h, caug, s_new):
        Co_ref[g, h] = caug[:, :DV]
        no_ref[g, pl.ds(h, 1), :] = _diag_row(caug[:, DV:], eye)
        So_ref[g, h] = s_new

    def store_state(g, h, caug, s_new):
        if sample:
            emit_state(g, h, caug, s_new)
        else:
            caug_s[nxt, h] = caug
            sst_s[nxt, h] = s_new

    def load_m(g):
        return m0_ref[g] if sample else mst_s[cur]

    def store_m(g, m_vec):
        if sample:
            mo_ref[g] = m_vec
        else:
            mst_s[nxt] = m_vec

    def group_fast(g):
        r0 = g * L
        rs = pl.ds(r0, L)
        xc = x_s[rs, :]
        rc = r_s[rs, :]
        m_prev = load_m(g)
        bg_tail = bg_s[pl.ds(r0 + L - SUBLANES, SUBLANES), :]
        heads = []
        for h in range(N_HEADS):
            ks = slice(h * DK, (h + 1) * DK)
            caug, s_st = load_state(g, h)
            c0 = jnp.max(rc[:, h:h + 1], axis=0, keepdims=True)
            q16 = qa_s[rs, ks].astype(bf16)
            kt16 = (ka_s[rs, ks] * jnp.exp(rc[:, h:h + 1] - c0)).astype(bf16)
            qh16 = qh_s[rs, ks].astype(bf16)
            kh = kh_s[rs, ks]
            heads.append(dict(
                caug=caug, s_st=s_st, c0=c0, kt16=kt16, kh=kh,
                va=jnp.concatenate([zcols(rs, OFF_VA + h * DV, DV).astype(bf16), ones16], axis=1),
                vb16=zcols(rs, OFF_VB + h * DV, DV).astype(bf16),
                s_raw=_mm_nt(q16, kt16), a_raw=_mm_nt(qh16, kh.astype(bf16)),
                ci=_mm(q16, caug.astype(bf16)), oi=_mm(qh16, s_st.astype(bf16))))
        for p in heads:
            p["ni"] = _mm(jnp.where(tri, p["s_raw"], 0.0).astype(bf16), p["va"])
            p["o"] = _mm(jnp.where(tri, p["a_raw"], 0.0).astype(bf16), p["vb16"]) + p["oi"]
        for h, p in enumerate(heads):
            p["eb_last"] = jnp.exp(bg_tail[SUBLANES - 1:SUBLANES, h * DK:(h + 1) * DK])
            p["kv"] = _mm_tn(p["kt16"], p["va"])
            p["skv"] = _mm_tn((p["kh"] * p["eb_last"]).astype(bf16), p["vb16"])
        m_vec = m_prev
        for h, p in enumerate(heads):
            m_st = m_prev[:, h:h + 1]
            b_col = xc[:, LANE_FG + h:LANE_FG + h + 1]
            b_last = b_col[L - 1:L, :]
            c0 = p["c0"]
            mm = jnp.maximum(c0, m_st)
            comb = jnp.exp(c0 - mm) * p["ni"] + jnp.exp(m_st - mm) * p["ci"]
            inv = 1.0 / jnp.maximum(jnp.abs(comb[:, DV:]), jnp.exp(-(b_col + mm)))
            ha_s[rs, h * DV:h * DV + DK] = comb[:, :DK] * inv
            ha_s[rs, h * DV + DK:(h + 1) * DV] = comb[:, DK:DV] * inv
            hb_s[rs, h * DV:(h + 1) * DV] = p["o"]
            m_new = jnp.maximum(b_last + m_st, b_last + c0)
            caug_new = jnp.exp(b_last + m_st - m_new) * p["caug"] + jnp.exp(b_last + c0 - m_new) * p["kv"]
            s_new = _row_to_col(p["eb_last"], eye) * p["s_st"] + p["skv"]
            store_state(g, h, caug_new, s_new)
            m_vec = jnp.where(lane_h == h, m_new, m_vec)
        store_m(g, m_vec)

    def group_exact(g):
        static = isinstance(g, int)
        r0 = g * L if static else pl.multiple_of(g * L, L)
        tail0 = r0 + L - SUBLANES if static else pl.multiple_of(r0 + L - SUBLANES, SUBLANES)
        rs = pl.ds(r0, L)
        xc = x_s[rs, :]
        rc = r_s[rs, :]
        sr = lax.broadcasted_iota(jnp.int32, (2 * SUBLANES, N_SMALL), 0)
        sc = lax.broadcasted_iota(jnp.int32, (2 * SUBLANES, N_SMALL), 1)
        sel = jnp.where((sr < N_HEADS) & (sc == sr + LANE_IG), 1.0,
                        jnp.where((sr < N_HEADS) & (sc == sr + LANE_FG), -1.0, 0.0)).astype(bf16)
        xh, xm, xl = _split3(xc)
        r_rows = _mm_nt(sel, xh) + (_mm_nt(sel, xm) + _mm_nt(sel, xl))
        m_prev = load_m(g)
        m_vec = m_prev
        bg_tail = bg_s[pl.ds(tail0, SUBLANES), :]
        for h in range(N_HEADS):
            ks = slice(h * DK, (h + 1) * DK)
            vs = slice(h * DV, (h + 1) * DV)
            caug, s_st = load_state(g, h)
            m_st = m_prev[:, h:h + 1]
            b_col = xc[:, LANE_FG + h:LANE_FG + h + 1]
            r_col = rc[:, h:h + 1]
            b_last = b_col[L - 1:L, :]
            v16 = zcols(rs, OFF_VA + h * DV, DV).astype(bf16)
            vb16 = zcols(rs, OFF_VB + h * DV, DV).astype(bf16)
            eb_last = jnp.exp(bg_tail[SUBLANES - 1:SUBLANES, ks])
            eb_col = _row_to_col(eb_last, eye)
            c_st = caug[:, :DV]
            n_row = _diag_row(caug[:, DV:], eye)
            r_row = r_rows[h:h + 1, :]
            if masked:
                r_row = jnp.where(vrow, r_row, NEG)
            dm = jnp.where(tri, b_col + r_row, NEG)
            m_t = jnp.maximum(jnp.max(dm, axis=1, keepdims=True), b_col + m_st)
            w = jnp.exp(dm - m_t)
            wi = jnp.exp(b_col + m_st - m_t)
            q = qa_s[rs, ks]
            k = ka_s[rs, ks]
            q16 = q.astype(bf16)
            s = _mm_nt(q16, k.astype(bf16)) * w
            num = _mm(s.astype(bf16), v16) + wi * _mm(q16, c_st.astype(bf16))
            den = jnp.sum(s, axis=1, keepdims=True) + wi * jnp.sum(q * n_row, axis=1, keepdims=True)
            ha_s[rs, vs] = num / jnp.maximum(jnp.abs(den), jnp.exp(-m_t))
            m_new = jnp.maximum(b_last + m_st, b_last + jnp.max(r_row, axis=1, keepdims=True))
            wc = jnp.exp(b_last + m_st - m_new)
            kt = k * jnp.exp(b_last + r_col - m_new)
            n_new = wc * n_row + jnp.sum(kt, axis=0, keepdims=True)
            caug_new = jnp.concatenate(
                [wc * c_st + _mm_tn(kt.astype(bf16), v16),
                 jnp.broadcast_to(_row_to_col(n_new, eye), (DK, DK))], axis=1)
            qb = zcols(rs, OFF_QB + h * DK, DK).astype(f32) * scale
            bgc = bg_s[rs, ks]
            sub = lax.broadcasted_iota(jnp.int32, (SUBLANES, 1), 0)

            def score_col(s_i, a, ks=ks, qb=qb, bgc=bgc, sub=sub):
                s8 = lax.shift_left(lax.shift_right_logical(s_i, 3), 3)
                base = pl.multiple_of(r0 + s8, SUBLANES)
                pick = sub == (s_i - s8)
                b_s = jnp.sum(jnp.where(pick, bg_s[pl.ds(base, SUBLANES), ks], 0.0), axis=0, keepdims=True)
                k_s = jnp.sum(jnp.where(pick, kb_s[pl.ds(base, SUBLANES), ks], 0.0), axis=0, keepdims=True)
                e = jnp.exp(jnp.where(lcol >= s_i, bgc - b_s, NEG))
                col = jnp.sum(qb * e * k_s, axis=1, keepdims=True)
                return jnp.where(lrow == s_i, col, a)

            a = lax.fori_loop(0, L, score_col, jnp.zeros((L, L), f32))
            hb_s[rs, vs] = (_mm(a.astype(bf16), vb16)
                            + _mm((qb * jnp.exp(bgc)).astype(bf16), s_st.astype(bf16)))
            kt_b = kb_s[rs, ks] * jnp.exp(bg_tail[SUBLANES - 1:SUBLANES, ks] - bgc)
            s_new = eb_col * s_st + _mm_tn(kt_b.astype(bf16), vb16)
            store_state(g, h, caug_new, s_new)
            m_vec = jnp.where(lane_h == h, m_new, m_vec)
        store_m(g, m_vec)

    def groups_fast_batched():
        r_all_m = r_s[...]
        c0_l, m_l, mm_l, fac, wc = [], [], [], [], []
        for g in range(n_groups):
            gs = slice(g * L, (g + 1) * L)
            c0_g = jnp.max(r_all_m[gs], axis=0, keepdims=True)
            m_g = m0_ref[g]
            b_last = b_al[gs][L - 1:L, :]
            m_new = jnp.maximum(b_last + m_g, b_last + c0_g)
            mo_ref[g] = m_new
            c0_l.append(c0_g)
            m_l.append(m_g)
            mm_l.append(jnp.maximum(c0_g, m_g))
            fac.append(jnp.exp(b_last + c0_g - m_new))
            wc.append(jnp.exp(b_last + m_g - m_new))
        rows_of = lambda lst: jnp.concatenate([jnp.broadcast_to(v, (L, N_SMALL)) for v in lst], axis=0)
        c0_r, m_r, mm_r = rows_of(c0_l), rows_of(m_l), rows_of(mm_l)
        beta = jnp.exp(r_all_m - c0_r)
        gi_r = jnp.exp(c0_r - mm_r)
        wi_r = jnp.exp(m_r - mm_r)
        floor = jnp.exp(-(b_al + mm_r))
        ones_g = jnp.ones((L, DK), bf16)
        for h in range(N_HEADS):
            ks = slice(h * DK, (h + 1) * DK)
            vs = slice(h * DV, (h + 1) * DV)
            kt = ka_s[:, ks] * beta[:, h:h + 1]
            v = zcols(every, OFF_VA + h * DV, DV)
            vb = zcols(every, OFF_VB + h * DV, DV)
            q16 = qa_s[:, ks].astype(bf16)
            va = jnp.concatenate([v.astype(bf16), jnp.ones((rows, DK), bf16)], axis=1)
            s16 = jnp.where(tril_mask, _mm_nt(q16, kt.astype(bf16)), 0.0).astype(bf16)
            ni = _mm(s16, va)
            qh16 = qh_s[:, ks].astype(bf16)
            kh = kh_s[:, ks]
            a16 = jnp.where(tril_mask, _mm_nt(qh16, kh.astype(bf16)), 0.0).astype(bf16)
            o_intra = _mm(a16, vb.astype(bf16))
            ci_parts, oi_parts = [], []
            for g in range(n_groups):
                gs = slice(g * L, (g + 1) * L)
                caug, s_st = load_state(g, h)
                va_g = jnp.concatenate([v[gs].astype(bf16), ones_g], axis=1)
                ci_parts.append(_mm(qa_s[gs, ks].astype(bf16), caug.astype(bf16)))
                caug_new = wc[g][:, h:h + 1] * caug + fac[g][:, h:h + 1] * _mm_tn(kt[gs].astype(bf16), va_g)
                oi_parts.append(_mm(qh_s[gs, ks].astype(bf16), s_st.astype(bf16)))
                eb_last = jnp.exp(bg_s[gs, ks][L - 1:L, :])
                s_new = (_row_to_col(eb_last, eye) * s_st
                         + _mm_tn((kh[gs] * eb_last).astype(bf16), vb[gs].astype(bf16)))
                store_state(g, h, caug_new, s_new)
            comb = gi_r[:, h:h + 1] * ni + wi_r[:, h:h + 1] * jnp.concatenate(ci_parts, axis=0)
            inv = 1.0 / jnp.maximum(jnp.abs(comb[:, DV:]), floor[:, h:h + 1])
            ha_s[:, h * DV:h * DV + DK] = comb[:, :DK] * inv
            ha_s[:, h * DV + DK:(h + 1) * DV] = comb[:, DK:DV] * inv
            hb_s[:, vs] = o_intra + jnp.concatenate(oi_parts, axis=0)

    def run_groups(exact):
        if not exact:
            if sample and L == SUBLANES:
                groups_fast_batched()
            else:
                for g in range(n_groups):
                    group_fast(g)
        elif n_groups == 1:
            group_exact(0)
        else:
            def body(g, carry_):
                group_exact(g)
                return carry_
            lax.fori_loop(0, n_groups, body, 0)

    def merge_heads():
        for h in range(N_HEADS):
            vs = slice(h * DV, (h + 1) * DV)
            u = pa_s[:, vs] * _rms(ha_s[:, vs]) + pb_s[:, vs] * _rms(hb_s[:, vs])
            u_ref[:, vs] = u.astype(u_ref.dtype)

    run_groups(False)
    merge_heads()

    @pl.when(jnp.logical_not(fast_ok))
    def _exact():
        kb_s[...] = kb
        run_groups(True)
        merge_heads()

    if not sample:
        @pl.when(last_tile)
        def _emit_final_states():
            for h in range(N_HEADS):
                emit_state(0, h, caug_s[nxt, h], sst_s[nxt, h])
            mo_ref[0] = mst_s[nxt]


def _mixer_call(zb, zs, init, consts, *, nb, nt, bt, L, vlo, vhi):
    sample = init is not None
    rows = bt * L
    tok_rows = bt * (vhi - vlo) if sample else rows
    n_state = nb * bt
    row_spec = lambda n, r=rows: pl.BlockSpec((r, n), lambda i, j: (i * nt + j, 0))
    st4 = pl.BlockSpec((bt, N_HEADS, DK, DV), lambda i, j: (i, 0, 0, 0))
    st_n = pl.BlockSpec((bt, N_HEADS, DK), lambda i, j: (i, 0, 0))
    st_m = pl.BlockSpec((bt, 1, N_SMALL), lambda i, j: (i, 0, 0))
    in_specs = [row_spec(N_BIG, tok_rows), row_spec(N_SMALL, tok_rows)]
    args = [zb, zs]
    scratch = _mixer_scratch(rows)
    if sample:
        assert nt == 1 and nb >= STATE_RING
        st4_in = pl.BlockSpec(memory_space=pl.ANY)
        in_specs += [row_spec(2 * QK), st4_in, st_n, st_m, st4_in]
        args += list(init)
        scratch += [pltpu.VMEM((rows, N_BIG), f32), pltpu.VMEM((rows, N_SMALL), f32),
                    pltpu.VMEM((rows, D_MODEL), f32),
                    pltpu.VMEM((STATE_RING, bt, N_HEADS, DK, DV), f32),
                    pltpu.VMEM((STATE_RING, bt, N_HEADS, DK, DV), f32),
                    pltpu.SemaphoreType.DMA((2, STATE_RING))]
    in_specs += [_const_spec(c.shape) for c in consts]
    args += list(consts)
    return pl.pallas_call(
        functools.partial(_mixer_body, L=L, vlo=vlo, vhi=vhi, sample=sample),
        out_shape=(
            jax.ShapeDtypeStruct((zb.shape[0], D_MODEL), bf16),
            jax.ShapeDtypeStruct((n_state, N_HEADS, DK, DV), f32),
            jax.ShapeDtypeStruct((n_state, N_HEADS, DK), f32),
            jax.ShapeDtypeStruct((n_state, 1, N_SMALL), f32),
            jax.ShapeDtypeStruct((n_state, N_HEADS, DK, DV), f32),
        ),
        grid=(nb, nt),
        in_specs=in_specs,
        out_specs=(row_spec(D_MODEL, tok_rows), st4, st_n, st_m, st4),
        scratch_shapes=scratch,
        compiler_params=pltpu.CompilerParams(
            dimension_semantics=("arbitrary", "arbitrary"), vmem_limit_bytes=VMEM_LIMIT_BYTES),
        name="mixer_sample" if sample else "mixer_prompt",
    )(*args)


def _fused_call(x2d, ada4, g_norm, w_big, ws_hi, ws_lo, consts, *, nb, nt, L):
    ada, row0 = ada4
    row_spec = lambda n: pl.BlockSpec((L, n), lambda i, j: (i * nt + j, 0))
    st4 = pl.BlockSpec((1, N_HEADS, DK, DV), lambda i, j: (i, 0, 0, 0))
    st_n = pl.BlockSpec((1, N_HEADS, DK), lambda i, j: (i, 0, 0))
    st_m = pl.BlockSpec((1, 1, N_SMALL), lambda i, j: (i, 0, 0))
    return pl.pallas_call(
        functools.partial(_fused_body, L=L),
        out_shape=(
            jax.ShapeDtypeStruct((nb * nt, SUBLANES, PROJ_COLS), f32),
            jax.ShapeDtypeStruct((x2d.shape[0], D_MODEL), bf16),
            jax.ShapeDtypeStruct((nb, N_HEADS, DK, DV), f32),
            jax.ShapeDtypeStruct((nb, N_HEADS, DK), f32),
            jax.ShapeDtypeStruct((nb, 1, N_SMALL), f32),
            jax.ShapeDtypeStruct((nb, N_HEADS, DK, DV), f32),
        ),
        grid=(nb, nt),
        in_specs=[
            row_spec(D_MODEL), _mod_spec(1, 0, row0), _mod_spec(1, 1, row0),
            _const_spec((1, D_MODEL)), _const_spec((D_MODEL, N_BIG)),
            _const_spec((D_MODEL, N_SMALL)), _const_spec((D_MODEL, N_SMALL)),
        ] + [_const_spec(c.shape) for c in consts],
        out_specs=(
            pl.BlockSpec((1, SUBLANES, PROJ_COLS), lambda i, j: (i * nt + j, 0, 0)),
            row_spec(D_MODEL), st4, st_n, st_m, st4,
        ),
        scratch_shapes=([pltpu.VMEM((L, PROJ_COLS), f32)] * (N_BIG // PROJ_COLS)
                        + [pltpu.VMEM((L, N_SMALL), f32)] + _mixer_scratch(L)),
        compiler_params=pltpu.CompilerParams(
            dimension_semantics=("arbitrary", "arbitrary"), vmem_limit_bytes=VMEM_LIMIT_BYTES),
        name="proj_mixer_prompt",
    )(x2d, ada, ada, g_norm, w_big, ws_hi, ws_lo, *consts)


def _out_body(u_ref, x_ref, gate1_ref, shift2_ref, scale2_ref, gate2_ref, gn2_ref, gfin_ref,
              wout_ref, wg_ref, wu_ref, wd_ref, y_ref, *, bt, tt):
    x1 = x_ref[...] + _mod_rows(gate1_ref, bt, tt) * _mm(u_ref[...], wout_ref[...])
    h2 = _rms(x1) * gn2_ref[...]
    h2 = (h2 * (1.0 + _mod_rows(scale2_ref, bt, tt)) + _mod_rows(shift2_ref, bt, tt)).astype(bf16)
    acc = jnp.zeros(x1.shape, f32)
    for c in range(0, D_FF, FFN_COLS):
        g = _mm(h2, wg_ref[:, c:c + FFN_COLS])
        up = _mm(h2, wu_ref[:, c:c + FFN_COLS])
        acc = acc + _mm((g * _sigmoid(g) * up).astype(bf16), wd_ref[c:c + FFN_COLS, :])
    x2 = x1 + _mod_rows(gate2_ref, bt, tt) * acc
    y_ref[...] = _rms(x2) * gfin_ref[...]


N_OUT_MODS = 4
STAGE_WIDE_ROWS = 128
STAGE_TALL_ROWS = 256


def _load_cast(w_hbm, w16_ref, stage, sem):
    n_rows, _ = w_hbm.shape
    chunk = stage.shape[1]
    assert n_rows % chunk == 0
    n = n_rows // chunk

    def copy(k):
        slot = k % 2
        return pltpu.make_async_copy(w_hbm.at[pl.ds(k * chunk, chunk), :], stage.at[slot], sem.at[slot])

    copy(0).start()
    for k in range(n):
        if k + 1 < n:
            copy(k + 1).start()
        copy(k).wait()
        w16_ref[k * chunk:(k + 1) * chunk, :] = stage[k % 2].astype(bf16)


def _out_both_body(*refs, n_prompt, tt_p, bt_s, tt_s):
    up_ref, xp_ref = refs[:2]
    mods_p = refs[2:2 + N_OUT_MODS]
    us_ref, xs_ref = refs[2 + N_OUT_MODS:4 + N_OUT_MODS]
    mods_s = refs[4 + N_OUT_MODS:4 + 2 * N_OUT_MODS]
    gn2_ref, gfin_ref = refs[4 + 2 * N_OUT_MODS:6 + 2 * N_OUT_MODS]
    w_hbm = refs[6 + 2 * N_OUT_MODS:10 + 2 * N_OUT_MODS]
    yp_ref, ys_ref = refs[10 + 2 * N_OUT_MODS:12 + 2 * N_OUT_MODS]
    w16 = refs[12 + 2 * N_OUT_MODS:16 + 2 * N_OUT_MODS]
    stage_wide, stage_tall, sem = refs[16 + 2 * N_OUT_MODS:]
    shared = (gn2_ref, gfin_ref) + tuple(w16)
    step = pl.program_id(0)

    @pl.when(step == 0)
    def _load_weights():
        for w_ref, w16_ref in zip(w_hbm, w16):
            stage = stage_wide if w_ref.shape[1] == D_FF else stage_tall
            _load_cast(w_ref, w16_ref, stage, sem)

    @pl.when(step < n_prompt)
    def _prompt():
        _out_body(up_ref, xp_ref, *mods_p, *shared, yp_ref, bt=1, tt=tt_p)

    @pl.when(step == n_prompt)
    def _sample():
        _out_body(us_ref, xs_ref, *mods_s, *shared, ys_ref, bt=bt_s, tt=tt_s)


def _out_call(u_p, x_p, u_s, x_s, ada, g_norm2, g_final, w_out, w_gate, w_up, w_down, *,
              row0_p, tiles_per_seq, tt_p, bt_s, tt_s):
    n_prompt = x_p.shape[0] // tt_p
    rows_s = bt_s * tt_s
    assert x_s.shape[0] == rows_s
    tile_p = lambda s: jnp.minimum(s, n_prompt - 1)
    prompt_rows = pl.BlockSpec((tt_p, D_MODEL), lambda s: (tile_p(s), 0))
    sample_rows = lambda: pl.BlockSpec((rows_s, D_MODEL), lambda s: (0, 0), pipeline_mode=pl.Buffered(1))
    mod_p = lambda k: pl.BlockSpec((1, 1, 1, D_MODEL), lambda s: (row0_p + tile_p(s) // tiles_per_seq, k, 0, 0))
    mod_s = lambda k: pl.BlockSpec((bt_s, 1, 1, D_MODEL), lambda s: (0, k, 0, 0), pipeline_mode=pl.Buffered(1))
    const = lambda shape: pl.BlockSpec(shape, lambda s: (0, 0), pipeline_mode=pl.Buffered(1))
    ks = range(N_ADA - N_OUT_MODS, N_ADA)
    return pl.pallas_call(
        functools.partial(_out_both_body, n_prompt=n_prompt, tt_p=tt_p, bt_s=bt_s, tt_s=tt_s),
        out_shape=(jax.ShapeDtypeStruct(x_p.shape, f32), jax.ShapeDtypeStruct(x_s.shape, f32)),
        grid=(n_prompt + 1,),
        in_specs=(
            [prompt_rows, prompt_rows] + [mod_p(k) for k in ks]
            + [sample_rows(), sample_rows()] + [mod_s(k) for k in ks]
            + [const((1, D_MODEL)), const((1, D_MODEL))] + [pl.BlockSpec(memory_space=pl.ANY)] * 4),
        out_specs=(prompt_rows, pl.BlockSpec((rows_s, D_MODEL), lambda s: (0, 0))),
        scratch_shapes=[
            pltpu.VMEM((D_MODEL, D_MODEL), bf16), pltpu.VMEM((D_MODEL, D_FF), bf16),
            pltpu.VMEM((D_MODEL, D_FF), bf16), pltpu.VMEM((D_FF, D_MODEL), bf16),
            pltpu.VMEM((2, STAGE_WIDE_ROWS, D_FF), f32), pltpu.VMEM((2, STAGE_TALL_ROWS, D_MODEL), f32),
            pltpu.SemaphoreType.DMA((2,)),
        ],
        compiler_params=pltpu.CompilerParams(
            dimension_semantics=("arbitrary",), vmem_limit_bytes=VMEM_LIMIT_BYTES),
        name="out_ffn",
    )(u_p, x_p, *([ada] * N_OUT_MODS), u_s, x_s, *([ada] * N_OUT_MODS),
      g_norm2, g_final, w_out, w_gate, w_up, w_down)


PROMPT_CHUNK = 256
PROMPT_FFN_ROWS = 512
SAMPLE_L = SUBLANES
SAMPLE_MIX_GROUPS = 8
SAMPLE_DENSE_GROUPS = 64


W_IN_SEG_A = (0, 3072)
W_IN_GATES = 3072
W_IN_SEG_B = (3080, 6152)
W_IN_LR = 6152
W_IN_SEG_C = (6168, 8216)
LANES = 128


PREP_COLS = 512
SHIFT_B = W_IN_SEG_B[0] - OFF_QB
SHIFT_C = W_IN_SEG_C[0] - OFF_GA
PREP_HALO = 32


def _prep_body(a_ref, b_ref, gt_ref, lt_ref, wb_ref, wh_ref, wl_ref):
    j = pl.program_id(0)
    both = jnp.concatenate([a_ref[...], b_ref[...]], axis=0)

    def emit(shift):
        wb_ref[...] = both[shift:shift + PREP_COLS, :].T.astype(bf16)

    pl.when(j < OFF_QB // PREP_COLS)(lambda: emit(0))
    pl.when((j >= OFF_QB // PREP_COLS) & (j < OFF_GA // PREP_COLS))(lambda: emit(SHIFT_B))
    pl.when(j >= OFF_GA // PREP_COLS)(lambda: emit(SHIFT_C))

    @pl.when(j == 0)
    def _narrow():
        row = lax.broadcasted_iota(jnp.int32, (LANES, 1), 0)
        small = jnp.where(row < LANE_LR, gt_ref[...], jnp.where(row < LANE_LR + GLA_RANK, lt_ref[...], 0.0)).T
        hi = small.astype(bf16)
        wh_ref[...] = hi
        wl_ref[...] = (small - hi.astype(f32)).astype(bf16)


def _prep_call(w_in):
    assert W_IN_GATES % LANES == 0 and (W_IN_LR - LANE_LR) % LANES == 0
    assert max(SHIFT_B, SHIFT_C) <= PREP_HALO and SHIFT_B % SUBLANES == 0 and SHIFT_C % SUBLANES == 0
    wt = jnp.transpose(w_in)
    halo_per_block = PREP_COLS // PREP_HALO
    last = (wt.shape[0] - 1) // PREP_HALO
    const = lambda shape: pl.BlockSpec(shape, lambda j: (0, 0))
    return pl.pallas_call(
        _prep_body,
        out_shape=(
            jax.ShapeDtypeStruct((D_MODEL, N_BIG), bf16),
            jax.ShapeDtypeStruct((D_MODEL, N_SMALL), bf16),
            jax.ShapeDtypeStruct((D_MODEL, N_SMALL), bf16),
        ),
        grid=(N_BIG // PREP_COLS,),
        in_specs=[
            pl.BlockSpec((PREP_COLS, D_MODEL), lambda j: (j, 0)),
            pl.BlockSpec((PREP_HALO, D_MODEL), lambda j: (jnp.minimum((j + 1) * halo_per_block, last), 0)),
            pl.BlockSpec((LANES, D_MODEL), lambda j: (W_IN_GATES // LANES, 0)),
            pl.BlockSpec((LANES, D_MODEL), lambda j: ((W_IN_LR - LANE_LR) // LANES, 0)),
        ],
        out_specs=(
            pl.BlockSpec((D_MODEL, PREP_COLS), lambda j: (0, j)),
            const((D_MODEL, N_SMALL)), const((D_MODEL, N_SMALL)),
        ),
        compiler_params=pltpu.CompilerParams(
            dimension_semantics=("arbitrary",), vmem_limit_bytes=VMEM_LIMIT_BYTES),
        name="w_in_prep",
    )(wt, wt, wt, wt)


def _prep_layer_weights(w_in, w_alpha, b_igate, b_fgate):
    w_big, ws_hi, ws_lo = _prep_call(w_in)
    w_al = jnp.zeros((N_SMALL, QK), f32).at[LANE_LR:LANE_LR + GLA_RANK].set(w_alpha).astype(bf16)
    g_bias = jnp.concatenate([b_igate, b_fgate, jnp.zeros((N_SMALL - 2 * N_HEADS,), f32)])[None, :]
    return w_big, ws_hi, ws_lo, w_al, g_bias


def _layer(xp2d, xs2d, st_s, ada_p, ada_s, p, *, bp, tp, bs):
    (g_norm1, g_norm2, w_in, conv_w, conv_b, b_igate, b_fgate, g_head_a,
     w_alpha, b_alpha, g_head_b, w_out, w_ffn_gate, w_ffn_up, w_ffn_down, g_final_or_ones) = p
    w_big, ws_hi, ws_lo, w_al, g_bias = _prep_layer_weights(w_in, w_alpha, b_igate, b_fgate)
    consts = (conv_w, conv_b[None, :], g_bias, w_al, b_alpha[None, :],
              g_head_a.reshape(1, D_MODEL), g_head_b.reshape(1, D_MODEL))
    gn1, gn2, gfin = g_norm1[None, :], g_norm2[None, :], g_final_or_ones[None, :]

    ntp = tp // PROMPT_CHUNK
    tail, u, c_p, n_p, m_p, s_p = _fused_call(xp2d, ada_p, gn1, w_big, ws_hi, ws_lo, consts,
                                              nb=bp, nt=ntp, L=PROMPT_CHUNK)
    conv_p = tail.reshape(bp, ntp, SUBLANES, 2 * QK)[:, ntp - 1, SUBLANES - (CONV_W - 1):, :]

    conv0, c0, n0, m0, s0 = st_s
    lo = CONV_W - 1
    n_tok = SAMPLE_L - lo - 1
    nbd = bs // SAMPLE_DENSE_GROUPS
    zb_s, zs_s, _ = _inproj_call(xs2d, ada_s, gn1, w_big, ws_hi, ws_lo,
                                 nb=nbd, nt=1, bt=SAMPLE_DENSE_GROUPS, tt=n_tok, zdtype=f32)
    conv_s = zb_s[:, OFF_QKA:OFF_QKA + 2 * QK].reshape(bs, n_tok, 2 * QK)[:, n_tok - lo:, :]
    conv0_rows = jnp.pad(conv0, ((0, 0), (0, SAMPLE_L - lo), (0, 0))).reshape(bs * SAMPLE_L, 2 * QK)
    m0_lanes = jnp.pad(m0[:, None, :], ((0, 0), (0, 0), (0, N_SMALL - N_HEADS)))
    u_s, c_s, n_s, m_s, s_s = _mixer_call(
        zb_s, zs_s, (conv0_rows, c0, n0, m0_lanes, s0), consts,
        nb=bs // SAMPLE_MIX_GROUPS, nt=1, bt=SAMPLE_MIX_GROUPS, L=SAMPLE_L, vlo=lo, vhi=lo + n_tok)
    assert ada_s[1] == 0 and ada_p[0] is ada_s[0]
    yp, ys = _out_call(u, xp2d, u_s, xs2d, ada_p[0], gn2, gfin, w_out, w_ffn_gate, w_ffn_up, w_ffn_down,
                       row0_p=ada_p[1], tiles_per_seq=tp // PROMPT_FFN_ROWS, tt_p=PROMPT_FFN_ROWS,
                       bt_s=bs, tt_s=n_tok)
    st_p = (conv_p, c_p, n_p, m_p[:, 0, :N_HEADS], s_p)
    st_s_new = (conv_s, c_s, n_s, m_s[:, 0, :N_HEADS], s_s)
    return yp, ys, st_p, st_s_new


def kernel(x_prompt, x_sample, state_conv, state_C, state_n, state_m, state_S, c_prompt, c_sample,
           g_norm1, g_norm2, w_ada, b_ada, w_in, conv_w, conv_b, b_igate, b_fgate, g_head_a,
           w_alpha, b_alpha, g_head_b, w_out, w_ffn_gate, w_ffn_up, w_ffn_down, g_final):
    bp, tp, _ = x_prompt.shape
    bs, ts, _ = x_sample.shape
    depth = w_in.shape[0]
    assert depth == 1, "the final norm is fused into the (single) layer's FFN kernel"
    lo = CONV_W - 1
    assert ts == SAMPLE_L - lo - 1
    xp2d = x_prompt.reshape(bp * tp, D_MODEL)
    xs2d = x_sample.reshape(bs * ts, D_MODEL)
    c_all = jnp.concatenate([c_sample, c_prompt], axis=0)
    new_p = [[] for _ in range(5)]
    new_s = [[] for _ in range(5)]
    for l in range(depth):
        ada = _ada_call(c_all, w_ada[l], b_ada[l][None, :])
        p = (g_norm1[l], g_norm2[l], w_in[l], conv_w[l], conv_b[l], b_igate[l], b_fgate[l], g_head_a[l],
             w_alpha[l], b_alpha[l], g_head_b[l], w_out[l], w_ffn_gate[l], w_ffn_up[l], w_ffn_down[l], g_final)
        st_s = (state_conv[l], state_C[l], state_n[l], state_m[l], state_S[l])
        xp2d, xs2d, st_p, st_s_new = _layer(xp2d, xs2d, st_s, (ada, bs), (ada, 0), p, bp=bp, tp=tp, bs=bs)
        for j in range(5):
            new_p[j].append(st_p[j])
            new_s[j].append(st_s_new[j])
    y_prompt = xp2d.reshape(bp, tp, D_MODEL)
    y_sample = xs2d.reshape(bs, ts, D_MODEL)
    outs_p = [jnp.stack(a, axis=0) for a in new_p]
    outs_s = [jnp.stack(a, axis=0) for a in new_s]
    return (y_prompt, y_sample, *outs_p, *outs_s)
```
